```python
import math
import jax, jax.numpy as jnp
from jax import lax
import numpy as np

D_MODEL = 1024
BATCH = 1
SEQ = 16384
DEPTH = 2

GRID_W = 64
CTX_LEN = 256
MIX_W = D_MODEL // 2
N_BRANCH = 3
IN_WIDTH = 6 * MIX_W
S5_GROUP = 16
S5_GROUPS = MIX_W // S5_GROUP
S5_STATE = 64
SG_CHUNK = 128
SG_GROUPS = 4
SG_GROUP_W = MIX_W // SG_GROUPS
CONV_K = 3
N_EXP_GROUPS = 4
EXPERTS_PER_GROUP = 8
N_EXPERTS = N_EXP_GROUPS * EXPERTS_PER_GROUP
TOP_K = 2
D_FF_EXPERT = D_MODEL // 4
NORM_EPS = 1e-6

kernel_name = "hybrid_s5_gmlp_conv_hmoe_dit"


def rms_norm(x, g):
    xf = x.astype(jnp.float32)
    y = xf * lax.rsqrt(jnp.mean(xf * xf, axis=-1, keepdims=True) + NORM_EPS) * g.astype(jnp.float32)
    return y.astype(x.dtype)


def layer_norm(x, g, b):
    xf = x.astype(jnp.float32)
    mu = jnp.mean(xf, axis=-1, keepdims=True)
    var = jnp.mean(jnp.square(xf - mu), axis=-1, keepdims=True)
    y = (xf - mu) * lax.rsqrt(var + NORM_EPS) * g.astype(jnp.float32) + b.astype(jnp.float32)
    return y.astype(x.dtype)


def modulate(h, shift, scale):
    return h * (1.0 + scale) + shift


def _lin_comb(e1, e2):
    a1, b1 = e1
    a2, b2 = e2
    return a1 * a2, a2 * b1 + b2


def s5_discretise(p, d):
    f32 = jnp.float32
    lam = lax.complex(p["lam_re"][d].astype(f32), p["lam_im"][d].astype(f32))
    dt = jnp.exp(p["log_dt"][d].astype(f32))[:, None]
    a_bar = jnp.exp(lam * dt)
    bmat = lax.complex(p["b_re"][d].astype(f32), p["b_im"][d].astype(f32))
    b_bar = ((a_bar - 1.0) / lam)[:, :, None] * bmat
    return a_bar, b_bar


def s5_scan(u, a_bar, b_bar, h0, reverse):
    bu = jnp.einsum("blgh,gph->blgp", u.astype(jnp.float32).astype(jnp.complex64), b_bar)
    a = jnp.broadcast_to(a_bar, bu.shape)
    a_cum, hs = lax.associative_scan(_lin_comb, (a, bu), reverse=reverse, axis=1)
    return hs + a_cum * h0[:, None]


def s5_readout(u, hf, hb, p):
    f32 = jnp.float32
    bsz, length = u.shape[0], u.shape[1]
    cf = lax.complex(p["c_re"][0].astype(f32), p["c_im"][0].astype(f32))
    cb = lax.complex(p["c_re"][1].astype(f32), p["c_im"][1].astype(f32))
    y = (jnp.einsum("blgp,ghp->blgh", hf, cf) + jnp.einsum("blgp,ghp->blgh", hb, cb)).real
    y = y.reshape(bsz, length, MIX_W) + p["d_skip"].astype(f32) * u.astype(f32)
    yg = jax.nn.gelu(y).astype(u.dtype)
    return yg * jax.nn.sigmoid(yg @ p["w_glu"])


def spatial_gate(u, v, p):
    bsz, length, _ = v.shape
    vn = layer_norm(v, p["sg_ln_g"], p["sg_ln_b"])
    v5 = vn.reshape(bsz, length // SG_CHUNK, SG_CHUNK, SG_GROUPS, SG_GROUP_W)
    s = jnp.einsum("bnpgc,gqp->bnqgc", v5, p["w_sg"]) + p["b_sg"].T[None, None, :, :, None]
    return u * s.reshape(bsz, length, MIX_W)


def short_conv(z, w, n_rows):
    bsz, length, ch = z.shape
    row_len = length // n_rows
    zr = z.reshape(bsz, n_rows, row_len, ch)
    half = CONV_K // 2
    zp = jnp.pad(zr, ((0, 0), (0, 0), (half, half), (0, 0)))
    y = w[0] * zp[:, :, 0:row_len]
    for k in range(1, CONV_K):
        y = y + w[k] * zp[:, :, k:k + row_len]
    return y.reshape(bsz, length, ch)


def mix_sequence(h, proj, hf, hb, p, n_rows):
    bsz, length, _ = h.shape
    ua, su, sv, cgb, cgc, chh = jnp.split(proj, 6, axis=-1)
    y_a = s5_readout(ua, hf, hb, p)
    y_b = spatial_gate(jax.nn.gelu(su), jax.nn.gelu(sv), p)
    y_c = cgb * short_conv(cgc * chh, p["w_conv"], n_rows)
    ys = jnp.stack([y_a, y_b, y_c], axis=2)
    br = jnp.einsum("blkm,kmd->blkd", ys, p["w_branch"])
    gates = jax.nn.sigmoid((h @ p["w_merge"] + p["b_merge"]).reshape(bsz, length, N_BRANCH, D_MODEL))
    return jnp.sum(gates * br, axis=2) @ p["w_o"]


def token_mix(h, hc, p, rows, need_ctx):
    bsz = h.shape[0]
    proj = h @ p["w_in"]
    projc = hc @ p["w_in"]
    u = proj[..., :MIX_W].reshape(bsz, h.shape[1], S5_GROUPS, S5_GROUP)
    uc = projc[..., :MIX_W].reshape(bsz, hc.shape[1], S5_GROUPS, S5_GROUP)
    a_f, b_f = s5_discretise(p, 0)
    a_b, b_b = s5_discretise(p, 1)
    z0 = jnp.zeros((bsz, S5_GROUPS, S5_STATE), jnp.complex64)
    hcf = s5_scan(uc, a_f, b_f, z0, False)
    hcb = s5_scan(uc, a_b, b_b, z0, True)
    hf = s5_scan(u, a_f, b_f, hcf[:, -1], False)
    hb = s5_scan(u, a_b, b_b, hcb[:, 0], True)
    out = mix_sequence(h, proj, hf, hb, p, rows)
    out_c = mix_sequence(hc, projc, hcf, hcb, p, 1) if need_ctx else None
    return out, out_c


def hier_moe(h, p):
    f32 = jnp.float32
    grp_prob = jax.nn.softmax((h @ p["w_grp"]).astype(f32) + p["b_grp"].astype(f32), axis=-1)
    g_p, g_idx = lax.top_k(grp_prob, 1)
    exp_logits = ((h @ p["w_exp"]).astype(f32) + p["b_exp"].astype(f32)).reshape(
        h.shape[:-1] + (N_EXP_GROUPS, EXPERTS_PER_GROUP))
    in_grp = jnp.take_along_axis(exp_logits, g_idx[..., None], axis=-2)[..., 0, :]
    e_prob = jax.nn.softmax(in_grp, axis=-1)
    e_p, e_idx = lax.top_k(e_prob, TOP_K)
    e_p = e_p / jnp.sum(e_p, axis=-1, keepdims=True)
    expert_id = g_idx * EXPERTS_PER_GROUP + e_idx
    combine = jnp.sum(jax.nn.one_hot(expert_id, N_EXPERTS, dtype=f32) * (g_p * e_p)[..., None], axis=-2)
    combine = combine.astype(h.dtype)
    out = jnp.zeros_like(h)
    for grp in range(N_EXP_GROUPS):
        sl = slice(grp * EXPERTS_PER_GROUP, (grp + 1) * EXPERTS_PER_GROUP)
        a = jnp.einsum("bld,edf->blef", h, p["w1"][sl])
        b = jnp.einsum("bld,edf->blef", h, p["w3"][sl])
        act = jax.nn.silu(a) * b * combine[..., sl, None]
        out = out + jnp.einsum("blef,efd->bld", act, p["w2"][sl])
    return out


def setup_inputs(seed: int = 0) -> dict:
    key = jax.random.key(seed)
    ks = jax.random.split(key, 40)
    f32 = jnp.float32
    D, L_, G, P, H = D_MODEL, DEPTH, S5_GROUPS, S5_STATE, S5_GROUP

    def nrm(k, shape, scale):
        return jax.random.normal(k, shape, f32) * scale

    lam_im = jnp.broadcast_to(jnp.pi * jnp.arange(P, dtype=f32), (L_, 2, G, P)) + nrm(ks[9], (L_, 2, G, P), 0.01)
    return {
        "x": nrm(ks[0], (BATCH, SEQ, D), 1.0),
        "c": nrm(ks[1], (BATCH, D), 1.0),
        "ctx": nrm(ks[2], (BATCH, CTX_LEN, D), 1.0),
        "c_ctx": nrm(ks[3], (D,), 1.0),
        "w_ada": nrm(ks[4], (L_, D, 6 * D), 0.5 * D ** -0.5),
        "b_ada": nrm(ks[5], (L_, 6 * D), 0.02),
        "g_mix": 1.0 + nrm(ks[6], (L_, D), 0.05),
        "g_ffn": 1.0 + nrm(ks[7], (L_, D), 0.05),
        "w_in": nrm(ks[8], (L_, D, IN_WIDTH), D ** -0.5),
        "lam_re": -0.5 + nrm(ks[10], (L_, 2, G, P), 0.01),
        "lam_im": lam_im,
        "log_dt": jax.random.uniform(ks[11], (L_, 2, G), f32, math.log(1e-3), math.log(1e-1)),
        "b_re": nrm(ks[12], (L_, 2, G, P, H), (2 * H) ** -0.5),
        "b_im": nrm(ks[13], (L_, 2, G, P, H), (2 * H) ** -0.5),
        "c_re": nrm(ks[14], (L_, 2, G, H, P), P ** -0.5),
        "c_im": nrm(ks[15], (L_, 2, G, H, P), P ** -0.5),
        "d_skip": nrm(ks[16], (L_, MIX_W), 1.0),
        "w_glu": nrm(ks[17], (L_, MIX_W, MIX_W), MIX_W ** -0.5),
        "sg_ln_g": 1.0 + nrm(ks[18], (L_, MIX_W), 0.05),
        "sg_ln_b": nrm(ks[19], (L_, MIX_W), 0.02),
        "w_sg": nrm(ks[20], (L_, SG_GROUPS, SG_CHUNK, SG_CHUNK), SG_CHUNK ** -0.5),
        "b_sg": 1.0 + nrm(ks[21], (L_, SG_GROUPS, SG_CHUNK), 0.1),
        "w_conv": nrm(ks[22], (L_, CONV_K, MIX_W), CONV_K ** -0.5),
        "w_branch": nrm(ks[23], (L_, N_BRANCH, MIX_W, D), MIX_W ** -0.5),
        "w_merge": nrm(ks[24], (L_, D, N_BRANCH * D), D ** -0.5),
        "b_merge": nrm(ks[25], (L_, N_BRANCH * D), 0.02),
        "w_o": nrm(ks[26], (L_, D, D), D ** -0.5),
        "w_grp": nrm(ks[27], (L_, D, N_EXP_GROUPS), D ** -0.5),
        "b_grp": nrm(ks[28], (L_, N_EXP_GROUPS), 0.01),
        "w_exp": nrm(ks[29], (L_, D, N_EXPERTS), D ** -0.5),
        "b_exp": nrm(ks[30], (L_, N_EXPERTS), 0.01),
        "w1": nrm(ks[31], (L_, N_EXPERTS, D, D_FF_EXPERT), D ** -0.5),
        "w3": nrm(ks[32], (L_, N_EXPERTS, D, D_FF_EXPERT), D ** -0.5),
        "w2": nrm(ks[33], (L_, N_EXPERTS, D_FF_EXPERT, D), D_FF_EXPERT ** -0.5),
        "g_final": 1.0 + nrm(ks[34], (D,), 0.05),
    }


def reference(x, c, ctx, c_ctx, w_ada, b_ada, g_mix, g_ffn, w_in, lam_re, lam_im, log_dt,
              b_re, b_im, c_re, c_im, d_skip, w_glu, sg_ln_g, sg_ln_b, w_sg, b_sg, w_conv,
              w_branch, w_merge, b_merge, w_o, w_grp, b_grp, w_exp, b_exp, w1, w3, w2, g_final):
    rows = x.shape[1] // GRID_W
    xc = ctx
    for l in range(DEPTH):
        need_ctx = l < DEPTH - 1
        p = dict(w_in=w_in[l], lam_re=lam_re[l], lam_im=lam_im[l], log_dt=log_dt[l],
                 b_re=b_re[l], b_im=b_im[l], c_re=c_re[l], c_im=c_im[l], d_skip=d_skip[l],
                 w_glu=w_glu[l], sg_ln_g=sg_ln_g[l], sg_ln_b=sg_ln_b[l], w_sg=w_sg[l], b_sg=b_sg[l],
                 w_conv=w_conv[l], w_branch=w_branch[l], w_merge=w_merge[l], b_merge=b_merge[l],
                 w_o=w_o[l], w_grp=w_grp[l], b_grp=b_grp[l], w_exp=w_exp[l], b_exp=b_exp[l],
                 w1=w1[l], w3=w3[l], w2=w2[l])
        sh1, sc1, ga1, sh2, sc2, ga2 = jnp.split(jax.nn.silu(c) @ w_ada[l] + b_ada[l], 6, axis=-1)
        csh1, csc1, cga1, csh2, csc2, cga2 = jnp.split(jax.nn.silu(c_ctx) @ w_ada[l] + b_ada[l], 6, axis=-1)
        h = modulate(rms_norm(x, g_mix[l]), sh1[:, None], sc1[:, None])
        hc = modulate(rms_norm(xc, g_mix[l]), csh1, csc1)
        out, out_c = token_mix(h, hc, p, rows, need_ctx)
        x = x + ga1[:, None] * out
        h2 = modulate(rms_norm(x, g_ffn[l]), sh2[:, None], sc2[:, None])
        x = x + ga2[:, None] * hier_moe(h2, p)
        if need_ctx:
            xc = xc + cga1 * out_c
            hc2 = modulate(rms_norm(xc, g_ffn[l]), csh2, csc2)
            xc = xc + cga2 * hier_moe(hc2, p)
    return rms_norm(x, g_final)
```

```python
import functools

import jax
import jax.numpy as jnp
from jax import lax
from jax.experimental import pallas as pl
from jax.experimental.pallas import tpu as pltpu

F32 = jnp.float32
BF16 = jnp.bfloat16
HIGHEST = lax.Precision.HIGHEST

D_MODEL = 1024
GRID_W = 64
MIX_W = D_MODEL // 2
N_BRANCH = 3
S5_GROUP = 16
S5_GROUPS = MIX_W // S5_GROUP
S5_STATE = 64
SG_CHUNK = 128
SG_GROUPS = 4
SG_GROUP_W = MIX_W // SG_GROUPS
CONV_K = 3
N_EXP_GROUPS = 4
EXPERTS_PER_GROUP = 8
N_EXPERTS = N_EXP_GROUPS * EXPERTS_PER_GROUP
D_FF_EXPERT = D_MODEL // 4
NORM_EPS = 1e-6

LANES = 128
SUBLANES = 8
S5_T = 16
S5_CW = S5_T * S5_GROUP
S5_GB = 4
TM = 256
MOE_TM = 1280
MOE_SUB = 256
VMEM_LIMIT = 56 * 1024 * 1024


def _dot(a, b):
    return jnp.dot(a, b, preferred_element_type=F32)


def _dot_hi(a, b):
    return jnp.dot(a, b, preferred_element_type=F32, precision=HIGHEST)


def _rms(x, g):
    return x * lax.rsqrt(jnp.mean(x * x, axis=-1, keepdims=True) + NORM_EPS) * g


def _const_spec(shape):
    nd = len(shape)
    return pl.BlockSpec(shape, lambda *_: (0,) * nd, pipeline_mode=pl.Buffered(1))


def _ada_kernel(cc_ref, w_ref, b_ref, o_ref):
    cc = cc_ref[...]
    o_ref[...] = _dot_hi(cc * jax.nn.sigmoid(cc), w_ref[...]) + b_ref[...]


def _ada(cc, w_ada, b_ada):
    depth = w_ada.shape[0]
    nblk = 6
    return pl.pallas_call(
        _ada_kernel,
        grid=(depth, nblk),
        in_specs=[
            pl.BlockSpec((8, D_MODEL), lambda l, j: (0, 0)),
            pl.BlockSpec((None, D_MODEL, D_MODEL), lambda l, j: (l, 0, j)),
            pl.BlockSpec((None, 1, D_MODEL), lambda l, j: (l, 0, j)),
        ],
        out_specs=pl.BlockSpec((None, 8, D_MODEL), lambda l, j: (l, 0, j)),
        out_shape=jax.ShapeDtypeStruct((depth, 8, 6 * D_MODEL), F32),
        name="ada",
    )(cc, w_ada, b_ada.reshape(depth, 1, 6 * D_MODEL))


def _mod_spec():
    return pl.BlockSpec((None, 1, 6 * D_MODEL), lambda i: (jnp.minimum(i, 1), 0, 0))


def _u_kernel(z_ref, mod_ref, g_ref, w_ref, u_ref):
    mod = mod_ref[...]
    h = _rms(z_ref[...], g_ref[...]) * (1.0 + mod[:, D_MODEL:2 * D_MODEL]) + mod[:, 0:D_MODEL]
    u_ref[...] = _dot(h.astype(BF16), w_ref[...])


def _u_proj(z, mod, g_mix, w_u):
    n = z.shape[0]
    return pl.pallas_call(
        _u_kernel,
        grid=(n // TM,),
        in_specs=[
            pl.BlockSpec((TM, D_MODEL), lambda i: (i, 0)),
            _mod_spec(),
            _const_spec((1, D_MODEL)),
            _const_spec((D_MODEL, MIX_W)),
        ],
        out_specs=pl.BlockSpec((TM, MIX_W), lambda i: (i, 0)),
        out_shape=jax.ShapeDtypeStruct((n, MIX_W), F32),
        name="u_proj",
    )(z, mod, g_mix, w_u)


def _s5_kernel(n_ctx_chunks, ug_ref, m_ref, bp_ref, cp_ref, a_ref, y_ref, v_ref, sp_ref):
    gb, n_chunks, _ = ug_ref.shape
    for g in range(gb):
        v_ref[g] = _dot_hi(ug_ref[g], bp_ref[g])

    blk = (gb, SUBLANES, LANES)
    a = a_ref[...]
    ar_f, ai_f, ar_b, ai_b = [jnp.broadcast_to(a[:, q:q + 1, :], blk) for q in range(4)]
    half = S5_STATE
    n_blocks = n_chunks // SUBLANES
    n_ctx_blocks = n_ctx_chunks // SUBLANES
    sub = lax.broadcasted_iota(jnp.int32, blk, 1)

    def step(k, carry):
        s_f, s_b = carry
        kb = jnp.where(k < n_ctx_blocks, n_ctx_blocks - 1 - k, n_blocks - 1 + n_ctx_blocks - k)
        rf = pl.multiple_of(k * SUBLANES, SUBLANES)
        rb = pl.multiple_of(kb * SUBLANES, SUBLANES)
        v_f = v_ref[:, pl.ds(rf, SUBLANES), 0:LANES]
        v_b = v_ref[:, pl.ds(rb, SUBLANES), LANES:2 * LANES]
        o_f = jnp.zeros(blk, F32)
        o_b = jnp.zeros(blk, F32)
        for j in range(SUBLANES):
            jb = SUBLANES - 1 - j
            o_f = jnp.where(sub == j, s_f, o_f)
            o_b = jnp.where(sub == jb, s_b, o_b)
            s_f = ar_f * s_f + ai_f * pltpu.roll(s_f, half, 2) + jnp.broadcast_to(v_f[:, j:j + 1, :], blk)
            s_b = ar_b * s_b + ai_b * pltpu.roll(s_b, half, 2) + jnp.broadcast_to(v_b[:, jb:jb + 1, :], blk)
        sp_ref[:, pl.ds(rf, SUBLANES), 0:LANES] = o_f
        sp_ref[:, pl.ds(rb, SUBLANES), LANES:2 * LANES] = o_b
        return s_f, s_b

    zero = jnp.zeros(blk, F32)
    lax.fori_loop(0, n_blocks, step, (zero, zero))

    for g in range(gb):
        y_ref[g] = _dot_hi(ug_ref[g], m_ref[g]) + _dot_hi(sp_ref[g], cp_ref[g])


def _s5_apply(ug, m, bp, cp, a16, n_ctx_chunks):
    n_groups, n_chunks, cw = ug.shape
    blk = lambda shape: pl.BlockSpec(shape, lambda i: (i, 0, 0))
    return pl.pallas_call(
        functools.partial(_s5_kernel, n_ctx_chunks),
        grid=(n_groups // S5_GB,),
        in_specs=[
            blk((S5_GB, n_chunks, cw)),
            blk((S5_GB, cw, cw)),
            blk((S5_GB, cw, 2 * LANES)),
            blk((S5_GB, 2 * LANES, cw)),
            blk((S5_GB, 4, LANES)),
        ],
        out_specs=blk((S5_GB, n_chunks, cw)),
        out_shape=jax.ShapeDtypeStruct((n_groups, n_chunks, cw), F32),
        scratch_shapes=[
            pltpu.VMEM((S5_GB, n_chunks, 2 * LANES), F32),
            pltpu.VMEM((S5_GB, n_chunks, 2 * LANES), F32),
        ],
        compiler_params=pltpu.CompilerParams(vmem_limit_bytes=VMEM_LIMIT),
        name="s5_scan",
    )(ug, m, bp, cp, a16)


def _s5_operators(lam_re, lam_im, log_dt, b_re, b_im, c_re, c_im, d_skip):
    t = S5_T
    k = jnp.arange(t + 1, dtype=F32)[:, None, None]
    pw, bb, cc = [], [], []
    for d in range(2):
        lam = lax.complex(lam_re[d], lam_im[d])
        dt = jnp.exp(log_dt[d])[:, None]
        ldt = lam * dt
        mag = jnp.exp(k * ldt.real[None])
        ang = k * ldt.imag[None]
        pw.append(lax.complex(mag * jnp.cos(ang), mag * jnp.sin(ang)))
        a_bar = jnp.exp(ldt)
        bb.append(((a_bar - 1.0) / lam)[:, :, None] * lax.complex(b_re[d], b_im[d]))
        cc.append(lax.complex(c_re[d], c_im[d]))

    def taps(d):
        return jnp.einsum("gip,lgp,gpj->lgij", cc[d], pw[d][:t], bb[d], precision=HIGHEST).real

    kf, kb = taps(0), taps(1)
    r = jnp.arange(t)[:, None]
    s = jnp.arange(t)[None, :]
    mf = jnp.where((s >= r)[:, :, None, None, None], kf[jnp.clip(s - r, 0, t - 1)], 0.0)
    mb = jnp.where((r >= s)[:, :, None, None, None], kb[jnp.clip(r - s, 0, t - 1)], 0.0)
    m = jnp.transpose(mf + mb, (2, 0, 4, 1, 3)).reshape(S5_GROUPS, S5_CW, S5_CW)
    m = m + jnp.eye(S5_CW, dtype=F32)[None] * jnp.tile(d_skip.reshape(S5_GROUPS, 1, S5_GROUP), (1, 1, t))

    bpf = pw[0][t - 1 - jnp.arange(t)][:, :, None, :] * jnp.transpose(bb[0], (0, 2, 1))[None]
    bpb = pw[1][jnp.arange(t)][:, :, None, :] * jnp.transpose(bb[1], (0, 2, 1))[None]
    to_rows = lambda x: jnp.transpose(x, (1, 0, 2, 3)).reshape(S5_GROUPS, S5_CW, S5_STATE)
    bp = jnp.concatenate([to_rows(bpf.real), to_rows(bpf.imag), to_rows(bpb.real), to_rows(bpb.imag)], axis=-1)

    xf = pw[0][1 + jnp.arange(t)][:, :, None, :] * cc[0][None]
    xb = pw[1][t - jnp.arange(t)][:, :, None, :] * cc[1][None]
    to_cols = lambda x: jnp.transpose(x, (1, 3, 0, 2)).reshape(S5_GROUPS, S5_STATE, S5_CW)
    cp = jnp.concatenate([to_cols(xf.real), -to_cols(xf.imag), to_cols(xb.real), -to_cols(xb.imag)], axis=1)

    def decay(d):
        re, im = pw[d][t].real, pw[d][t].imag
        return [jnp.concatenate([re, re], -1), jnp.concatenate([-im, im], -1)]

    a16 = jnp.stack(decay(0) + decay(1), axis=1)
    return m, bp, cp, a16


def _mixer_kernel(n_ctx_tiles, z_ref, y_ref, mod_ref, gmix_ref, gffn_ref, win_ref, wglu_ref, lng_ref,
                  lnb_ref, wsg_ref, bsg_ref, wconv_ref, wbr_ref, wmg_ref, bmg_ref, wo_ref, wr_ref,
                  br_ref, z1_ref, h2_ref, comb_ref):
    d = D_MODEL
    z = z_ref[...]
    mod = mod_ref[...]
    sh1, sc1, ga1 = mod[:, 0:d], mod[:, d:2 * d], mod[:, 2 * d:3 * d]
    sh2, sc2 = mod[:, 3 * d:4 * d], mod[:, 4 * d:5 * d]
    h = _rms(z, gmix_ref[...]) * (1.0 + sc1) + sh1
    hb = h.astype(BF16)
    proj = _dot(hb, win_ref[...])
    su, sv, cgb, cgc, chh = [proj[:, k * MIX_W:(k + 1) * MIX_W] for k in range(5)]

    yg = jax.nn.gelu(y_ref[...])
    y_a = yg * jax.nn.sigmoid(_dot(yg.astype(BF16), wglu_ref[...]))

    gv = jax.nn.gelu(sv)
    mu = jnp.mean(gv, axis=-1, keepdims=True)
    var = jnp.mean(jnp.square(gv - mu), axis=-1, keepdims=True)
    vn = ((gv - mu) * lax.rsqrt(var + NORM_EPS) * lng_ref[...] + lnb_ref[...]).astype(BF16)
    bsg = bsg_ref[...]
    chunks = []
    for n in range(TM // SG_CHUNK):
        rows = slice(n * SG_CHUNK, (n + 1) * SG_CHUNK)
        cols = [_dot(wsg_ref[g], vn[rows, g * SG_GROUP_W:(g + 1) * SG_GROUP_W]) for g in range(SG_GROUPS)]
        chunks.append(jnp.concatenate(cols, axis=1) + bsg)
    y_b = jax.nn.gelu(su) * jnp.concatenate(chunks, axis=0)

    zc = cgc * chh
    row_len = jnp.where(pl.program_id(0) < n_ctx_tiles, TM, GRID_W)
    pos = lax.broadcasted_iota(jnp.int32, (TM, 1), 0) % row_len
    prev = jnp.where(pos == 0, 0.0, pltpu.roll(zc, 1, 0))
    nxt = jnp.where(pos == row_len - 1, 0.0, pltpu.roll(zc, TM - 1, 0))
    wc = wconv_ref[...]
    y_c = cgb * (wc[0:1] * prev + wc[1:2] * zc + wc[2:3] * nxt)

    merged = jnp.zeros((TM, d), F32)
    for k, y_k in enumerate((y_a, y_b, y_c)):
        gate = jax.nn.sigmoid(_dot(hb, wmg_ref[:, k * d:(k + 1) * d]) + bmg_ref[:, k * d:(k + 1) * d])
        merged = merged + gate * _dot(y_k.astype(BF16), wbr_ref[k])
    z1 = z + ga1 * _dot(merged.astype(BF16), wo_ref[...])
    z1_ref[...] = z1

    h2 = _rms(z1, gffn_ref[...]) * (1.0 + sc2) + sh2
    h2_ref[...] = h2.astype(BF16)
    logits = _dot_hi(h2, wr_ref[...]) + br_ref[...]
    lane = lax.broadcasted_iota(jnp.int32, (TM, LANES), 1)
    big = jnp.int32(LANES)
    neg = jnp.float32(-jnp.inf)
    is_grp = (lane >= N_EXPERTS) & (lane < N_EXPERTS + N_EXP_GROUPS)
    gl = jnp.where(is_grp, logits, neg)
    ge = jnp.exp(gl - jnp.max(gl, axis=-1, keepdims=True))
    gp = ge / jnp.sum(ge, axis=-1, keepdims=True)
    g_p = jnp.max(gp, axis=-1, keepdims=True)
    g_idx = jnp.min(jnp.where(is_grp & (gp == g_p), lane, big), axis=-1, keepdims=True) - N_EXPERTS
    in_grp = (lane >= g_idx * EXPERTS_PER_GROUP) & (lane < (g_idx + 1) * EXPERTS_PER_GROUP)
    el = jnp.where(in_grp, logits, neg)
    ee = jnp.exp(el - jnp.max(el, axis=-1, keepdims=True))
    ep = ee / jnp.sum(ee, axis=-1, keepdims=True)
    p1 = jnp.max(ep, axis=-1, keepdims=True)
    i1 = jnp.min(jnp.where(in_grp & (ep == p1), lane, big), axis=-1, keepdims=True)
    rest = in_grp & (lane != i1)
    ep2 = jnp.where(rest, ep, -1.0)
    p2 = jnp.max(ep2, axis=-1, keepdims=True)
    i2 = jnp.min(jnp.where(rest & (ep2 == p2), lane, big), axis=-1, keepdims=True)
    tot = p1 + p2
    comb_ref[...] = jnp.where(lane == i1, g_p * (p1 / tot), 0.0) + jnp.where(lane == i2, g_p * (p2 / tot), 0.0)


def _mixer(z, ytm, mod, p, n_ctx_tiles):
    n = z.shape[0]
    d = D_MODEL
    tile = lambda w: pl.BlockSpec((TM, w), lambda i: (i, 0))
    return pl.pallas_call(
        functools.partial(_mixer_kernel, n_ctx_tiles),
        grid=(n // TM,),
        in_specs=[
            tile(d), tile(MIX_W), _mod_spec(),
            _const_spec((1, d)), _const_spec((1, d)),
            _const_spec((d, 5 * MIX_W)), _const_spec((MIX_W, MIX_W)),
            _const_spec((1, MIX_W)), _const_spec((1, MIX_W)),
            _const_spec((SG_GROUPS, SG_CHUNK, SG_CHUNK)), _const_spec((SG_CHUNK, MIX_W)),
            _const_spec((CONV_K, MIX_W)), _const_spec((N_BRANCH, MIX_W, d)),
            _const_spec((d, N_BRANCH * d)), _const_spec((1, N_BRANCH * d)),
            _const_spec((d, d)), _const_spec((d, LANES)), _const_spec((1, LANES)),
        ],
        out_specs=[tile(d), tile(d), tile(LANES)],
        out_shape=[
            jax.ShapeDtypeStruct((n, d), F32),
            jax.ShapeDtypeStruct((n, d), BF16),
            jax.ShapeDtypeStruct((n, LANES), F32),
        ],
        compiler_params=pltpu.CompilerParams(vmem_limit_bytes=VMEM_LIMIT),
        name="mixer",
    )(z, ytm, mod, p["g_mix"], p["g_ffn"], p["w_in5"], p["w_glu"], p["sg_ln_g"], p["sg_ln_b"],
      p["w_sg"], p["b_sg_full"], p["w_conv"], p["w_branch"], p["w_merge"], p["b_merge"], p["w_o"],
      p["w_router"], p["b_router"])


def _moe_kernel(h2_ref, comb_ref, w1_ref, w3_ref, w2_ref, out_ref):
    grp = pl.program_id(1)

    @pl.when(grp == 0)
    def _():
        out_ref[...] = jnp.zeros_like(out_ref)

    f = D_FF_EXPERT

    def sub_tile(t, carry):
        r = pl.multiple_of(t * MOE_SUB, MOE_SUB)
        hb = h2_ref[pl.ds(r, MOE_SUB), :]
        act = jax.nn.silu(_dot(hb, w1_ref[...])) * _dot(hb, w3_ref[...])
        cm = comb_ref[pl.ds(r, MOE_SUB), :]
        lane = lax.broadcasted_iota(jnp.int32, cm.shape, 1)
        parts = []
        for e in range(EXPERTS_PER_GROUP):
            c_e = jnp.sum(jnp.where(lane == grp * EXPERTS_PER_GROUP + e, cm, 0.0), axis=-1, keepdims=True)
            parts.append((act[:, e * f:(e + 1) * f] * c_e).astype(BF16))
        out_ref[pl.ds(r, MOE_SUB), :] += _dot(jnp.concatenate(parts, axis=1), w2_ref[...])
        return carry

    lax.fori_loop(0, MOE_TM // MOE_SUB, sub_tile, 0)


def _moe_dense(h2b, comb, w1g, w3g, w2g):
    n = h2b.shape[0]
    d = D_MODEL
    gf = EXPERTS_PER_GROUP * D_FF_EXPERT
    return pl.pallas_call(
        _moe_kernel,
        grid=(n // MOE_TM, N_EXP_GROUPS),
        in_specs=[
            pl.BlockSpec((MOE_TM, d), lambda i, g: (i, 0)),
            pl.BlockSpec((MOE_TM, LANES), lambda i, g: (i, 0)),
            pl.BlockSpec((None, d, gf), lambda i, g: (g, 0, 0)),
            pl.BlockSpec((None, d, gf), lambda i, g: (g, 0, 0)),
            pl.BlockSpec((None, gf, d), lambda i, g: (g, 0, 0)),
        ],
        out_specs=pl.BlockSpec((MOE_TM, d), lambda i, g: (i, 0)),
        out_shape=jax.ShapeDtypeStruct((n, d), F32),
        compiler_params=pltpu.CompilerParams(
            dimension_semantics=("parallel", "arbitrary"), vmem_limit_bytes=VMEM_LIMIT),
        name="moe_dense",
    )(h2b, comb, w1g, w3g, w2g)


def _residual_kernel(final, z1_ref, moe_ref, mod_ref, g_ref, o_ref):
    ga2 = mod_ref[...][:, 5 * D_MODEL:6 * D_MODEL]
    z2 = z1_ref[...] + ga2 * moe_ref[...]
    o_ref[...] = _rms(z2, g_ref[...]) if final else z2


def _residual(z1, moe, mod, g_final, final):
    n = z1.shape[0]
    tile = pl.BlockSpec((TM, D_MODEL), lambda i: (i, 0))
    return pl.pallas_call(
        functools.partial(_residual_kernel, final),
        grid=(n // TM,),
        in_specs=[tile, tile, _mod_spec(), _const_spec((1, D_MODEL))],
        out_specs=tile,
        out_shape=jax.ShapeDtypeStruct((n, D_MODEL), F32),
        name="residual",
    )(z1, moe, mod, g_final)


def kernel(x, c, ctx, c_ctx, w_ada, b_ada, g_mix, g_ffn, w_in, lam_re, lam_im, log_dt, b_re, b_im, c_re, c_im, d_skip, w_glu, sg_ln_g, sg_ln_b, w_sg, b_sg, w_conv, w_branch, w_merge, b_merge, w_o, w_grp, b_grp, w_exp, b_exp, w1, w3, w2, g_final):
    depth = w_ada.shape[0]
    bsz, seq, d = x.shape
    n_ctx = ctx.shape[1]
    assert bsz == 1 and d == D_MODEL and n_ctx % TM == 0 and seq % TM == 0
    n = n_ctx + seq
    assert n % MOE_TM == 0 and n_ctx % (S5_T * SUBLANES) == 0 and seq % (S5_T * SUBLANES) == 0
    n_chunks = n // S5_T

    z = jnp.concatenate([ctx[0], x[0]], axis=0)
    cc = jnp.zeros((8, d), F32).at[0].set(c_ctx).at[1].set(c[0])
    mods = _ada(cc, w_ada, b_ada)[:, :2].reshape(depth, 2, 1, 6 * d)

    for l in range(depth):
        gf = EXPERTS_PER_GROUP * D_FF_EXPERT
        pad = LANES - N_EXPERTS - N_EXP_GROUPS
        p = dict(
            g_mix=g_mix[l].reshape(1, d), g_ffn=g_ffn[l].reshape(1, d),
            w_in5=w_in[l][:, MIX_W:].astype(BF16), w_glu=w_glu[l].astype(BF16),
            sg_ln_g=sg_ln_g[l].reshape(1, MIX_W), sg_ln_b=sg_ln_b[l].reshape(1, MIX_W),
            w_sg=w_sg[l].astype(BF16),
            b_sg_full=jnp.repeat(b_sg[l].T, SG_GROUP_W, axis=1),
            w_conv=w_conv[l], w_branch=w_branch[l].astype(BF16), w_merge=w_merge[l].astype(BF16),
            b_merge=b_merge[l].reshape(1, N_BRANCH * d), w_o=w_o[l].astype(BF16),
            w_router=jnp.pad(jnp.concatenate([w_exp[l], w_grp[l]], axis=1), ((0, 0), (0, pad))),
            b_router=jnp.pad(jnp.concatenate([b_exp[l], b_grp[l]]), (0, pad)).reshape(1, LANES),
        )
        mod = mods[l]

        u = _u_proj(z, mod, p["g_mix"], w_in[l][:, :MIX_W].astype(BF16))
        ug = u.reshape(n_chunks, S5_T, S5_GROUPS, S5_GROUP).transpose(2, 0, 1, 3).reshape(S5_GROUPS, n_chunks, S5_CW)
        m, bp, cp, a16 = _s5_operators(lam_re[l], lam_im[l], log_dt[l], b_re[l], b_im[l], c_re[l], c_im[l], d_skip[l])
        yg = _s5_apply(ug, m, bp, cp, a16, n_ctx // S5_T)
        ytm = yg.reshape(S5_GROUPS, n_chunks, S5_T, S5_GROUP).transpose(1, 2, 0, 3).reshape(n, MIX_W)

        z1, h2b, comb = _mixer(z, ytm, mod, p, n_ctx // TM)

        to_cols = lambda w: w.reshape(N_EXP_GROUPS, EXPERTS_PER_GROUP, d, D_FF_EXPERT).transpose(0, 2, 1, 3).reshape(N_EXP_GROUPS, d, gf)
        moe = _moe_dense(h2b, comb, to_cols(w1[l]).astype(BF16), to_cols(w3[l]).astype(BF16),
                         w2[l].reshape(N_EXP_GROUPS, gf, d).astype(BF16))
        last = l == depth - 1
        z = _residual(z1, moe, mod, g_final.reshape(1, d) if last else p["g_ffn"], last)

    return z[n_ctx:].reshape(bsz, seq, d)
```

```python
import functools

import jax
import jax.numpy as jnp
from jax import lax
from jax.experimental import pallas as pl
from jax.experimental.pallas import tpu as pltpu

F32 = jnp.float32
BF16 = jnp.bfloat16
HIGHEST = lax.Precision.HIGHEST

D_MODEL = 1024
GRID_W = 64
MIX_W = D_MODEL // 2
N_BRANCH = 3
S5_GROUP = 16
S5_GROUPS = MIX_W // S5_GROUP
S5_STATE = 64
SG_CHUNK = 128
SG_GROUPS = 4
SG_GROUP_W = MIX_W // SG_GROUPS
CONV_K = 3
N_EXP_GROUPS = 4
EXPERTS_PER_GROUP = 8
N_EXPERTS = N_EXP_GROUPS * EXPERTS_PER_GROUP
D_FF_EXPERT = D_MODEL // 4
NORM_EPS = 1e-6

LANES = 128
SUBLANES = 8
S5_T = 16
S5_CW = S5_T * S5_GROUP
S5_GB = 4
CTX_ALIGN = 512
TM = 256
MOE_TM = 1536
MOE_SUB = 256
VMEM_LIMIT = 56 * 1024 * 1024


def _dot(a, b):
    return jnp.dot(a, b, preferred_element_type=F32)


def _dot_hi(a, b):
    return jnp.dot(a, b, preferred_element_type=F32, precision=HIGHEST)


def _rms(x, g):
    return x * lax.rsqrt(jnp.mean(x * x, axis=-1, keepdims=True) + NORM_EPS) * g


def _const_spec(shape):
    nd = len(shape)
    return pl.BlockSpec(shape, lambda *_: (0,) * nd, pipeline_mode=pl.Buffered(1))


def _ada_kernel(cc_ref, w_ref, b_ref, o_ref):
    cc = cc_ref[...]
    o_ref[...] = _dot_hi(cc * jax.nn.sigmoid(cc), w_ref[...]) + b_ref[...]


def _ada(cc, w_ada, b_ada):
    depth = w_ada.shape[0]
    nblk = 6
    return pl.pallas_call(
        _ada_kernel,
        grid=(depth, nblk),
        in_specs=[
            pl.BlockSpec((8, D_MODEL), lambda l, j: (0, 0)),
            pl.BlockSpec((None, D_MODEL, D_MODEL), lambda l, j: (l, 0, j)),
            pl.BlockSpec((None, 1, D_MODEL), lambda l, j: (l, 0, j)),
        ],
        out_specs=pl.BlockSpec((None, 8, D_MODEL), lambda l, j: (l, 0, j)),
        out_shape=jax.ShapeDtypeStruct((depth, 8, 6 * D_MODEL), F32),
        name="ada",
    )(cc, w_ada, b_ada.reshape(depth, 1, 6 * D_MODEL))


def _mod_spec(n_ctx_tiles):
    return pl.BlockSpec((None, 1, 6 * D_MODEL), lambda i: (jnp.where(i < n_ctx_tiles, 0, 1), 0, 0))


def _u_kernel(z_ref, mod_ref, g_ref, w_ref, u_ref):
    mod = mod_ref[...]
    h = _rms(z_ref[...], g_ref[...]) * (1.0 + mod[:, D_MODEL:2 * D_MODEL]) + mod[:, 0:D_MODEL]
    u_ref[...] = _dot(h.astype(BF16), w_ref[...])


def _u_proj(z, mod, g_mix, w_u, n_ctx_tiles):
    n = z.shape[0]
    return pl.pallas_call(
        _u_kernel,
        grid=(n // TM,),
        in_specs=[
            pl.BlockSpec((TM, D_MODEL), lambda i: (i, 0)),
            _mod_spec(n_ctx_tiles),
            _const_spec((1, D_MODEL)),
            _const_spec((D_MODEL, MIX_W)),
        ],
        out_specs=pl.BlockSpec((TM, MIX_W), lambda i: (i, 0)),
        out_shape=jax.ShapeDtypeStruct((n, MIX_W), F32),
        name="u_proj",
    )(z, mod, g_mix, w_u)


def _s5_prep_kernel(n_seg_chunks, par_ref, bt_ref, ct_ref, dsk_ref, m_ref, bp_ref, cp_ref, tab_ref, pow_ref):
    t = S5_T
    par = par_ref[...]
    lo = lax.broadcasted_iota(jnp.int32, (1, LANES), 1) < S5_STATE
    sgn = jnp.where(lo, -1.0, 1.0)

    def rows(n):
        return lax.broadcasted_iota(jnp.int32, (n, 1), 0).astype(F32)

    def expand(pw_rows, vec_rows):
        pr, pi = pw_rows
        vr, vi = vec_rows
        pr, pi = pr[:, None, :], pi[:, None, :]
        vr, vi = vr[None, :, :], vi[None, :, :]
        re = (pr * vr - pi * vi).reshape(-1, LANES)
        im = (pr * vi + pi * vr).reshape(-1, LANES)
        return re, im

    mt = None
    bp_cols, cp_rows, tab_rows, pow_rows = [], [], [], []
    for d in range(2):
        lre, lim = par[2 * d:2 * d + 1], par[2 * d + 1:2 * d + 2]
        dt = jnp.exp(par[4 + d:5 + d])
        xr, xi = lre * dt, lim * dt

        def cpow(m):
            mag = jnp.exp(m * xr)
            return mag * jnp.cos(m * xi), mag * jnp.sin(m * xi)

        ar, ai = cpow(1.0)
        den = lre * lre + lim * lim
        qr = ((ar - 1.0) * lre + ai * lim) / den
        qi = (ai * lre - (ar - 1.0) * lim) / den
        br, bi = bt_ref[2 * d], bt_ref[2 * d + 1]
        bb = (qr * br - qi * bi, qr * bi + qi * br)
        cc = (ct_ref[2 * d], ct_ref[2 * d + 1])

        inj_pow = cpow(t - 1.0 - rows(t)) if d == 0 else cpow(rows(t))
        re, im = expand(inj_pow, bb)
        bp_cols.append(jnp.where(lo, re, im))

        out_pow = cpow(rows(t) + 1.0) if d == 0 else cpow(t - rows(t))
        re, im = expand(out_pow, cc)
        cp_rows.append(jnp.where(lo, re, -im).T)

        lag_pow = cpow(rows(t)) if d == 0 else cpow(t - 1.0 - rows(t))
        re, im = expand(lag_pow, cc)
        ctab = jnp.where(lo, re, -im)
        bcat = jnp.broadcast_to(jnp.where(lo, bb[0], bb[1])[None], (t, S5_GROUP, LANES)).reshape(-1, LANES)
        k256 = lax.dot_general(ctab, bcat, (((1,), (1,)), ((), ())), precision=HIGHEST,
                               preferred_element_type=F32)
        col_blk = lax.broadcasted_iota(jnp.int32, (S5_CW, S5_CW), 1) // S5_GROUP
        acc = jnp.zeros((S5_CW, S5_CW), F32)
        for r in range(t):
            sh = r * S5_GROUP if d == 0 else (t - 1 - r) * S5_GROUP
            if sh == 0:
                shifted = k256
            elif d == 0:
                shifted = jnp.concatenate([jnp.zeros((sh, S5_CW), F32), k256[:S5_CW - sh]], axis=0)
            else:
                shifted = jnp.concatenate([k256[sh:], jnp.zeros((sh, S5_CW), F32)], axis=0)
            acc = acc + jnp.where(col_blk == r, shifted, 0.0)
        mt = acc if mt is None else mt + acc

        for m in (float(t), float(t * n_seg_chunks)):
            re, im = cpow(m)
            tab_rows += [re, sgn * im]
        kk = rows(n_seg_chunks)
        re, im = cpow(t * kk) if d == 0 else cpow(t * (n_seg_chunks - 1.0 - kk))
        pow_rows += [re, sgn * im]

    eye = (lax.broadcasted_iota(jnp.int32, (S5_CW, S5_CW), 0) == lax.broadcasted_iota(jnp.int32, (S5_CW, S5_CW), 1))
    mt = mt + jnp.where(eye, dsk_ref[...], 0.0)
    m_ref[...] = mt.T.astype(BF16)
    bp_ref[...] = jnp.concatenate(bp_cols, axis=1).astype(BF16)
    cp_ref[...] = jnp.concatenate(cp_rows, axis=0).astype(BF16)
    tab_ref[...] = jnp.concatenate(tab_rows, axis=0)
    for q in range(4):
        pow_ref[q] = pow_rows[q]


def _s5_prep(lam_re, lam_im, log_dt, b_re, b_im, c_re, c_im, d_skip, n_seg_chunks):
    depth = lam_re.shape[0]
    g, p_, h = S5_GROUPS, S5_STATE, S5_GROUP
    dup = lambda a: jnp.concatenate([a, a], axis=-1)
    dirs_last = lambda a: jnp.moveaxis(a, 1, 2)
    lam = jnp.stack([lam_re, lam_im], axis=3)
    lam = dirs_last(lam).reshape(depth, g, 4, p_)
    ldt = jnp.broadcast_to(dirs_last(log_dt[..., None]), (depth, g, 2, p_))
    par = dup(jnp.concatenate([lam, ldt, jnp.zeros((depth, g, 2, p_), F32)], axis=2))
    bt = jnp.stack([b_re, b_im], axis=3)
    bt = dup(jnp.swapaxes(dirs_last(bt), -1, -2).reshape(depth, g, 4, h, p_))
    ct = dup(dirs_last(jnp.stack([c_re, c_im], axis=3)).reshape(depth, g, 4, h, p_))
    dsk = jnp.tile(d_skip.reshape(depth, g, 1, h), (1, 1, 1, S5_T))

    blk = lambda *shape: pl.BlockSpec((None, None) + shape, lambda l, j: (l, j) + (0,) * len(shape))
    return pl.pallas_call(
        functools.partial(_s5_prep_kernel, n_seg_chunks),
        grid=(depth, g),
        in_specs=[blk(8, LANES), blk(4, h, LANES), blk(4, h, LANES), blk(1, S5_CW)],
        out_specs=[blk(S5_CW, S5_CW), blk(S5_CW, 2 * LANES), blk(2 * LANES, S5_CW), blk(8, LANES),
                   blk(4, n_seg_chunks, LANES)],
        out_shape=[
            jax.ShapeDtypeStruct((depth, g, S5_CW, S5_CW), BF16),
            jax.ShapeDtypeStruct((depth, g, S5_CW, 2 * LANES), BF16),
            jax.ShapeDtypeStruct((depth, g, 2 * LANES, S5_CW), BF16),
            jax.ShapeDtypeStruct((depth, g, 8, LANES), F32),
            jax.ShapeDtypeStruct((depth, g, 4, n_seg_chunks, LANES), F32),
        ],
        name="s5_prep",
    )(par, bt, ct, dsk)


def _cmul_step(s, s_sw, v, v_sw, re, ims):
    return re * s + ims * s_sw + v, re * s_sw - ims * s + v_sw


def _s5_kernel(n_ctx_chunks, n_lat0, ug_ref, m_ref, bp_ref, cp_ref, tab_ref, pow_ref, y_ref,
               vf_ref, vfs_ref, vb_ref, vbs_ref):
    gb, n_rows, _ = ug_ref.shape
    n_seg = (n_rows - n_lat0) // SUBLANES
    pitch = vf_ref.shape[1] // SUBLANES
    half = S5_STATE

    def put_segments(ref, g, val):
        for r in range(SUBLANES):
            ref[g, r * pitch:r * pitch + n_seg, :] = val[r * n_seg:(r + 1) * n_seg]

    tab = tab_ref[...]
    ctx_end = []
    sp_ctx = []
    for g in range(gb):
        v = _dot(ug_ref[g].astype(BF16), bp_ref[g])
        vl = v[n_lat0:]
        put_segments(vf_ref, g, vl[:, :LANES])
        put_segments(vb_ref, g, vl[:, LANES:])
        put_segments(vfs_ref, g, pltpu.roll(vl[:, :LANES], half, 1))
        put_segments(vbs_ref, g, pltpu.roll(vl[:, LANES:], half, 1))

        vc = v[:n_ctx_chunks]
        vcs = jnp.concatenate([pltpu.roll(vc[:, :LANES], half, 1), pltpu.roll(vc[:, LANES:], half, 1)], axis=1)
        zero = jnp.zeros((1, LANES), F32)
        sf, sfs, sb, sbs = zero, zero, zero, zero
        rows_f, rows_b = [], [None] * n_ctx_chunks
        for j in range(n_ctx_chunks):
            jb = n_ctx_chunks - 1 - j
            rows_f.append(sf)
            rows_b[jb] = sb
            sf, sfs = _cmul_step(sf, sfs, vc[j:j + 1, :LANES], vcs[j:j + 1, :LANES], tab[g, 0:1], tab[g, 1:2])
            sb, sbs = _cmul_step(sb, sbs, vc[jb:jb + 1, LANES:], vcs[jb:jb + 1, LANES:], tab[g, 4:5], tab[g, 5:6])
        ctx_end.append((sf, sfs, sb, sbs))
        sp_ctx.append(jnp.concatenate([jnp.concatenate(rows_f, axis=0), jnp.concatenate(rows_b, axis=0)], axis=1))

    coef = [[jnp.broadcast_to(tab[g, q:q + 1], (SUBLANES, LANES)) for q in (0, 1, 4, 5)] for g in range(gb)]

    def step(k, carry):
        kb = n_seg - 1 - k
        out = []
        for g in range(gb):
            sf, sfs, sb, sbs = carry[g]
            are_f, aim_f, are_b, aim_b = coef[g]
            at_f = pl.ds(k, SUBLANES, stride=pitch)
            at_b = pl.ds(kb, SUBLANES, stride=pitch)
            v_f, v_fs = vf_ref[g, at_f, :], vfs_ref[g, at_f, :]
            v_b, v_bs = vb_ref[g, at_b, :], vbs_ref[g, at_b, :]
            vf_ref[g, at_f, :] = sf
            vb_ref[g, at_b, :] = sb
            sf, sfs = _cmul_step(sf, sfs, v_f, v_fs, are_f, aim_f)
            sb, sbs = _cmul_step(sb, sbs, v_b, v_bs, are_b, aim_b)
            out.append((sf, sfs, sb, sbs))
        return tuple(out)

    zero8 = jnp.zeros((SUBLANES, LANES), F32)
    ends = lax.fori_loop(0, n_seg, step, tuple((zero8, zero8, zero8, zero8) for _ in range(gb)))

    for g in range(gb):
        ef, efs, eb, ebs = ends[g]
        sf, sfs, sb, sbs = ctx_end[g]
        ent_f, ent_b = [], [None] * SUBLANES
        for r in range(SUBLANES):
            rb = SUBLANES - 1 - r
            ent_f.append((sf, sfs))
            ent_b[rb] = (sb, sbs)
            sf, sfs = _cmul_step(sf, sfs, ef[r:r + 1], efs[r:r + 1], tab[g, 2:3], tab[g, 3:4])
            sb, sbs = _cmul_step(sb, sbs, eb[rb:rb + 1], ebs[rb:rb + 1], tab[g, 6:7], tab[g, 7:8])

        segs = []
        for r in range(SUBLANES):
            seg_rows = slice(r * pitch, r * pitch + n_seg)
            f = vf_ref[g, seg_rows, :] + pow_ref[g, 0] * ent_f[r][0] + pow_ref[g, 1] * ent_f[r][1]
            b = vb_ref[g, seg_rows, :] + pow_ref[g, 2] * ent_b[r][0] + pow_ref[g, 3] * ent_b[r][1]
            segs.append(jnp.concatenate([f, b], axis=1))
        pad = jnp.zeros((n_lat0 - n_ctx_chunks, 2 * LANES), F32)
        sprev = jnp.concatenate([sp_ctx[g], pad] + segs, axis=0).astype(BF16)
        y_ref[g] = _dot(ug_ref[g].astype(BF16), m_ref[g]) + _dot(sprev, cp_ref[g])


def _s5_apply(ug, m, bp, cp, tab, pw, n_ctx_chunks, n_lat0):
    n_groups, n_rows, cw = ug.shape
    n_seg = pw.shape[-2]
    assert n_rows - n_lat0 == SUBLANES * n_seg and n_lat0 % SUBLANES == 0
    blk = lambda *shape: pl.BlockSpec((S5_GB,) + shape, lambda i: (i,) + (0,) * len(shape))
    seg = pltpu.VMEM((S5_GB, SUBLANES * (n_seg + SUBLANES), LANES), F32)
    return pl.pallas_call(
        functools.partial(_s5_kernel, n_ctx_chunks, n_lat0),
        grid=(n_groups // S5_GB,),
        in_specs=[blk(n_rows, cw), blk(cw, cw), blk(cw, 2 * LANES), blk(2 * LANES, cw), blk(8, LANES),
                  blk(4, n_seg, LANES)],
        out_specs=blk(n_rows, cw),
        out_shape=jax.ShapeDtypeStruct((n_groups, n_rows, cw), F32),
        scratch_shapes=[seg, seg, seg, seg],
        compiler_params=pltpu.CompilerParams(vmem_limit_bytes=VMEM_LIMIT),
        name="s5_scan",
    )(ug, m, bp, cp, tab, pw)


def _mixer_kernel(n_ctx_tiles, ctx_row_len, z_ref, y_ref, mod_ref, gmix_ref, gffn_ref, win_ref, wglu_ref,
                  lng_ref, lnb_ref, wsg_ref, bsg_ref, wconv_ref, wbr_ref, wmg_ref, bmg_ref, wo_ref, wr_ref,
                  br_ref, z1_ref, h2_ref, comb_ref):
    d = D_MODEL
    z = z_ref[...]
    mod = mod_ref[...]
    sh1, sc1, ga1 = mod[:, 0:d], mod[:, d:2 * d], mod[:, 2 * d:3 * d]
    sh2, sc2 = mod[:, 3 * d:4 * d], mod[:, 4 * d:5 * d]
    h = _rms(z, gmix_ref[...]) * (1.0 + sc1) + sh1
    hb = h.astype(BF16)
    proj = _dot(hb, win_ref[...])
    su, sv, cgb, cgc, chh = [proj[:, k * MIX_W:(k + 1) * MIX_W] for k in range(5)]

    yg = jax.nn.gelu(y_ref[...])
    y_a = yg * jax.nn.sigmoid(_dot(yg.astype(BF16), wglu_ref[...]))

    gv = jax.nn.gelu(sv)
    mu = jnp.mean(gv, axis=-1, keepdims=True)
    var = jnp.mean(jnp.square(gv - mu), axis=-1, keepdims=True)
    vn = ((gv - mu) * lax.rsqrt(var + NORM_EPS) * lng_ref[...] + lnb_ref[...]).astype(BF16)
    bsg = bsg_ref[...]
    chunks = []
    for n in range(TM // SG_CHUNK):
        rows = slice(n * SG_CHUNK, (n + 1) * SG_CHUNK)
        cols = [_dot(wsg_ref[g], vn[rows, g * SG_GROUP_W:(g + 1) * SG_GROUP_W]) for g in range(SG_GROUPS)]
        chunks.append(jnp.concatenate(cols, axis=1) + bsg)
    y_b = jax.nn.gelu(su) * jnp.concatenate(chunks, axis=0)

    zc = cgc * chh
    row_len = jnp.where(pl.program_id(0) < n_ctx_tiles, ctx_row_len, GRID_W)
    pos = lax.broadcasted_iota(jnp.int32, (TM, 1), 0) % row_len
    prev = jnp.where(pos == 0, 0.0, pltpu.roll(zc, 1, 0))
    nxt = jnp.where(pos == row_len - 1, 0.0, pltpu.roll(zc, TM - 1, 0))
    wc = wconv_ref[...]
    y_c = cgb * (wc[0:1] * prev + wc[1:2] * zc + wc[2:3] * nxt)

    merged = jnp.zeros((TM, d), F32)
    for k, y_k in enumerate((y_a, y_b, y_c)):
        gate = jax.nn.sigmoid(_dot(hb, wmg_ref[:, k * d:(k + 1) * d]) + bmg_ref[:, k * d:(k + 1) * d])
        merged = merged + gate * _dot(y_k.astype(BF16), wbr_ref[k])
    z1 = z + ga1 * _dot(merged.astype(BF16), wo_ref[...])
    z1_ref[...] = z1

    h2 = _rms(z1, gffn_ref[...]) * (1.0 + sc2) + sh2
    h2_ref[...] = h2.astype(BF16)
    logits = _dot_hi(h2, wr_ref[...]) + br_ref[...]
    lane = lax.broadcasted_iota(jnp.int32, (TM, LANES), 1)
    big = jnp.int32(LANES)
    neg = jnp.float32(-jnp.inf)
    is_grp = (lane >= N_EXPERTS) & (lane < N_EXPERTS + N_EXP_GROUPS)
    gl = jnp.where(is_grp, logits, neg)
    ge = jnp.exp(gl - jnp.max(gl, axis=-1, keepdims=True))
    gp = ge / jnp.sum(ge, axis=-1, keepdims=True)
    g_p = jnp.max(gp, axis=-1, keepdims=True)
    g_idx = jnp.min(jnp.where(is_grp & (gp == g_p), lane, big), axis=-1, keepdims=True) - N_EXPERTS
    in_grp = (lane >= g_idx * EXPERTS_PER_GROUP) & (lane < (g_idx + 1) * EXPERTS_PER_GROUP)
    el = jnp.where(in_grp, logits, neg)
    ee = jnp.exp(el - jnp.max(el, axis=-1, keepdims=True))
    ep = ee / jnp.sum(ee, axis=-1, keepdims=True)
    p1 = jnp.max(ep, axis=-1, keepdims=True)
    i1 = jnp.min(jnp.where(in_grp & (ep == p1), lane, big), axis=-1, keepdims=True)
    rest = in_grp & (lane != i1)
    ep2 = jnp.where(rest, ep, -1.0)
    p2 = jnp.max(ep2, axis=-1, keepdims=True)
    i2 = jnp.min(jnp.where(rest & (ep2 == p2), lane, big), axis=-1, keepdims=True)
    tot = p1 + p2
    comb_ref[...] = jnp.where(lane == i1, g_p * (p1 / tot), 0.0) + jnp.where(lane == i2, g_p * (p2 / tot), 0.0)


def _mixer(z, ytm, mod, p, n_ctx_tiles, ctx_row_len):
    n = z.shape[0]
    d = D_MODEL
    tile = lambda w: pl.BlockSpec((TM, w), lambda i: (i, 0))
    return pl.pallas_call(
        functools.partial(_mixer_kernel, n_ctx_tiles, ctx_row_len),
        grid=(n // TM,),
        in_specs=[
            tile(d), tile(MIX_W), _mod_spec(n_ctx_tiles),
            _const_spec((1, d)), _const_spec((1, d)),
            _const_spec((d, 5 * MIX_W)), _const_spec((MIX_W, MIX_W)),
            _const_spec((1, MIX_W)), _const_spec((1, MIX_W)),
            _const_spec((SG_GROUPS, SG_CHUNK, SG_CHUNK)), _const_spec((SG_CHUNK, MIX_W)),
            _const_spec((CONV_K, MIX_W)), _const_spec((N_BRANCH, MIX_W, d)),
            _const_spec((d, N_BRANCH * d)), _const_spec((1, N_BRANCH * d)),
            _const_spec((d, d)), _const_spec((d, LANES)), _const_spec((1, LANES)),
        ],
        out_specs=[tile(d), tile(d), tile(LANES)],
        out_shape=[
            jax.ShapeDtypeStruct((n, d), F32),
            jax.ShapeDtypeStruct((n, d), BF16),
            jax.ShapeDtypeStruct((n, LANES), F32),
        ],
        compiler_params=pltpu.CompilerParams(vmem_limit_bytes=VMEM_LIMIT),
        name="mixer",
    )(z, ytm, mod, p["g_mix"], p["g_ffn"], p["w_in5"], p["w_glu"], p["sg_ln_g"], p["sg_ln_b"],
      p["w_sg"], p["b_sg_full"], p["w_conv"], p["w_branch"], p["w_merge"], p["b_merge"], p["w_o"],
      p["w_router"], p["b_router"])


def _moe_kernel(h2_ref, comb_ref, w1_ref, w3_ref, w2_ref, out_ref):
    grp = pl.program_id(1)

    @pl.when(grp == 0)
    def _():
        out_ref[...] = jnp.zeros_like(out_ref)

    f = D_FF_EXPERT

    def sub_tile(t, carry):
        r = pl.multiple_of(t * MOE_SUB, MOE_SUB)
        hb = h2_ref[pl.ds(r, MOE_SUB), :]
        act = jax.nn.silu(_dot(hb, w1_ref[...])) * _dot(hb, w3_ref[...])
        cm = comb_ref[pl.ds(r, MOE_SUB), :]
        lane = lax.broadcasted_iota(jnp.int32, cm.shape, 1)
        parts = []
        for e in range(EXPERTS_PER_GROUP):
            c_e = jnp.sum(jnp.where(lane == grp * EXPERTS_PER_GROUP + e, cm, 0.0), axis=-1, keepdims=True)
            parts.append((act[:, e * f:(e + 1) * f] * c_e).astype(BF16))
        out_ref[pl.ds(r, MOE_SUB), :] += _dot(jnp.concatenate(parts, axis=1), w2_ref[...])
        return carry

    lax.fori_loop(0, MOE_TM // MOE_SUB, sub_tile, 0)


def _moe_dense(h2b, comb, w1g, w3g, w2g):
    n = h2b.shape[0]
    d = D_MODEL
    gf = EXPERTS_PER_GROUP * D_FF_EXPERT
    return pl.pallas_call(
        _moe_kernel,
        grid=(n // MOE_TM, N_EXP_GROUPS),
        in_specs=[
            pl.BlockSpec((MOE_TM, d), lambda i, g: (i, 0)),
            pl.BlockSpec((MOE_TM, LANES), lambda i, g: (i, 0)),
            pl.BlockSpec((None, d, gf), lambda i, g: (g, 0, 0)),
            pl.BlockSpec((None, d, gf), lambda i, g: (g, 0, 0)),
            pl.BlockSpec((None, gf, d), lambda i, g: (g, 0, 0)),
        ],
        out_specs=pl.BlockSpec((MOE_TM, d), lambda i, g: (i, 0)),
        out_shape=jax.ShapeDtypeStruct((n, d), F32),
        compiler_params=pltpu.CompilerParams(
            dimension_semantics=("parallel", "arbitrary"), vmem_limit_bytes=VMEM_LIMIT),
        name="moe_dense",
    )(h2b, comb, w1g, w3g, w2g)


def _residual_kernel(final, z1_ref, moe_ref, mod_ref, g_ref, o_ref):
    ga2 = mod_ref[...][:, 5 * D_MODEL:6 * D_MODEL]
    z2 = z1_ref[...] + ga2 * moe_ref[...]
    o_ref[...] = _rms(z2, g_ref[...]) if final else z2


def _residual(z1, moe, mod, g_final, final, n_ctx_tiles):
    n = z1.shape[0]
    tile = pl.BlockSpec((TM, D_MODEL), lambda i: (i, 0))
    return pl.pallas_call(
        functools.partial(_residual_kernel, final),
        grid=(n // TM,),
        in_specs=[tile, tile, _mod_spec(n_ctx_tiles), _const_spec((1, D_MODEL))],
        out_specs=tile,
        out_shape=jax.ShapeDtypeStruct((n, D_MODEL), F32),
        name="residual",
    )(z1, moe, mod, g_final)


def kernel(x, c, ctx, c_ctx, w_ada, b_ada, g_mix, g_ffn, w_in, lam_re, lam_im, log_dt, b_re, b_im, c_re, c_im, d_skip, w_glu, sg_ln_g, sg_ln_b, w_sg, b_sg, w_conv, w_branch, w_merge, b_merge, w_o, w_grp, b_grp, w_exp, b_exp, w1, w3, w2, g_final):
    depth = w_ada.shape[0]
    bsz, seq, d = x.shape
    n_ctx = ctx.shape[1]
    ctx_rows = -(-n_ctx // CTX_ALIGN) * CTX_ALIGN
    n = ctx_rows + seq
    assert bsz == 1 and d == D_MODEL
    assert ctx_rows % TM == 0 and seq % TM == 0 and n % MOE_TM == 0 and (TM % n_ctx == 0 or n_ctx % TM == 0)
    assert n_ctx % S5_T == 0 and seq % (S5_T * SUBLANES * SUBLANES) == 0
    n_ctx_tiles = ctx_rows // TM
    n_seg_chunks = seq // (S5_T * SUBLANES)

    z = jnp.concatenate([ctx[0], jnp.zeros((ctx_rows - n_ctx, d), F32), x[0]], axis=0)
    cc = jnp.zeros((8, d), F32).at[0].set(c_ctx).at[1].set(c[0])
    mods = _ada(cc, w_ada, b_ada)[:, :2].reshape(depth, 2, 1, 6 * d)
    s5_m, s5_bp, s5_cp, s5_tab, s5_pow = _s5_prep(lam_re, lam_im, log_dt, b_re, b_im, c_re, c_im, d_skip, n_seg_chunks)

    for l in range(depth):
        gf = EXPERTS_PER_GROUP * D_FF_EXPERT
        pad = LANES - N_EXPERTS - N_EXP_GROUPS
        p = dict(
            g_mix=g_mix[l].reshape(1, d), g_ffn=g_ffn[l].reshape(1, d),
            w_in5=w_in[l][:, MIX_W:].astype(BF16), w_glu=w_glu[l].astype(BF16),
            sg_ln_g=sg_ln_g[l].reshape(1, MIX_W), sg_ln_b=sg_ln_b[l].reshape(1, MIX_W),
            w_sg=w_sg[l].astype(BF16),
            b_sg_full=jnp.repeat(b_sg[l].T, SG_GROUP_W, axis=1),
            w_conv=w_conv[l], w_branch=w_branch[l].astype(BF16), w_merge=w_merge[l].astype(BF16),
            b_merge=b_merge[l].reshape(1, N_BRANCH * d), w_o=w_o[l].astype(BF16),
            w_router=jnp.pad(jnp.concatenate([w_exp[l], w_grp[l]], axis=1), ((0, 0), (0, pad))),
            b_router=jnp.pad(jnp.concatenate([b_exp[l], b_grp[l]]), (0, pad)).reshape(1, LANES),
        )
        mod = mods[l]

        u = _u_proj(z, mod, p["g_mix"], w_in[l][:, :MIX_W].astype(BF16), n_ctx_tiles)
        n_rows = n // S5_T
        ug = u.reshape(n_rows, S5_T, S5_GROUPS, S5_GROUP).transpose(2, 0, 1, 3).reshape(S5_GROUPS, n_rows, S5_CW)
        yg = _s5_apply(ug, s5_m[l], s5_bp[l], s5_cp[l], s5_tab[l], s5_pow[l], n_ctx // S5_T, ctx_rows // S5_T)
        ytm = yg.reshape(S5_GROUPS, n_rows, S5_T, S5_GROUP).transpose(1, 2, 0, 3).reshape(n, MIX_W)

        z1, h2b, comb = _mixer(z, ytm, mod, p, n_ctx_tiles, min(n_ctx, TM))

        to_cols = lambda w: w.reshape(N_EXP_GROUPS, EXPERTS_PER_GROUP, d, D_FF_EXPERT).transpose(0, 2, 1, 3).reshape(N_EXP_GROUPS, d, gf)
        moe = _moe_dense(h2b, comb, to_cols(w1[l]).astype(BF16), to_cols(w3[l]).astype(BF16),
                         w2[l].reshape(N_EXP_GROUPS, gf, d).astype(BF16))
        last = l == depth - 1
        z = _residual(z1, moe, mod, g_final.reshape(1, d) if last else p["g_ffn"], last, n_ctx_tiles)

    return z[ctx_rows:].reshape(bsz, seq, d)
```

```python
import functools

import jax
import jax.numpy as jnp
from jax import lax
from jax.experimental import pallas as pl
from jax.experimental.pallas import tpu as pltpu

F32 = jnp.float32
BF16 = jnp.bfloat16
HIGHEST = lax.Precision.HIGHEST

D_MODEL = 1024
GRID_W = 64
MIX_W = D_MODEL // 2
N_BRANCH = 3
S5_GROUP = 16
S5_GROUPS = MIX_W // S5_GROUP
S5_STATE = 64
SG_CHUNK = 128
SG_GROUPS = 4
SG_GROUP_W = MIX_W // SG_GROUPS
CONV_K = 3
N_EXP_GROUPS = 4
EXPERTS_PER_GROUP = 8
N_EXPERTS = N_EXP_GROUPS * EXPERTS_PER_GROUP
D_FF_EXPERT = D_MODEL // 4
NORM_EPS = 1e-6

LANES = 128
SUBLANES = 8
S5_T = 16
S5_CW = S5_T * S5_GROUP
S5_GB = 4
CTX_ALIGN = 512
TM = 512
MOE_TM = 1536
MOE_SUB = 256
VMEM_LIMIT = 56 * 1024 * 1024


def _dot(a, b):
    return jnp.dot(a, b, preferred_element_type=F32)


def _dot_hi(a, b):
    return jnp.dot(a, b, preferred_element_type=F32, precision=HIGHEST)


def _sigmoid(x):
    return 0.5 * jnp.tanh(0.5 * x) + 0.5


def _split_bf16(w):
    hi = w.astype(BF16)
    return jnp.stack([hi, (w - hi.astype(F32)).astype(BF16)])


def _rms(x, g):
    return x * lax.rsqrt(jnp.mean(x * x, axis=-1, keepdims=True) + NORM_EPS) * g


def _const_spec(shape):
    nd = len(shape)
    return pl.BlockSpec(shape, lambda *_: (0,) * nd, pipeline_mode=pl.Buffered(1))


def _ada_kernel(cc_ref, w_ref, b_ref, o_ref):
    cc = cc_ref[...]
    o_ref[...] = _dot_hi(cc * _sigmoid(cc), w_ref[...]) + b_ref[...]


def _ada(cc, w_ada, b_ada):
    depth = w_ada.shape[0]
    nblk = 6
    return pl.pallas_call(
        _ada_kernel,
        grid=(depth, nblk),
        in_specs=[
            pl.BlockSpec((8, D_MODEL), lambda l, j: (0, 0)),
            pl.BlockSpec((None, D_MODEL, D_MODEL), lambda l, j: (l, 0, j)),
            pl.BlockSpec((None, 1, D_MODEL), lambda l, j: (l, 0, j)),
        ],
        out_specs=pl.BlockSpec((None, 8, D_MODEL), lambda l, j: (l, 0, j)),
        out_shape=jax.ShapeDtypeStruct((depth, 8, 6 * D_MODEL), F32),
        name="ada",
    )(cc, w_ada, b_ada.reshape(depth, 1, 6 * D_MODEL))


def _mod_spec(n_ctx_tiles):
    return pl.BlockSpec((None, 1, 6 * D_MODEL), lambda i: (jnp.where(i < n_ctx_tiles, 0, 1), 0, 0))


def _u_kernel(z_ref, mod_ref, g_ref, w_ref, u_ref):
    mod = mod_ref[...]
    h = _rms(z_ref[...], g_ref[...]) * (1.0 + mod[:, D_MODEL:2 * D_MODEL]) + mod[:, 0:D_MODEL]
    u_ref[...] = _dot(h.astype(BF16), w_ref[...])


def _u_proj(z, mod, g_mix, w_u, n_ctx_tiles):
    n = z.shape[0]
    return pl.pallas_call(
        _u_kernel,
        grid=(n // TM,),
        in_specs=[
            pl.BlockSpec((TM, D_MODEL), lambda i: (i, 0)),
            _mod_spec(n_ctx_tiles),
            _const_spec((1, D_MODEL)),
            _const_spec((D_MODEL, MIX_W)),
        ],
        out_specs=pl.BlockSpec((TM, MIX_W), lambda i: (i, 0)),
        out_shape=jax.ShapeDtypeStruct((n, MIX_W), F32),
        name="u_proj",
    )(z, mod, g_mix, w_u)


def _s5_prep_kernel(n_seg_chunks, par_ref, bt_ref, ct_ref, dsk_ref, m_ref, bp_ref, cp_ref, tab_ref, pow_ref):
    t = S5_T
    par = par_ref[...]
    lo = lax.broadcasted_iota(jnp.int32, (1, LANES), 1) < S5_STATE
    sgn = jnp.where(lo, -1.0, 1.0)

    def rows(n):
        return lax.broadcasted_iota(jnp.int32, (n, 1), 0).astype(F32)

    def expand(pw_rows, vec_rows):
        pr, pi = pw_rows
        vr, vi = vec_rows
        pr, pi = pr[:, None, :], pi[:, None, :]
        vr, vi = vr[None, :, :], vi[None, :, :]
        re = (pr * vr - pi * vi).reshape(-1, LANES)
        im = (pr * vi + pi * vr).reshape(-1, LANES)
        return re, im

    mt = None
    bp_cols, cp_rows, tab_rows, pow_rows = [], [], [], []
    for d in range(2):
        lre, lim = par[2 * d:2 * d + 1], par[2 * d + 1:2 * d + 2]
        dt = jnp.exp(par[4 + d:5 + d])
        xr, xi = lre * dt, lim * dt

        def cpow(m):
            mag = jnp.exp(m * xr)
            return mag * jnp.cos(m * xi), mag * jnp.sin(m * xi)

        ar, ai = cpow(1.0)
        den = lre * lre + lim * lim
        qr = ((ar - 1.0) * lre + ai * lim) / den
        qi = (ai * lre - (ar - 1.0) * lim) / den
        br, bi = bt_ref[2 * d], bt_ref[2 * d + 1]
        bb = (qr * br - qi * bi, qr * bi + qi * br)
        cc = (ct_ref[2 * d], ct_ref[2 * d + 1])

        inj_pow = cpow(t - 1.0 - rows(t)) if d == 0 else cpow(rows(t))
        re, im = expand(inj_pow, bb)
        bp_cols.append(jnp.where(lo, re, im))

        out_pow = cpow(rows(t) + 1.0) if d == 0 else cpow(t - rows(t))
        re, im = expand(out_pow, cc)
        cp_rows.append(jnp.where(lo, re, -im).T)

        lag_pow = cpow(rows(t)) if d == 0 else cpow(t - 1.0 - rows(t))
        re, im = expand(lag_pow, cc)
        ctab = jnp.where(lo, re, -im)
        bcat = jnp.broadcast_to(jnp.where(lo, bb[0], bb[1])[None], (t, S5_GROUP, LANES)).reshape(-1, LANES)
        k256 = lax.dot_general(ctab, bcat, (((1,), (1,)), ((), ())), precision=HIGHEST,
                               preferred_element_type=F32)
        col_blk = lax.broadcasted_iota(jnp.int32, (S5_CW, S5_CW), 1) // S5_GROUP
        acc = jnp.zeros((S5_CW, S5_CW), F32)
        for r in range(t):
            sh = r * S5_GROUP if d == 0 else (t - 1 - r) * S5_GROUP
            if sh == 0:
                shifted = k256
            elif d == 0:
                shifted = jnp.concatenate([jnp.zeros((sh, S5_CW), F32), k256[:S5_CW - sh]], axis=0)
            else:
                shifted = jnp.concatenate([k256[sh:], jnp.zeros((sh, S5_CW), F32)], axis=0)
            acc = acc + jnp.where(col_blk == r, shifted, 0.0)
        mt = acc if mt is None else mt + acc

        for m in (float(t), float(t * n_seg_chunks)):
            re, im = cpow(m)
            tab_rows += [re, sgn * im]
        kk = rows(n_seg_chunks)
        re, im = cpow(t * kk) if d == 0 else cpow(t * (n_seg_chunks - 1.0 - kk))
        pow_rows += [re, sgn * im]

    eye = (lax.broadcasted_iota(jnp.int32, (S5_CW, S5_CW), 0) == lax.broadcasted_iota(jnp.int32, (S5_CW, S5_CW), 1))
    mt = mt + jnp.where(eye, dsk_ref[...], 0.0)
    m_ref[...] = mt.T.astype(BF16)
    bp_ref[...] = jnp.concatenate(bp_cols, axis=1).astype(BF16)
    cp_ref[...] = jnp.concatenate(cp_rows, axis=0).astype(BF16)
    tab_ref[...] = jnp.concatenate(tab_rows, axis=0)
    for q in range(4):
        pow_ref[q] = pow_rows[q]


def _s5_prep(lam_re, lam_im, log_dt, b_re, b_im, c_re, c_im, d_skip, n_seg_chunks):
    depth = lam_re.shape[0]
    g, p_, h = S5_GROUPS, S5_STATE, S5_GROUP
    dup = lambda a: jnp.concatenate([a, a], axis=-1)
    dirs_last = lambda a: jnp.moveaxis(a, 1, 2)
    lam = jnp.stack([lam_re, lam_im], axis=3)
    lam = dirs_last(lam).reshape(depth, g, 4, p_)
    ldt = jnp.broadcast_to(dirs_last(log_dt[..., None]), (depth, g, 2, p_))
    par = dup(jnp.concatenate([lam, ldt, jnp.zeros((depth, g, 2, p_), F32)], axis=2))
    bt = jnp.stack([b_re, b_im], axis=3)
    bt = dup(jnp.swapaxes(dirs_last(bt), -1, -2).reshape(depth, g, 4, h, p_))
    ct = dup(dirs_last(jnp.stack([c_re, c_im], axis=3)).reshape(depth, g, 4, h, p_))
    dsk = jnp.tile(d_skip.reshape(depth, g, 1, h), (1, 1, 1, S5_T))

    blk = lambda *shape: pl.BlockSpec((None, None) + shape, lambda l, j: (l, j) + (0,) * len(shape))
    return pl.pallas_call(
        functools.partial(_s5_prep_kernel, n_seg_chunks),
        grid=(depth, g),
        in_specs=[blk(8, LANES), blk(4, h, LANES), blk(4, h, LANES), blk(1, S5_CW)],
        out_specs=[blk(S5_CW, S5_CW), blk(S5_CW, 2 * LANES), blk(2 * LANES, S5_CW), blk(8, LANES),
                   blk(4, n_seg_chunks, LANES)],
        out_shape=[
            jax.ShapeDtypeStruct((depth, g, S5_CW, S5_CW), BF16),
            jax.ShapeDtypeStruct((depth, g, S5_CW, 2 * LANES), BF16),
            jax.ShapeDtypeStruct((depth, g, 2 * LANES, S5_CW), BF16),
            jax.ShapeDtypeStruct((depth, g, 8, LANES), F32),
            jax.ShapeDtypeStruct((depth, g, 4, n_seg_chunks, LANES), F32),
        ],
        name="s5_prep",
    )(par, bt, ct, dsk)


def _cmul_step(s, s_sw, v, v_sw, re, ims):
    return re * s + ims * s_sw + v, re * s_sw - ims * s + v_sw


def _s5_kernel(n_ctx_chunks, n_lat0, ug_ref, m_ref, bp_ref, cp_ref, tab_ref, pow_ref, y_ref,
               vf_ref, vfs_ref, vb_ref, vbs_ref):
    gb, n_rows, _ = ug_ref.shape
    n_seg = (n_rows - n_lat0) // SUBLANES
    pitch = vf_ref.shape[1] // SUBLANES
    half = S5_STATE

    def put_segments(ref, g, val):
        for r in range(SUBLANES):
            ref[g, r * pitch:r * pitch + n_seg, :] = val[r * n_seg:(r + 1) * n_seg]

    tab = tab_ref[...]
    ctx_end = []
    sp_ctx = []
    for g in range(gb):
        v = _dot(ug_ref[g].astype(BF16), bp_ref[g])
        vl = v[n_lat0:]
        put_segments(vf_ref, g, vl[:, :LANES])
        put_segments(vb_ref, g, vl[:, LANES:])
        put_segments(vfs_ref, g, pltpu.roll(vl[:, :LANES], half, 1))
        put_segments(vbs_ref, g, pltpu.roll(vl[:, LANES:], half, 1))

        vc = v[:n_ctx_chunks]
        vcs = jnp.concatenate([pltpu.roll(vc[:, :LANES], half, 1), pltpu.roll(vc[:, LANES:], half, 1)], axis=1)
        zero = jnp.zeros((1, LANES), F32)
        sf, sfs, sb, sbs = zero, zero, zero, zero
        rows_f, rows_b = [], [None] * n_ctx_chunks
        for j in range(n_ctx_chunks):
            jb = n_ctx_chunks - 1 - j
            rows_f.append(sf)
            rows_b[jb] = sb
            sf, sfs = _cmul_step(sf, sfs, vc[j:j + 1, :LANES], vcs[j:j + 1, :LANES], tab[g, 0:1], tab[g, 1:2])
            sb, sbs = _cmul_step(sb, sbs, vc[jb:jb + 1, LANES:], vcs[jb:jb + 1, LANES:], tab[g, 4:5], tab[g, 5:6])
        ctx_end.append((sf, sfs, sb, sbs))
        sp_ctx.append(jnp.concatenate([jnp.concatenate(rows_f, axis=0), jnp.concatenate(rows_b, axis=0)], axis=1))

    coef = [[jnp.broadcast_to(tab[g, q:q + 1], (SUBLANES, LANES)) for q in (0, 1, 4, 5)] for g in range(gb)]

    def step(k, carry):
        kb = n_seg - 1 - k
        out = []
        for g in range(gb):
            sf, sfs, sb, sbs = carry[g]
            are_f, aim_f, are_b, aim_b = coef[g]
            at_f = pl.ds(k, SUBLANES, stride=pitch)
            at_b = pl.ds(kb, SUBLANES, stride=pitch)
            v_f, v_fs = vf_ref[g, at_f, :], vfs_ref[g, at_f, :]
            v_b, v_bs = vb_ref[g, at_b, :], vbs_ref[g, at_b, :]
            vf_ref[g, at_f, :] = sf
            vb_ref[g, at_b, :] = sb
            sf, sfs = _cmul_step(sf, sfs, v_f, v_fs, are_f, aim_f)
            sb, sbs = _cmul_step(sb, sbs, v_b, v_bs, are_b, aim_b)
            out.append((sf, sfs, sb, sbs))
        return tuple(out)

    zero8 = jnp.zeros((SUBLANES, LANES), F32)
    ends = lax.fori_loop(0, n_seg, step, tuple((zero8, zero8, zero8, zero8) for _ in range(gb)))

    for g in range(gb):
        ef, efs, eb, ebs = ends[g]
        sf, sfs, sb, sbs = ctx_end[g]
        ent_f, ent_b = [], [None] * SUBLANES
        for r in range(SUBLANES):
            rb = SUBLANES - 1 - r
            ent_f.append((sf, sfs))
            ent_b[rb] = (sb, sbs)
            sf, sfs = _cmul_step(sf, sfs, ef[r:r + 1], efs[r:r + 1], tab[g, 2:3], tab[g, 3:4])
            sb, sbs = _cmul_step(sb, sbs, eb[rb:rb + 1], ebs[rb:rb + 1], tab[g, 6:7], tab[g, 7:8])

        segs = []
        for r in range(SUBLANES):
            seg_rows = slice(r * pitch, r * pitch + n_seg)
            f = vf_ref[g, seg_rows, :] + pow_ref[g, 0] * ent_f[r][0] + pow_ref[g, 1] * ent_f[r][1]
            b = vb_ref[g, seg_rows, :] + pow_ref[g, 2] * ent_b[r][0] + pow_ref[g, 3] * ent_b[r][1]
            segs.append(jnp.concatenate([f, b], axis=1))
        pad = jnp.zeros((n_lat0 - n_ctx_chunks, 2 * LANES), F32)
        sprev = jnp.concatenate([sp_ctx[g], pad] + segs, axis=0).astype(BF16)
        y_ref[g] = _dot(ug_ref[g].astype(BF16), m_ref[g]) + _dot(sprev, cp_ref[g])


def _s5_apply(ug, m, bp, cp, tab, pw, n_ctx_chunks, n_lat0):
    n_groups, n_rows, cw = ug.shape
    n_seg = pw.shape[-2]
    assert n_rows - n_lat0 == SUBLANES * n_seg and n_lat0 % SUBLANES == 0
    blk = lambda *shape: pl.BlockSpec((S5_GB,) + shape, lambda i: (i,) + (0,) * len(shape))
    seg = pltpu.VMEM((S5_GB, SUBLANES * (n_seg + SUBLANES), LANES), F32)
    return pl.pallas_call(
        functools.partial(_s5_kernel, n_ctx_chunks, n_lat0),
        grid=(n_groups // S5_GB,),
        in_specs=[blk(n_rows, cw), blk(cw, cw), blk(cw, 2 * LANES), blk(2 * LANES, cw), blk(8, LANES),
                  blk(4, n_seg, LANES)],
        out_specs=blk(n_rows, cw),
        out_shape=jax.ShapeDtypeStruct((n_groups, n_rows, cw), F32),
        scratch_shapes=[seg, seg, seg, seg],
        compiler_params=pltpu.CompilerParams(vmem_limit_bytes=VMEM_LIMIT),
        name="s5_scan",
    )(ug, m, bp, cp, tab, pw)


def _mixer_kernel(n_ctx_tiles, ctx_row_len, z_ref, y_ref, mod_ref, gmix_ref, gffn_ref, win_ref, wglu_ref,
                  lng_ref, lnb_ref, wsg_ref, bsg_ref, wconv_ref, wbr_ref, wmg_ref, bmg_ref, wo_ref, wr_ref,
                  br_ref, z1_ref, h2_ref, comb_ref):
    d = D_MODEL
    z = z_ref[...]
    mod = mod_ref[...]
    sh1, sc1, ga1 = mod[:, 0:d], mod[:, d:2 * d], mod[:, 2 * d:3 * d]
    sh2, sc2 = mod[:, 3 * d:4 * d], mod[:, 4 * d:5 * d]
    h = _rms(z, gmix_ref[...]) * (1.0 + sc1) + sh1
    hb = h.astype(BF16)
    proj = _dot(hb, win_ref[...])
    su, sv, cgb, cgc, chh = [proj[:, k * MIX_W:(k + 1) * MIX_W] for k in range(5)]

    def merge_term(k, y_k):
        gate = _sigmoid(_dot(hb, wmg_ref[:, k * d:(k + 1) * d]) + bmg_ref[:, k * d:(k + 1) * d])
        return gate * _dot(y_k.astype(BF16), wbr_ref[k])

    yg = jax.nn.gelu(y_ref[...])
    merged = merge_term(0, yg * _sigmoid(_dot(yg.astype(BF16), wglu_ref[...])))

    gv = jax.nn.gelu(sv)
    mu = jnp.mean(gv, axis=-1, keepdims=True)
    var = jnp.mean(jnp.square(gv - mu), axis=-1, keepdims=True)
    vn = ((gv - mu) * lax.rsqrt(var + NORM_EPS) * lng_ref[...] + lnb_ref[...]).astype(BF16)
    bsg = bsg_ref[...]
    chunks = []
    for n in range(TM // SG_CHUNK):
        rows = slice(n * SG_CHUNK, (n + 1) * SG_CHUNK)
        cols = [_dot(wsg_ref[g], vn[rows, g * SG_GROUP_W:(g + 1) * SG_GROUP_W]) for g in range(SG_GROUPS)]
        chunks.append(jnp.concatenate(cols, axis=1) + bsg)
    merged = merged + merge_term(1, jax.nn.gelu(su) * jnp.concatenate(chunks, axis=0))

    zc = cgc * chh
    row = lax.broadcasted_iota(jnp.int32, (TM, 1), 0)
    is_ctx = pl.program_id(0) < n_ctx_tiles
    pos = jnp.where(is_ctx, row % ctx_row_len, row % GRID_W)
    prev = jnp.where(pos == 0, 0.0, pltpu.roll(zc, 1, 0))
    nxt = jnp.where(pos == jnp.where(is_ctx, ctx_row_len - 1, GRID_W - 1), 0.0, pltpu.roll(zc, TM - 1, 0))
    wc = wconv_ref[...]
    merged = merged + merge_term(2, cgb * (wc[0:1] * prev + wc[1:2] * zc + wc[2:3] * nxt))

    z1 = z + ga1 * _dot(merged.astype(BF16), wo_ref[...])
    z1_ref[...] = z1

    h2 = _rms(z1, gffn_ref[...]) * (1.0 + sc2) + sh2
    h2_hi = h2.astype(BF16)
    h2_ref[...] = h2_hi
    h2_lo = (h2 - h2_hi.astype(F32)).astype(BF16)
    logits = _dot(h2_hi, wr_ref[0]) + _dot(h2_lo, wr_ref[0]) + _dot(h2_hi, wr_ref[1]) + br_ref[...]
    lane = lax.broadcasted_iota(jnp.int32, (TM, LANES), 1)
    big = jnp.int32(LANES)
    neg = jnp.float32(-jnp.inf)
    is_grp = (lane >= N_EXPERTS) & (lane < N_EXPERTS + N_EXP_GROUPS)
    gl = jnp.where(is_grp, logits, neg)
    ge = jnp.exp(gl - jnp.max(gl, axis=-1, keepdims=True))
    gp = ge / jnp.sum(ge, axis=-1, keepdims=True)
    g_p = jnp.max(gp, axis=-1, keepdims=True)
    g_idx = jnp.min(jnp.where(is_grp & (gp == g_p), lane, big), axis=-1, keepdims=True) - N_EXPERTS
    in_grp = (lane >= g_idx * EXPERTS_PER_GROUP) & (lane < (g_idx + 1) * EXPERTS_PER_GROUP)
    el = jnp.where(in_grp, logits, neg)
    ee = jnp.exp(el - jnp.max(el, axis=-1, keepdims=True))
    ep = ee / jnp.sum(ee, axis=-1, keepdims=True)
    p1 = jnp.max(ep, axis=-1, keepdims=True)
    i1 = jnp.min(jnp.where(in_grp & (ep == p1), lane, big), axis=-1, keepdims=True)
    rest = in_grp & (lane != i1)
    ep2 = jnp.where(rest, ep, -1.0)
    p2 = jnp.max(ep2, axis=-1, keepdims=True)
    i2 = jnp.min(jnp.where(rest & (ep2 == p2), lane, big), axis=-1, keepdims=True)
    tot = p1 + p2
    comb_ref[...] = jnp.where(lane == i1, g_p * (p1 / tot), 0.0) + jnp.where(lane == i2, g_p * (p2 / tot), 0.0)


def _mixer(z, ytm, mod, p, n_ctx_tiles, ctx_row_len):
    n = z.shape[0]
    d = D_MODEL
    tile = lambda w: pl.BlockSpec((TM, w), lambda i: (i, 0))
    return pl.pallas_call(
        functools.partial(_mixer_kernel, n_ctx_tiles, ctx_row_len),
        grid=(n // TM,),
        in_specs=[
            tile(d), tile(MIX_W), _mod_spec(n_ctx_tiles),
            _const_spec((1, d)), _const_spec((1, d)),
            _const_spec((d, 5 * MIX_W)), _const_spec((MIX_W, MIX_W)),
            _const_spec((1, MIX_W)), _const_spec((1, MIX_W)),
            _const_spec((SG_GROUPS, SG_CHUNK, SG_CHUNK)), _const_spec((SG_CHUNK, MIX_W)),
            _const_spec((CONV_K, MIX_W)), _const_spec((N_BRANCH, MIX_W, d)),
            _const_spec((d, N_BRANCH * d)), _const_spec((1, N_BRANCH * d)),
            _const_spec((d, d)), _const_spec((2, d, LANES)), _const_spec((1, LANES)),
        ],
        out_specs=[tile(d), tile(d), tile(LANES)],
        out_shape=[
            jax.ShapeDtypeStruct((n, d), F32),
            jax.ShapeDtypeStruct((n, d), BF16),
            jax.ShapeDtypeStruct((n, LANES), F32),
        ],
        compiler_params=pltpu.CompilerParams(vmem_limit_bytes=VMEM_LIMIT),
        name="mixer",
    )(z, ytm, mod, p["g_mix"], p["g_ffn"], p["w_in5"], p["w_glu"], p["sg_ln_g"], p["sg_ln_b"],
      p["w_sg"], p["b_sg_full"], p["w_conv"], p["w_branch"], p["w_merge"], p["b_merge"], p["w_o"],
      p["w_router"], p["b_router"])


def _moe_kernel(h2_ref, comb_ref, w1_ref, w3_ref, w2_ref, out_ref):
    grp = pl.program_id(1)

    @pl.when(grp == 0)
    def _():
        out_ref[...] = jnp.zeros_like(out_ref)

    f = D_FF_EXPERT

    def sub_tile(t, carry):
        r = pl.multiple_of(t * MOE_SUB, MOE_SUB)
        hb = h2_ref[pl.ds(r, MOE_SUB), :]
        a = _dot(hb, w1_ref[...])
        act = a * _sigmoid(a) * _dot(hb, w3_ref[...])
        cm = comb_ref[pl.ds(r, MOE_SUB), :]
        lane = lax.broadcasted_iota(jnp.int32, cm.shape, 1)
        parts = []
        for e in range(EXPERTS_PER_GROUP):
            c_e = jnp.sum(jnp.where(lane == grp * EXPERTS_PER_GROUP + e, cm, 0.0), axis=-1, keepdims=True)
            parts.append((act[:, e * f:(e + 1) * f] * c_e).astype(BF16))
        out_ref[pl.ds(r, MOE_SUB), :] += _dot(jnp.concatenate(parts, axis=1), w2_ref[...])
        return carry

    lax.fori_loop(0, MOE_TM // MOE_SUB, sub_tile, 0)


def _moe_dense(h2b, comb, w1g, w3g, w2g):
    n = h2b.shape[0]
    d = D_MODEL
    gf = EXPERTS_PER_GROUP * D_FF_EXPERT
    return pl.pallas_call(
        _moe_kernel,
        grid=(n // MOE_TM, N_EXP_GROUPS),
        in_specs=[
            pl.BlockSpec((MOE_TM, d), lambda i, g: (i, 0)),
            pl.BlockSpec((MOE_TM, LANES), lambda i, g: (i, 0)),
            pl.BlockSpec((None, d, gf), lambda i, g: (g, 0, 0)),
            pl.BlockSpec((None, d, gf), lambda i, g: (g, 0, 0)),
            pl.BlockSpec((None, gf, d), lambda i, g: (g, 0, 0)),
        ],
        out_specs=pl.BlockSpec((MOE_TM, d), lambda i, g: (i, 0)),
        out_shape=jax.ShapeDtypeStruct((n, d), F32),
        compiler_params=pltpu.CompilerParams(
            dimension_semantics=("parallel", "arbitrary"), vmem_limit_bytes=VMEM_LIMIT),
        name="moe_dense",
    )(h2b, comb, w1g, w3g, w2g)


def _residual_kernel(final, z1_ref, moe_ref, mod_ref, g_ref, o_ref):
    ga2 = mod_ref[...][:, 5 * D_MODEL:6 * D_MODEL]
    z2 = z1_ref[...] + ga2 * moe_ref[...]
    o_ref[...] = _rms(z2, g_ref[...]) if final else z2


def _residual(z1, moe, mod, g_final, final, n_ctx_tiles):
    n = z1.shape[0]
    tile = pl.BlockSpec((TM, D_MODEL), lambda i: (i, 0))
    return pl.pallas_call(
        functools.partial(_residual_kernel, final),
        grid=(n // TM,),
        in_specs=[tile, tile, _mod_spec(n_ctx_tiles), _const_spec((1, D_MODEL))],
        out_specs=tile,
        out_shape=jax.ShapeDtypeStruct((n, D_MODEL), F32),
        name="residual",
    )(z1, moe, mod, g_final)


def kernel(x, c, ctx, c_ctx, w_ada, b_ada, g_mix, g_ffn, w_in, lam_re, lam_im, log_dt, b_re, b_im, c_re, c_im, d_skip, w_glu, sg_ln_g, sg_ln_b, w_sg, b_sg, w_conv, w_branch, w_merge, b_merge, w_o, w_grp, b_grp, w_exp, b_exp, w1, w3, w2, g_final):
    depth = w_ada.shape[0]
    bsz, seq, d = x.shape
    n_ctx = ctx.shape[1]
    ctx_rows = -(-n_ctx // CTX_ALIGN) * CTX_ALIGN
    n = ctx_rows + seq
    assert bsz == 1 and d == D_MODEL
    assert ctx_rows % TM == 0 and seq % TM == 0 and n % MOE_TM == 0 and (TM % n_ctx == 0 or n_ctx % TM == 0)
    assert n_ctx % S5_T == 0 and seq % (S5_T * SUBLANES * SUBLANES) == 0
    n_ctx_tiles = ctx_rows // TM
    n_seg_chunks = seq // (S5_T * SUBLANES)

    z = jnp.concatenate([ctx[0], jnp.zeros((ctx_rows - n_ctx, d), F32), x[0]], axis=0)
    cc = jnp.zeros((8, d), F32).at[0].set(c_ctx).at[1].set(c[0])
    mods = _ada(cc, w_ada, b_ada)[:, :2].reshape(depth, 2, 1, 6 * d)
    s5_m, s5_bp, s5_cp, s5_tab, s5_pow = _s5_prep(lam_re, lam_im, log_dt, b_re, b_im, c_re, c_im, d_skip, n_seg_chunks)

    for l in range(depth):
        gf = EXPERTS_PER_GROUP * D_FF_EXPERT
        pad = LANES - N_EXPERTS - N_EXP_GROUPS
        p = dict(
            g_mix=g_mix[l].reshape(1, d), g_ffn=g_ffn[l].reshape(1, d),
            w_in5=w_in[l][:, MIX_W:].astype(BF16), w_glu=w_glu[l].astype(BF16),
            sg_ln_g=sg_ln_g[l].reshape(1, MIX_W), sg_ln_b=sg_ln_b[l].reshape(1, MIX_W),
            w_sg=w_sg[l].astype(BF16),
            b_sg_full=jnp.repeat(b_sg[l].T, SG_GROUP_W, axis=1),
            w_conv=w_conv[l], w_branch=w_branch[l].astype(BF16), w_merge=w_merge[l].astype(BF16),
            b_merge=b_merge[l].reshape(1, N_BRANCH * d), w_o=w_o[l].astype(BF16),
            w_router=_split_bf16(jnp.pad(jnp.concatenate([w_exp[l], w_grp[l]], axis=1), ((0, 0), (0, pad)))),
            b_router=jnp.pad(jnp.concatenate([b_exp[l], b_grp[l]]), (0, pad)).reshape(1, LANES),
        )
        mod = mods[l]

        u = _u_proj(z, mod, p["g_mix"], w_in[l][:, :MIX_W].astype(BF16), n_ctx_tiles)
        n_rows = n // S5_T
        ug = u.reshape(n_rows, S5_T, S5_GROUPS, S5_GROUP).transpose(2, 0, 1, 3).reshape(S5_GROUPS, n_rows, S5_CW)
        yg = _s5_apply(ug, s5_m[l], s5_bp[l], s5_cp[l], s5_tab[l], s5_pow[l], n_ctx // S5_T, ctx_rows // S5_T)
        ytm = yg.reshape(S5_GROUPS, n_rows, S5_T, S5_GROUP).transpose(1, 2, 0, 3).reshape(n, MIX_W)

        z1, h2b, comb = _mixer(z, ytm, mod, p, n_ctx_tiles, min(n_ctx, TM))

        to_cols = lambda w: w.reshape(N_EXP_GROUPS, EXPERTS_PER_GROUP, d, D_FF_EXPERT).transpose(0, 2, 1, 3).reshape(N_EXP_GROUPS, d, gf)
        moe = _moe_dense(h2b, comb, to_cols(w1[l]).astype(BF16), to_cols(w3[l]).astype(BF16),
                         w2[l].reshape(N_EXP_GROUPS, gf, d).astype(BF16))
        last = l == depth - 1
        z = _residual(z1, moe, mod, g_final.reshape(1, d) if last else p["g_ffn"], last, n_ctx_tiles)

    return z[ctx_rows:].reshape(bsz, seq, d)
```

```python
import functools

import jax
import jax.numpy as jnp
from jax import lax
from jax.experimental import pallas as pl
from jax.experimental.pallas import tpu as pltpu

F32 = jnp.float32
BF16 = jnp.bfloat16
HIGHEST = lax.Precision.HIGHEST

D_MODEL = 1024
GRID_W = 64
MIX_W = D_MODEL // 2
N_BRANCH = 3
S5_GROUP = 16
S5_GROUPS = MIX_W // S5_GROUP
S5_STATE = 64
SG_CHUNK = 128
SG_GROUPS = 4
SG_GROUP_W = MIX_W // SG_GROUPS
CONV_K = 3
N_EXP_GROUPS = 4
EXPERTS_PER_GROUP = 8
N_EXPERTS = N_EXP_GROUPS * EXPERTS_PER_GROUP
D_FF_EXPERT = D_MODEL // 4
NORM_EPS = 1e-6

LANES = 128
SUBLANES = 8
S5_T = 16
S5_CW = S5_T * S5_GROUP
S5_GB = 4
CTX_ALIGN = 512
TM = 512
MOE_BLK = 16
MOE_SLOTS = 2 * TM + N_EXPERTS * MOE_BLK
MOE_TE = 512
MOE_NB = MOE_TE // MOE_BLK
VMEM_LIMIT = 56 * 1024 * 1024


def _dot(a, b):
    return jnp.dot(a, b, preferred_element_type=F32)


def _dot_hi(a, b):
    return jnp.dot(a, b, preferred_element_type=F32, precision=HIGHEST)


def _sigmoid(x):
    return 0.5 * jnp.tanh(0.5 * x) + 0.5


def _split_bf16(w):
    hi = w.astype(BF16)
    return jnp.stack([hi, (w - hi.astype(F32)).astype(BF16)])


def _rms(x, g):
    return x * lax.rsqrt(jnp.mean(x * x, axis=-1, keepdims=True) + NORM_EPS) * g


def _const_spec(shape):
    nd = len(shape)
    return pl.BlockSpec(shape, lambda *_: (0,) * nd, pipeline_mode=pl.Buffered(1))


def _ada_kernel(cc_ref, w_ref, b_ref, o_ref):
    cc = cc_ref[...]
    o_ref[...] = _dot_hi(cc * _sigmoid(cc), w_ref[...]) + b_ref[...]


def _ada(cc, w_ada, b_ada):
    depth = w_ada.shape[0]
    nblk = 6
    return pl.pallas_call(
        _ada_kernel,
        grid=(depth, nblk),
        in_specs=[
            pl.BlockSpec((8, D_MODEL), lambda l, j: (0, 0)),
            pl.BlockSpec((None, D_MODEL, D_MODEL), lambda l, j: (l, 0, j)),
            pl.BlockSpec((None, 1, D_MODEL), lambda l, j: (l, 0, j)),
        ],
        out_specs=pl.BlockSpec((None, 8, D_MODEL), lambda l, j: (l, 0, j)),
        out_shape=jax.ShapeDtypeStruct((depth, 8, 6 * D_MODEL), F32),
        name="ada",
    )(cc, w_ada, b_ada.reshape(depth, 1, 6 * D_MODEL))


def _mod_spec(n_ctx_tiles):
    return pl.BlockSpec((None, 1, 6 * D_MODEL), lambda i: (jnp.where(i < n_ctx_tiles, 0, 1), 0, 0))


def _u_kernel(z_ref, mod_ref, g_ref, w_ref, u_ref):
    mod = mod_ref[...]
    h = _rms(z_ref[...], g_ref[...]) * (1.0 + mod[:, D_MODEL:2 * D_MODEL]) + mod[:, 0:D_MODEL]
    u_ref[...] = _dot(h.astype(BF16), w_ref[...])


def _u_proj(z, mod, g_mix, w_u, n_ctx_tiles):
    n = z.shape[0]
    return pl.pallas_call(
        _u_kernel,
        grid=(n // TM,),
        in_specs=[
            pl.BlockSpec((TM, D_MODEL), lambda i: (i, 0)),
            _mod_spec(n_ctx_tiles),
            _const_spec((1, D_MODEL)),
            _const_spec((D_MODEL, MIX_W)),
        ],
        out_specs=pl.BlockSpec((TM, MIX_W), lambda i: (i, 0)),
        out_shape=jax.ShapeDtypeStruct((n, MIX_W), F32),
        name="u_proj",
    )(z, mod, g_mix, w_u)


def _s5_prep_kernel(n_seg_chunks, par_ref, bt_ref, ct_ref, dsk_ref, m_ref, bp_ref, cp_ref, tab_ref, pow_ref):
    t = S5_T
    par = par_ref[...]
    lo = lax.broadcasted_iota(jnp.int32, (1, LANES), 1) < S5_STATE
    sgn = jnp.where(lo, -1.0, 1.0)

    def rows(n):
        return lax.broadcasted_iota(jnp.int32, (n, 1), 0).astype(F32)

    def expand(pw_rows, vec_rows):
        pr, pi = pw_rows
        vr, vi = vec_rows
        pr, pi = pr[:, None, :], pi[:, None, :]
        vr, vi = vr[None, :, :], vi[None, :, :]
        re = (pr * vr - pi * vi).reshape(-1, LANES)
        im = (pr * vi + pi * vr).reshape(-1, LANES)
        return re, im

    mt = None
    bp_cols, cp_rows, tab_rows, pow_rows = [], [], [], []
    for d in range(2):
        lre, lim = par[2 * d:2 * d + 1], par[2 * d + 1:2 * d + 2]
        dt = jnp.exp(par[4 + d:5 + d])
        xr, xi = lre * dt, lim * dt

        def cpow(m):
            mag = jnp.exp(m * xr)
            return mag * jnp.cos(m * xi), mag * jnp.sin(m * xi)

        ar, ai = cpow(1.0)
        den = lre * lre + lim * lim
        qr = ((ar - 1.0) * lre + ai * lim) / den
        qi = (ai * lre - (ar - 1.0) * lim) / den
        br, bi = bt_ref[2 * d], bt_ref[2 * d + 1]
        bb = (qr * br - qi * bi, qr * bi + qi * br)
        cc = (ct_ref[2 * d], ct_ref[2 * d + 1])

        inj_pow = cpow(t - 1.0 - rows(t)) if d == 0 else cpow(rows(t))
        re, im = expand(inj_pow, bb)
        bp_cols.append(jnp.where(lo, re, im))

        out_pow = cpow(rows(t) + 1.0) if d == 0 else cpow(t - rows(t))
        re, im = expand(out_pow, cc)
        cp_rows.append(jnp.where(lo, re, -im).T)

        lag_pow = cpow(rows(t)) if d == 0 else cpow(t - 1.0 - rows(t))
        re, im = expand(lag_pow, cc)
        ctab = jnp.where(lo, re, -im)
        bcat = jnp.broadcast_to(jnp.where(lo, bb[0], bb[1])[None], (t, S5_GROUP, LANES)).reshape(-1, LANES)
        k256 = lax.dot_general(ctab, bcat, (((1,), (1,)), ((), ())), precision=HIGHEST,
                               preferred_element_type=F32)
        col_blk = lax.broadcasted_iota(jnp.int32, (S5_CW, S5_CW), 1) // S5_GROUP
        acc = jnp.zeros((S5_CW, S5_CW), F32)
        for r in range(t):
            sh = r * S5_GROUP if d == 0 else (t - 1 - r) * S5_GROUP
            if sh == 0:
                shifted = k256
            elif d == 0:
                shifted = jnp.concatenate([jnp.zeros((sh, S5_CW), F32), k256[:S5_CW - sh]], axis=0)
            else:
                shifted = jnp.concatenate([k256[sh:], jnp.zeros((sh, S5_CW), F32)], axis=0)
            acc = acc + jnp.where(col_blk == r, shifted, 0.0)
        mt = acc if mt is None else mt + acc

        for m in (float(t), float(t * n_seg_chunks)):
            re, im = cpow(m)
            tab_rows += [re, sgn * im]
        kk = rows(n_seg_chunks)
        re, im = cpow(t * kk) if d == 0 else cpow(t * (n_seg_chunks - 1.0 - kk))
        pow_rows += [re, sgn * im]

    eye = (lax.broadcasted_iota(jnp.int32, (S5_CW, S5_CW), 0) == lax.broadcasted_iota(jnp.int32, (S5_CW, S5_CW), 1))
    mt = mt + jnp.where(eye, dsk_ref[...], 0.0)
    m_ref[...] = mt.T.astype(BF16)
    bp_ref[...] = jnp.concatenate(bp_cols, axis=1).astype(BF16)
    cp_ref[...] = jnp.concatenate(cp_rows, axis=0).astype(BF16)
    tab_ref[...] = jnp.concatenate(tab_rows, axis=0)
    for q in range(4):
        pow_ref[q] = pow_rows[q]


def _s5_prep(lam_re, lam_im, log_dt, b_re, b_im, c_re, c_im, d_skip, n_seg_chunks):
    depth = lam_re.shape[0]
    g, p_, h = S5_GROUPS, S5_STATE, S5_GROUP
    dup = lambda a: jnp.concatenate([a, a], axis=-1)
    dirs_last = lambda a: jnp.moveaxis(a, 1, 2)
    lam = jnp.stack([lam_re, lam_im], axis=3)
    lam = dirs_last(lam).reshape(depth, g, 4, p_)
    ldt = jnp.broadcast_to(dirs_last(log_dt[..., None]), (depth, g, 2, p_))
    par = dup(jnp.concatenate([lam, ldt, jnp.zeros((depth, g, 2, p_), F32)], axis=2))
    bt = jnp.stack([b_re, b_im], axis=3)
    bt = dup(jnp.swapaxes(dirs_last(bt), -1, -2).reshape(depth, g, 4, h, p_))
    ct = dup(dirs_last(jnp.stack([c_re, c_im], axis=3)).reshape(depth, g, 4, h, p_))
    dsk = jnp.tile(d_skip.reshape(depth, g, 1, h), (1, 1, 1, S5_T))

    blk = lambda *shape: pl.BlockSpec((None, None) + shape, lambda l, j: (l, j) + (0,) * len(shape))
    return pl.pallas_call(
        functools.partial(_s5_prep_kernel, n_seg_chunks),
        grid=(depth, g),
        in_specs=[blk(8, LANES), blk(4, h, LANES), blk(4, h, LANES), blk(1, S5_CW)],
        out_specs=[blk(S5_CW, S5_CW), blk(S5_CW, 2 * LANES), blk(2 * LANES, S5_CW), blk(8, LANES),
                   blk(4, n_seg_chunks, LANES)],
        out_shape=[
            jax.ShapeDtypeStruct((depth, g, S5_CW, S5_CW), BF16),
            jax.ShapeDtypeStruct((depth, g, S5_CW, 2 * LANES), BF16),
            jax.ShapeDtypeStruct((depth, g, 2 * LANES, S5_CW), BF16),
            jax.ShapeDtypeStruct((depth, g, 8, LANES), F32),
            jax.ShapeDtypeStruct((depth, g, 4, n_seg_chunks, LANES), F32),
        ],
        name="s5_prep",
    )(par, bt, ct, dsk)


def _cmul_step(s, s_sw, v, v_sw, re, ims):
    return re * s + ims * s_sw + v, re * s_sw - ims * s + v_sw


def _s5_kernel(n_ctx_chunks, n_lat0, ug_ref, m_ref, bp_ref, cp_ref, tab_ref, pow_ref, y_ref,
               vf_ref, vfs_ref, vb_ref, vbs_ref):
    gb, n_rows, _ = ug_ref.shape
    n_seg = (n_rows - n_lat0) // SUBLANES
    pitch = vf_ref.shape[1] // SUBLANES
    half = S5_STATE

    def put_segments(ref, g, val):
        for r in range(SUBLANES):
            ref[g, r * pitch:r * pitch + n_seg, :] = val[r * n_seg:(r + 1) * n_seg]

    tab = tab_ref[...]
    ctx_end = []
    sp_ctx = []
    for g in range(gb):
        v = _dot(ug_ref[g].astype(BF16), bp_ref[g])
        vl = v[n_lat0:]
        put_segments(vf_ref, g, vl[:, :LANES])
        put_segments(vb_ref, g, vl[:, LANES:])
        put_segments(vfs_ref, g, pltpu.roll(vl[:, :LANES], half, 1))
        put_segments(vbs_ref, g, pltpu.roll(vl[:, LANES:], half, 1))

        vc = v[:n_ctx_chunks]
        vcs = jnp.concatenate([pltpu.roll(vc[:, :LANES], half, 1), pltpu.roll(vc[:, LANES:], half, 1)], axis=1)
        zero = jnp.zeros((1, LANES), F32)
        sf, sfs, sb, sbs = zero, zero, zero, zero
        rows_f, rows_b = [], [None] * n_ctx_chunks
        for j in range(n_ctx_chunks):
            jb = n_ctx_chunks - 1 - j
            rows_f.append(sf)
            rows_b[jb] = sb
            sf, sfs = _cmul_step(sf, sfs, vc[j:j + 1, :LANES], vcs[j:j + 1, :LANES], tab[g, 0:1], tab[g, 1:2])
            sb, sbs = _cmul_step(sb, sbs, vc[jb:jb + 1, LANES:], vcs[jb:jb + 1, LANES:], tab[g, 4:5], tab[g, 5:6])
        ctx_end.append((sf, sfs, sb, sbs))
        sp_ctx.append(jnp.concatenate([jnp.concatenate(rows_f, axis=0), jnp.concatenate(rows_b, axis=0)], axis=1))

    coef = [[jnp.broadcast_to(tab[g, q:q + 1], (SUBLANES, LANES)) for q in (0, 1, 4, 5)] for g in range(gb)]

    def step(k, carry):
        kb = n_seg - 1 - k
        out = []
        for g in range(gb):
            sf, sfs, sb, sbs = carry[g]
            are_f, aim_f, are_b, aim_b = coef[g]
            at_f = pl.ds(k, SUBLANES, stride=pitch)
            at_b = pl.ds(kb, SUBLANES, stride=pitch)
            v_f, v_fs = vf_ref[g, at_f, :], vfs_ref[g, at_f, :]
            v_b, v_bs = vb_ref[g, at_b, :], vbs_ref[g, at_b, :]
            vf_ref[g, at_f, :] = sf
            vb_ref[g, at_b, :] = sb
            sf, sfs = _cmul_step(sf, sfs, v_f, v_fs, are_f, aim_f)
            sb, sbs = _cmul_step(sb, sbs, v_b, v_bs, are_b, aim_b)
            out.append((sf, sfs, sb, sbs))
        return tuple(out)

    zero8 = jnp.zeros((SUBLANES, LANES), F32)
    ends = lax.fori_loop(0, n_seg, step, tuple((zero8, zero8, zero8, zero8) for _ in range(gb)))

    for g in range(gb):
        ef, efs, eb, ebs = ends[g]
        sf, sfs, sb, sbs = ctx_end[g]
        ent_f, ent_b = [], [None] * SUBLANES
        for r in range(SUBLANES):
            rb = SUBLANES - 1 - r
            ent_f.append((sf, sfs))
            ent_b[rb] = (sb, sbs)
            sf, sfs = _cmul_step(sf, sfs, ef[r:r + 1], efs[r:r + 1], tab[g, 2:3], tab[g, 3:4])
            sb, sbs = _cmul_step(sb, sbs, eb[rb:rb + 1], ebs[rb:rb + 1], tab[g, 6:7], tab[g, 7:8])

        segs = []
        for r in range(SUBLANES):
            seg_rows = slice(r * pitch, r * pitch + n_seg)
            f = vf_ref[g, seg_rows, :] + pow_ref[g, 0] * ent_f[r][0] + pow_ref[g, 1] * ent_f[r][1]
            b = vb_ref[g, seg_rows, :] + pow_ref[g, 2] * ent_b[r][0] + pow_ref[g, 3] * ent_b[r][1]
            segs.append(jnp.concatenate([f, b], axis=1))
        pad = jnp.zeros((n_lat0 - n_ctx_chunks, 2 * LANES), F32)
        sprev = jnp.concatenate([sp_ctx[g], pad] + segs, axis=0).astype(BF16)
        y_ref[g] = _dot(ug_ref[g].astype(BF16), m_ref[g]) + _dot(sprev, cp_ref[g])


def _s5_apply(ug, m, bp, cp, tab, pw, n_ctx_chunks, n_lat0):
    n_groups, n_rows, cw = ug.shape
    n_seg = pw.shape[-2]
    assert n_rows - n_lat0 == SUBLANES * n_seg and n_lat0 % SUBLANES == 0
    blk = lambda *shape: pl.BlockSpec((S5_GB,) + shape, lambda i: (i,) + (0,) * len(shape))
    seg = pltpu.VMEM((S5_GB, SUBLANES * (n_seg + SUBLANES), LANES), F32)
    return pl.pallas_call(
        functools.partial(_s5_kernel, n_ctx_chunks, n_lat0),
        grid=(n_groups // S5_GB,),
        in_specs=[blk(n_rows, cw), blk(cw, cw), blk(cw, 2 * LANES), blk(2 * LANES, cw), blk(8, LANES),
                  blk(4, n_seg, LANES)],
        out_specs=blk(n_rows, cw),
        out_shape=jax.ShapeDtypeStruct((n_groups, n_rows, cw), F32),
        scratch_shapes=[seg, seg, seg, seg],
        compiler_params=pltpu.CompilerParams(vmem_limit_bytes=VMEM_LIMIT),
        name="s5_scan",
    )(ug, m, bp, cp, tab, pw)


def _mixer_kernel(n_ctx_tiles, ctx_row_len, z_ref, y_ref, mod_ref, gmix_ref, gffn_ref, win_ref, wglu_ref,
                  lng_ref, lnb_ref, wsg_ref, bsg_ref, wconv_ref, wbr_ref, wmg_ref, bmg_ref, wo_ref, wr_ref,
                  br_ref, z1_ref, xs_ref, route_ref, pc_ref):
    d = D_MODEL
    z = z_ref[...]
    mod = mod_ref[...]
    sh1, sc1, ga1 = mod[:, 0:d], mod[:, d:2 * d], mod[:, 2 * d:3 * d]
    sh2, sc2 = mod[:, 3 * d:4 * d], mod[:, 4 * d:5 * d]
    h = _rms(z, gmix_ref[...]) * (1.0 + sc1) + sh1
    hb = h.astype(BF16)
    proj = _dot(hb, win_ref[...])
    su, sv, cgb, cgc, chh = [proj[:, k * MIX_W:(k + 1) * MIX_W] for k in range(5)]

    def merge_term(k, y_k):
        gate = _sigmoid(_dot(hb, wmg_ref[:, k * d:(k + 1) * d]) + bmg_ref[:, k * d:(k + 1) * d])
        return gate * _dot(y_k.astype(BF16), wbr_ref[k])

    yg = jax.nn.gelu(y_ref[...])
    merged = merge_term(0, yg * _sigmoid(_dot(yg.astype(BF16), wglu_ref[...])))

    gv = jax.nn.gelu(sv)
    mu = jnp.mean(gv, axis=-1, keepdims=True)
    var = jnp.mean(jnp.square(gv - mu), axis=-1, keepdims=True)
    vn = ((gv - mu) * lax.rsqrt(var + NORM_EPS) * lng_ref[...] + lnb_ref[...]).astype(BF16)
    bsg = bsg_ref[...]
    chunks = []
    for n in range(TM // SG_CHUNK):
        rows = slice(n * SG_CHUNK, (n + 1) * SG_CHUNK)
        cols = [_dot(wsg_ref[g], vn[rows, g * SG_GROUP_W:(g + 1) * SG_GROUP_W]) for g in range(SG_GROUPS)]
        chunks.append(jnp.concatenate(cols, axis=1) + bsg)
    merged = merged + merge_term(1, jax.nn.gelu(su) * jnp.concatenate(chunks, axis=0))

    zc = cgc * chh
    row = lax.broadcasted_iota(jnp.int32, (TM, 1), 0)
    is_ctx = pl.program_id(0) < n_ctx_tiles
    pos = jnp.where(is_ctx, row % ctx_row_len, row % GRID_W)
    prev = jnp.where(pos == 0, 0.0, pltpu.roll(zc, 1, 0))
    nxt = jnp.where(pos == jnp.where(is_ctx, ctx_row_len - 1, GRID_W - 1), 0.0, pltpu.roll(zc, TM - 1, 0))
    wc = wconv_ref[...]
    merged = merged + merge_term(2, cgb * (wc[0:1] * prev + wc[1:2] * zc + wc[2:3] * nxt))

    z1 = z + ga1 * _dot(merged.astype(BF16), wo_ref[...])
    z1_ref[...] = z1

    h2 = _rms(z1, gffn_ref[...]) * (1.0 + sc2) + sh2
    h2_hi = h2.astype(BF16)
    h2_lo = (h2 - h2_hi.astype(F32)).astype(BF16)
    logits = _dot(h2_hi, wr_ref[0]) + _dot(h2_lo, wr_ref[0]) + _dot(h2_hi, wr_ref[1]) + br_ref[...]
    lane = lax.broadcasted_iota(jnp.int32, (TM, LANES), 1)
    big = jnp.int32(LANES)
    neg = jnp.float32(-jnp.inf)
    is_grp = (lane >= N_EXPERTS) & (lane < N_EXPERTS + N_EXP_GROUPS)
    gl = jnp.where(is_grp, logits, neg)
    ge = jnp.exp(gl - jnp.max(gl, axis=-1, keepdims=True))
    gp = ge / jnp.sum(ge, axis=-1, keepdims=True)
    g_p = jnp.max(gp, axis=-1, keepdims=True)
    g_idx = jnp.min(jnp.where(is_grp & (gp == g_p), lane, big), axis=-1, keepdims=True) - N_EXPERTS
    in_grp = (lane >= g_idx * EXPERTS_PER_GROUP) & (lane < (g_idx + 1) * EXPERTS_PER_GROUP)
    el = jnp.where(in_grp, logits, neg)
    ee = jnp.exp(el - jnp.max(el, axis=-1, keepdims=True))
    ep = ee / jnp.sum(ee, axis=-1, keepdims=True)
    p1 = jnp.max(ep, axis=-1, keepdims=True)
    i1 = jnp.min(jnp.where(in_grp & (ep == p1), lane, big), axis=-1, keepdims=True)
    rest = in_grp & (lane != i1)
    ep2 = jnp.where(rest, ep, -1.0)
    p2 = jnp.max(ep2, axis=-1, keepdims=True)
    i2 = jnp.min(jnp.where(rest & (ep2 == p2), lane, big), axis=-1, keepdims=True)
    tot = p1 + p2
    w_1, w_2 = g_p * (p1 / tot), g_p * (p2 / tot)

    oh1, oh2 = lane == i1, lane == i2
    oh = jnp.where(oh1 | oh2, 1.0, 0.0)
    cnt = jnp.broadcast_to(jnp.sum(oh, axis=0, keepdims=True), (SUBLANES, LANES))
    pc = jnp.floor((cnt + (MOE_BLK - 1)) * (1.0 / MOE_BLK)) * MOE_BLK
    lane8 = lax.broadcasted_iota(jnp.int32, (SUBLANES, LANES), 1)
    incl = pc
    sh = 1
    while sh < N_EXPERTS:
        incl = incl + jnp.where(lane8 >= sh, pltpu.roll(incl, sh, 1), 0.0)
        sh *= 2
    start = (incl - pc)[0:1]
    tri = jnp.where(lax.broadcasted_iota(jnp.int32, (TM, TM), 0) > lax.broadcasted_iota(jnp.int32, (TM, TM), 1),
                    1.0, 0.0).astype(BF16)
    rank = _dot(tri, oh.astype(BF16))
    slot = start + rank
    slot1 = jnp.sum(jnp.where(oh1, slot, 0.0), axis=-1, keepdims=True)
    slot2 = jnp.sum(jnp.where(oh2, slot, 0.0), axis=-1, keepdims=True)
    route_ref[...] = jnp.where(lane == 0, slot1, jnp.where(lane == 1, slot2, jnp.where(lane == 2, w_1,
                               jnp.where(lane == 3, w_2, 0.0))))
    pc_ref[...] = pc.astype(jnp.int32)
    sl = xs_ref.shape[0]
    slot_lane = lax.broadcasted_iota(jnp.int32, (TM, sl), 1).astype(F32)
    perm_t = jnp.where((slot_lane == slot1) | (slot_lane == slot2), 1.0, 0.0).astype(BF16)
    xs_ref[...] = lax.dot_general(perm_t, h2_hi, (((0,), (0,)), ((), ())), preferred_element_type=F32).astype(BF16)


def _mixer(z, ytm, mod, p, n_ctx_tiles, ctx_row_len):
    n = z.shape[0]
    d = D_MODEL
    tile = lambda w: pl.BlockSpec((TM, w), lambda i: (i, 0))
    return pl.pallas_call(
        functools.partial(_mixer_kernel, n_ctx_tiles, ctx_row_len),
        grid=(n // TM,),
        in_specs=[
            tile(d), tile(MIX_W), _mod_spec(n_ctx_tiles),
            _const_spec((1, d)), _const_spec((1, d)),
            _const_spec((d, 5 * MIX_W)), _const_spec((MIX_W, MIX_W)),
            _const_spec((1, MIX_W)), _const_spec((1, MIX_W)),
            _const_spec((SG_GROUPS, SG_CHUNK, SG_CHUNK)), _const_spec((SG_CHUNK, MIX_W)),
            _const_spec((CONV_K, MIX_W)), _const_spec((N_BRANCH, MIX_W, d)),
            _const_spec((d, N_BRANCH * d)), _const_spec((1, N_BRANCH * d)),
            _const_spec((d, d)), _const_spec((2, d, LANES)), _const_spec((1, LANES)),
        ],
        out_specs=[tile(d), pl.BlockSpec((None, MOE_SLOTS, d), lambda i: (i, 0, 0)), tile(LANES),
                   pl.BlockSpec((None, SUBLANES, LANES), lambda i: (i, 0, 0))],
        out_shape=[
            jax.ShapeDtypeStruct((n, d), F32),
            jax.ShapeDtypeStruct((n // TM, MOE_SLOTS, d), BF16),
            jax.ShapeDtypeStruct((n, LANES), F32),
            jax.ShapeDtypeStruct((n // TM, SUBLANES, LANES), jnp.int32),
        ],
        compiler_params=pltpu.CompilerParams(vmem_limit_bytes=VMEM_LIMIT),
        name="mixer",
    )(z, ytm, mod, p["g_mix"], p["g_ffn"], p["w_in5"], p["w_glu"], p["sg_ln_g"], p["sg_ln_b"],
      p["w_sg"], p["b_sg_full"], p["w_conv"], p["w_branch"], p["w_merge"], p["b_merge"], p["w_o"],
      p["w_router"], p["b_router"])


def _plan_kernel(nt, n_work, pc_ref, src_ref, exp_ref, run_ref):
    bpt = MOE_SLOTS // MOE_BLK

    def clear(i, c):
        run_ref[i] = 0
        return c

    lax.fori_loop(0, nt, clear, 0)

    def no_block(b, c):
        src_ref[b] = -1
        return c

    def expert(e, g0):
        def tile(i, pos):
            nb = pc_ref[i, e] // MOE_BLK
            s0 = i * bpt + run_ref[i] // MOE_BLK

            def blk(b, c):
                src_ref[g0 * MOE_NB + pos + b] = s0 + b
                return c

            lax.fori_loop(0, nb, blk, 0)
            run_ref[i] = run_ref[i] + pc_ref[i, e]
            return pos + nb

        pos = lax.fori_loop(0, nt, tile, 0)
        n_tiles = (pos + MOE_NB - 1) // MOE_NB
        lax.fori_loop(g0 * MOE_NB + pos, (g0 + n_tiles) * MOE_NB, no_block, 0)

        def owner(t, c):
            exp_ref[g0 + t] = e
            return c

        lax.fori_loop(0, n_tiles, owner, 0)
        return g0 + n_tiles

    g_end = lax.fori_loop(0, N_EXPERTS, expert, 0)
    lax.fori_loop(g_end * MOE_NB, n_work * MOE_NB, no_block, 0)

    def idle(g, c):
        exp_ref[g] = N_EXPERTS - 1
        return c

    lax.fori_loop(g_end, n_work, idle, 0)


def _moe_plan(pc, n_work):
    nt = pc.shape[0]
    smem = pl.BlockSpec(memory_space=pltpu.SMEM)
    return pl.pallas_call(
        functools.partial(_plan_kernel, nt, n_work),
        in_specs=[smem],
        out_specs=[smem, smem],
        out_shape=[jax.ShapeDtypeStruct((n_work * MOE_NB,), jnp.int32), jax.ShapeDtypeStruct((n_work,), jnp.int32)],
        scratch_shapes=[pltpu.SMEM((nt,), jnp.int32)],
        name="moe_plan",
    )(pc)


def _expert_kernel(src_ref, exp_ref, xs_hbm, w1_ref, w3_ref, w2_ref, yt_in_hbm, yt_hbm, lhs_ref, out_ref,
                   sem_in, sem_out):
    del exp_ref, yt_in_hbm
    g = pl.program_id(0)
    n_work = pl.num_programs(0)

    def gather_copy(step, b):
        slot = step % 2
        return pltpu.make_async_copy(xs_hbm.at[src_ref[step * MOE_NB + b]],
                                     lhs_ref.at[slot, pl.ds(b * MOE_BLK, MOE_BLK)], sem_in.at[slot])

    def scatter_copy(step, b):
        slot = step % 2
        return pltpu.make_async_copy(out_ref.at[slot, pl.ds(b * MOE_BLK, MOE_BLK)],
                                     yt_hbm.at[src_ref[step * MOE_NB + b]], sem_out.at[slot])

    def for_blocks(step, fn):
        def body(b, c):
            @pl.when(src_ref[step * MOE_NB + b] >= 0)
            def _():
                fn(b)
            return c
        lax.fori_loop(0, MOE_NB, body, 0)

    @pl.when(g == 0)
    def _():
        for_blocks(g, lambda b: gather_copy(g, b).start())

    @pl.when(g + 1 < n_work)
    def _():
        for_blocks(g + 1, lambda b: gather_copy(g + 1, b).start())

    for_blocks(g, lambda b: gather_copy(g, b).wait())

    @pl.when(g >= 2)
    def _():
        for_blocks(g - 2, lambda b: scatter_copy(g - 2, b).wait())

    slot = g % 2

    @pl.when(src_ref[g * MOE_NB] >= 0)
    def _():
        def zero_missing(b, c):
            @pl.when(src_ref[g * MOE_NB + b] < 0)
            def _():
                lhs_ref[slot, pl.ds(b * MOE_BLK, MOE_BLK), :] = jnp.zeros((MOE_BLK, D_MODEL), BF16)
            return c
        lax.fori_loop(0, MOE_NB, zero_missing, 0)
        x = lhs_ref[slot]
        a = _dot(x, w1_ref[...].astype(BF16))
        act = a * _sigmoid(a) * _dot(x, w3_ref[...].astype(BF16))
        out_ref[slot] = _dot(act.astype(BF16), w2_ref[...].astype(BF16)).astype(BF16)

    for_blocks(g, lambda b: scatter_copy(g, b).start())

    @pl.when(g == n_work - 1)
    def _():
        @pl.when(g >= 1)
        def _():
            for_blocks(g - 1, lambda b: scatter_copy(g - 1, b).wait())
        for_blocks(g, lambda b: scatter_copy(g, b).wait())


def _moe_experts(xs, src, owner, w1, w3, w2):
    nt, sl, d = xs.shape
    n_work = owner.shape[0]
    xs_blocks = xs.reshape(nt * sl // MOE_BLK, MOE_BLK, d)
    any_spec = pl.BlockSpec(memory_space=pl.ANY)
    by_owner = lambda shape: pl.BlockSpec((None,) + shape, lambda g, src, own: (own[g], 0, 0))
    yt = pl.pallas_call(
        _expert_kernel,
        grid_spec=pltpu.PrefetchScalarGridSpec(
            num_scalar_prefetch=2,
            grid=(n_work,),
            in_specs=[any_spec, by_owner((d, D_FF_EXPERT)), by_owner((d, D_FF_EXPERT)), by_owner((D_FF_EXPERT, d)),
                      any_spec],
            out_specs=any_spec,
            scratch_shapes=[
                pltpu.VMEM((2, MOE_TE, d), BF16), pltpu.VMEM((2, MOE_TE, d), BF16),
                pltpu.SemaphoreType.DMA((2,)), pltpu.SemaphoreType.DMA((2,)),
            ],
        ),
        out_shape=jax.ShapeDtypeStruct(xs_blocks.shape, BF16),
        input_output_aliases={6: 0},
        compiler_params=pltpu.CompilerParams(dimension_semantics=("arbitrary",), vmem_limit_bytes=VMEM_LIMIT),
        name="moe_experts",
    )(src, owner, xs_blocks, w1, w3, w2, jnp.zeros(xs_blocks.shape, BF16))
    return yt.reshape(nt, sl, d)


def _combine_kernel(final, yt_ref, route_ref, z1_ref, mod_ref, g_ref, o_ref):
    route = route_ref[...]
    slot1, slot2, w_1, w_2 = route[:, 0:1], route[:, 1:2], route[:, 2:3], route[:, 3:4]
    yt = yt_ref[...]
    slot_lane = lax.broadcasted_iota(jnp.int32, (TM, MOE_SLOTS), 1).astype(F32)
    pick = lambda s: _dot(jnp.where(slot_lane == s, 1.0, 0.0).astype(BF16), yt)
    moe = w_1 * pick(slot1) + w_2 * pick(slot2)
    ga2 = mod_ref[...][:, 5 * D_MODEL:6 * D_MODEL]
    z2 = z1_ref[...] + ga2 * moe
    o_ref[...] = _rms(z2, g_ref[...]) if final else z2


def _combine(yt, route, z1, mod, g_final, final, n_ctx_tiles):
    n, d = z1.shape
    tile = lambda w: pl.BlockSpec((TM, w), lambda i: (i, 0))
    return pl.pallas_call(
        functools.partial(_combine_kernel, final),
        grid=(n // TM,),
        in_specs=[pl.BlockSpec((None, MOE_SLOTS, d), lambda i: (i, 0, 0)), tile(LANES), tile(d),
                  _mod_spec(n_ctx_tiles), _const_spec((1, d))],
        out_specs=tile(d),
        out_shape=jax.ShapeDtypeStruct((n, d), F32),
        compiler_params=pltpu.CompilerParams(vmem_limit_bytes=VMEM_LIMIT),
        name="combine",
    )(yt, route, z1, mod, g_final)


def kernel(x, c, ctx, c_ctx, w_ada, b_ada, g_mix, g_ffn, w_in, lam_re, lam_im, log_dt, b_re, b_im, c_re, c_im, d_skip, w_glu, sg_ln_g, sg_ln_b, w_sg, b_sg, w_conv, w_branch, w_merge, b_merge, w_o, w_grp, b_grp, w_exp, b_exp, w1, w3, w2, g_final):
    depth = w_ada.shape[0]
    bsz, seq, d = x.shape
    n_ctx = ctx.shape[1]
    ctx_rows = -(-n_ctx // CTX_ALIGN) * CTX_ALIGN
    n = ctx_rows + seq
    assert bsz == 1 and d == D_MODEL
    assert ctx_rows % TM == 0 and seq % TM == 0 and (TM % n_ctx == 0 or n_ctx % TM == 0)
    assert n_ctx % S5_T == 0 and seq % (S5_T * SUBLANES * SUBLANES) == 0
    n_ctx_tiles = ctx_rows // TM
    n_seg_chunks = seq // (S5_T * SUBLANES)
    n_work = (n // TM) * MOE_SLOTS // MOE_TE + N_EXPERTS

    z = jnp.concatenate([ctx[0], jnp.zeros((ctx_rows - n_ctx, d), F32), x[0]], axis=0)
    cc = jnp.zeros((8, d), F32).at[0].set(c_ctx).at[1].set(c[0])
    mods = _ada(cc, w_ada, b_ada)[:, :2].reshape(depth, 2, 1, 6 * d)
    s5_m, s5_bp, s5_cp, s5_tab, s5_pow = _s5_prep(lam_re, lam_im, log_dt, b_re, b_im, c_re, c_im, d_skip, n_seg_chunks)

    for l in range(depth):
        pad = LANES - N_EXPERTS - N_EXP_GROUPS
        p = dict(
            g_mix=g_mix[l].reshape(1, d), g_ffn=g_ffn[l].reshape(1, d),
            w_in5=w_in[l][:, MIX_W:].astype(BF16), w_glu=w_glu[l].astype(BF16),
            sg_ln_g=sg_ln_g[l].reshape(1, MIX_W), sg_ln_b=sg_ln_b[l].reshape(1, MIX_W),
            w_sg=w_sg[l].astype(BF16),
            b_sg_full=jnp.repeat(b_sg[l].T, SG_GROUP_W, axis=1),
            w_conv=w_conv[l], w_branch=w_branch[l].astype(BF16), w_merge=w_merge[l].astype(BF16),
            b_merge=b_merge[l].reshape(1, N_BRANCH * d), w_o=w_o[l].astype(BF16),
            w_router=_split_bf16(jnp.pad(jnp.concatenate([w_exp[l], w_grp[l]], axis=1), ((0, 0), (0, pad)))),
            b_router=jnp.pad(jnp.concatenate([b_exp[l], b_grp[l]]), (0, pad)).reshape(1, LANES),
        )
        mod = mods[l]

        u = _u_proj(z, mod, p["g_mix"], w_in[l][:, :MIX_W].astype(BF16), n_ctx_tiles)
        n_rows = n // S5_T
        ug = u.reshape(n_rows, S5_T, S5_GROUPS, S5_GROUP).transpose(2, 0, 1, 3).reshape(S5_GROUPS, n_rows, S5_CW)
        yg = _s5_apply(ug, s5_m[l], s5_bp[l], s5_cp[l], s5_tab[l], s5_pow[l], n_ctx // S5_T, ctx_rows // S5_T)
        ytm = yg.reshape(S5_GROUPS, n_rows, S5_T, S5_GROUP).transpose(1, 2, 0, 3).reshape(n, MIX_W)

        z1, xs, route, pc = _mixer(z, ytm, mod, p, n_ctx_tiles, min(n_ctx, TM))
        src, owner = _moe_plan(pc[:, 0, :N_EXPERTS], n_work)
        yt = _moe_experts(xs, src, owner, w1[l], w3[l], w2[l])
        last = l == depth - 1
        z = _combine(yt, route, z1, mod, g_final.reshape(1, d) if last else p["g_ffn"], last, n_ctx_tiles)

    return z[ctx_rows:].reshape(bsz, seq, d)
```

```python
import functools

import jax
import jax.numpy as jnp
from jax import lax
from jax.experimental import pallas as pl
from jax.experimental.pallas import tpu as pltpu

F32 = jnp.float32
BF16 = jnp.bfloat16
HIGHEST = lax.Precision.HIGHEST

D_MODEL = 1024
GRID_W = 64
MIX_W = D_MODEL // 2
N_BRANCH = 3
S5_GROUP = 16
S5_GROUPS = MIX_W // S5_GROUP
S5_STATE = 64
SG_CHUNK = 128
SG_GROUPS = 4
SG_GROUP_W = MIX_W // SG_GROUPS
CONV_K = 3
N_EXP_GROUPS = 4
EXPERTS_PER_GROUP = 8
N_EXPERTS = N_EXP_GROUPS * EXPERTS_PER_GROUP
D_FF_EXPERT = D_MODEL // 4
NORM_EPS = 1e-6

LANES = 128
SUBLANES = 8
S5_T = 16
S5_CW = S5_T * S5_GROUP
S5_GB = 4
CTX_ALIGN = 512
TM = 512
MOE_BLK = 16
MOE_SLOTS = 2 * TM + N_EXPERTS * MOE_BLK
MOE_TE = 512
MOE_NB = MOE_TE // MOE_BLK
VMEM_LIMIT = 56 * 1024 * 1024


def _dot(a, b):
    return jnp.dot(a, b, preferred_element_type=F32)


def _dot_hi(a, b):
    return jnp.dot(a, b, preferred_element_type=F32, precision=HIGHEST)


def _sigmoid(x):
    return 0.5 * jnp.tanh(0.5 * x) + 0.5


def _split_bf16(w):
    hi = w.astype(BF16)
    return jnp.stack([hi, (w - hi.astype(F32)).astype(BF16)])


def _rms(x, g):
    return x * lax.rsqrt(jnp.mean(x * x, axis=-1, keepdims=True) + NORM_EPS) * g


def _block_transpose(xs):
    nblk = LANES // S5_GROUP
    assert len(xs) == nblk == 8
    blk = lax.broadcasted_iota(jnp.int32, xs[0].shape, 1) // S5_GROUP
    for k in range(3):
        dist = 1 << k
        hi_half = (blk & dist) != 0
        out = list(xs)
        for a in range(nblk):
            if a & dist == 0:
                p = a + dist
                out[a] = jnp.where(hi_half, pltpu.roll(xs[p], dist * S5_GROUP, 1), xs[a])
                out[p] = jnp.where(hi_half, xs[p], pltpu.roll(xs[a], LANES - dist * S5_GROUP, 1))
        xs = out
    return xs


def _const_spec(shape):
    nd = len(shape)
    return pl.BlockSpec(shape, lambda *_: (0,) * nd, pipeline_mode=pl.Buffered(1))


def _ada_kernel(cc_ref, w_ref, b_ref, o_ref):
    cc = cc_ref[...]
    o_ref[...] = _dot_hi(cc * _sigmoid(cc), w_ref[...]) + b_ref[...]


def _ada(cc, w_ada, b_ada):
    depth = w_ada.shape[0]
    nblk = 6
    return pl.pallas_call(
        _ada_kernel,
        grid=(depth, nblk),
        in_specs=[
            pl.BlockSpec((8, D_MODEL), lambda l, j: (0, 0)),
            pl.BlockSpec((None, D_MODEL, D_MODEL), lambda l, j: (l, 0, j)),
            pl.BlockSpec((None, 1, D_MODEL), lambda l, j: (l, 0, j)),
        ],
        out_specs=pl.BlockSpec((None, 8, D_MODEL), lambda l, j: (l, 0, j)),
        out_shape=jax.ShapeDtypeStruct((depth, 8, 6 * D_MODEL), F32),
        name="ada",
    )(cc, w_ada, b_ada.reshape(depth, 1, 6 * D_MODEL))


def _mod_spec(n_ctx_tiles):
    return pl.BlockSpec((None, 1, 6 * D_MODEL), lambda i: (jnp.where(i < n_ctx_tiles, 0, 1), 0, 0))


def _u_kernel(z_ref, mod_ref, g_ref, w_ref, ug_ref, scr_ref):
    mod = mod_ref[...]
    h = _rms(z_ref[...], g_ref[...]) * (1.0 + mod[:, D_MODEL:2 * D_MODEL]) + mod[:, 0:D_MODEL]
    u = _dot(h.astype(BF16), w_ref[...])
    n_chunks = TM // S5_T
    for q in range(MIX_W // LANES):
        scr_ref[q] = u[:, q * LANES:(q + 1) * LANES]
    for q in range(MIX_W // LANES):
        for half in range(S5_T // SUBLANES):
            rows = [scr_ref[q, pl.ds(half * SUBLANES + a, n_chunks, stride=S5_T), :] for a in range(SUBLANES)]
            for b, y in enumerate(_block_transpose(rows)):
                ug_ref[q * SUBLANES + b, :, half * LANES:(half + 1) * LANES] = y.astype(BF16)


def _u_proj(z, mod, g_mix, w_u, n_ctx_tiles):
    n = z.shape[0]
    return pl.pallas_call(
        _u_kernel,
        grid=(n // TM,),
        in_specs=[
            pl.BlockSpec((TM, D_MODEL), lambda i: (i, 0)),
            _mod_spec(n_ctx_tiles),
            _const_spec((1, D_MODEL)),
            _const_spec((D_MODEL, MIX_W)),
        ],
        out_specs=pl.BlockSpec((S5_GROUPS, TM // S5_T, S5_CW), lambda i: (0, i, 0)),
        out_shape=jax.ShapeDtypeStruct((S5_GROUPS, n // S5_T, S5_CW), BF16),
        scratch_shapes=[pltpu.VMEM((MIX_W // LANES, TM, LANES), F32)],
        name="u_proj",
    )(z, mod, g_mix, w_u)


def _s5_prep_kernel(n_seg_chunks, par_ref, bt_ref, ct_ref, dsk_ref, m_ref, bp_ref, cp_ref, tab_ref, pow_ref):
    t = S5_T
    par = par_ref[...]
    lo = lax.broadcasted_iota(jnp.int32, (1, LANES), 1) < S5_STATE
    sgn = jnp.where(lo, -1.0, 1.0)

    def rows(n):
        return lax.broadcasted_iota(jnp.int32, (n, 1), 0).astype(F32)

    def expand(pw_rows, vec_rows):
        pr, pi = pw_rows
        vr, vi = vec_rows
        pr, pi = pr[:, None, :], pi[:, None, :]
        vr, vi = vr[None, :, :], vi[None, :, :]
        re = (pr * vr - pi * vi).reshape(-1, LANES)
        im = (pr * vi + pi * vr).reshape(-1, LANES)
        return re, im

    mt = None
    bp_cols, cp_rows, tab_rows, pow_rows = [], [], [], []
    for d in range(2):
        lre, lim = par[2 * d:2 * d + 1], par[2 * d + 1:2 * d + 2]
        dt = jnp.exp(par[4 + d:5 + d])
        xr, xi = lre * dt, lim * dt

        def cpow(m):
            mag = jnp.exp(m * xr)
            return mag * jnp.cos(m * xi), mag * jnp.sin(m * xi)

        ar, ai = cpow(1.0)
        den = lre * lre + lim * lim
        qr = ((ar - 1.0) * lre + ai * lim) / den
        qi = (ai * lre - (ar - 1.0) * lim) / den
        br, bi = bt_ref[2 * d], bt_ref[2 * d + 1]
        bb = (qr * br - qi * bi, qr * bi + qi * br)
        cc = (ct_ref[2 * d], ct_ref[2 * d + 1])

        inj_pow = cpow(t - 1.0 - rows(t)) if d == 0 else cpow(rows(t))
        re, im = expand(inj_pow, bb)
        bp_cols.append(jnp.where(lo, re, im))

        out_pow = cpow(rows(t) + 1.0) if d == 0 else cpow(t - rows(t))
        re, im = expand(out_pow, cc)
        cp_rows.append(jnp.where(lo, re, -im).T)

        lag_pow = cpow(rows(t)) if d == 0 else cpow(t - 1.0 - rows(t))
        re, im = expand(lag_pow, cc)
        ctab = jnp.where(lo, re, -im)
        bcat = jnp.broadcast_to(jnp.where(lo, bb[0], bb[1])[None], (t, S5_GROUP, LANES)).reshape(-1, LANES)
        k256 = lax.dot_general(ctab, bcat, (((1,), (1,)), ((), ())), precision=HIGHEST,
                               preferred_element_type=F32)
        col_blk = lax.broadcasted_iota(jnp.int32, (S5_CW, S5_CW), 1) // S5_GROUP
        acc = jnp.zeros((S5_CW, S5_CW), F32)
        for r in range(t):
            sh = r * S5_GROUP if d == 0 else (t - 1 - r) * S5_GROUP
            if sh == 0:
                shifted = k256
            elif d == 0:
                shifted = jnp.concatenate([jnp.zeros((sh, S5_CW), F32), k256[:S5_CW - sh]], axis=0)
            else:
                shifted = jnp.concatenate([k256[sh:], jnp.zeros((sh, S5_CW), F32)], axis=0)
            acc = acc + jnp.where(col_blk == r, shifted, 0.0)
        mt = acc if mt is None else mt + acc

        for m in (float(t), float(t * n_seg_chunks)):
            re, im = cpow(m)
            tab_rows += [re, sgn * im]
        kk = rows(n_seg_chunks)
        re, im = cpow(t * kk) if d == 0 else cpow(t * (n_seg_chunks - 1.0 - kk))
        pow_rows += [re, sgn * im]

    eye = (lax.broadcasted_iota(jnp.int32, (S5_CW, S5_CW), 0) == lax.broadcasted_iota(jnp.int32, (S5_CW, S5_CW), 1))
    mt = mt + jnp.where(eye, dsk_ref[...], 0.0)
    m_ref[...] = mt.T.astype(BF16)
    bp_ref[...] = jnp.concatenate(bp_cols, axis=1).astype(BF16)
    cp_ref[...] = jnp.concatenate(cp_rows, axis=0).astype(BF16)
    tab_ref[...] = jnp.concatenate(tab_rows, axis=0)
    for q in range(4):
        pow_ref[q] = pow_rows[q]


def _s5_prep(lam_re, lam_im, log_dt, b_re, b_im, c_re, c_im, d_skip, n_seg_chunks):
    depth = lam_re.shape[0]
    g, p_, h = S5_GROUPS, S5_STATE, S5_GROUP
    dup = lambda a: jnp.concatenate([a, a], axis=-1)
    dirs_last = lambda a: jnp.moveaxis(a, 1, 2)
    lam = jnp.stack([lam_re, lam_im], axis=3)
    lam = dirs_last(lam).reshape(depth, g, 4, p_)
    ldt = jnp.broadcast_to(dirs_last(log_dt[..., None]), (depth, g, 2, p_))
    par = dup(jnp.concatenate([lam, ldt, jnp.zeros((depth, g, 2, p_), F32)], axis=2))
    bt = jnp.stack([b_re, b_im], axis=3)
    bt = dup(jnp.swapaxes(dirs_last(bt), -1, -2).reshape(depth, g, 4, h, p_))
    ct = dup(dirs_last(jnp.stack([c_re, c_im], axis=3)).reshape(depth, g, 4, h, p_))
    dsk = jnp.tile(d_skip.reshape(depth, g, 1, h), (1, 1, 1, S5_T))

    blk = lambda *shape: pl.BlockSpec((None, None) + shape, lambda l, j: (l, j) + (0,) * len(shape))
    return pl.pallas_call(
        functools.partial(_s5_prep_kernel, n_seg_chunks),
        grid=(depth, g),
        in_specs=[blk(8, LANES), blk(4, h, LANES), blk(4, h, LANES), blk(1, S5_CW)],
        out_specs=[blk(S5_CW, S5_CW), blk(S5_CW, 2 * LANES), blk(2 * LANES, S5_CW), blk(8, LANES),
                   blk(4, n_seg_chunks, LANES)],
        out_shape=[
            jax.ShapeDtypeStruct((depth, g, S5_CW, S5_CW), BF16),
            jax.ShapeDtypeStruct((depth, g, S5_CW, 2 * LANES), BF16),
            jax.ShapeDtypeStruct((depth, g, 2 * LANES, S5_CW), BF16),
            jax.ShapeDtypeStruct((depth, g, 8, LANES), F32),
            jax.ShapeDtypeStruct((depth, g, 4, n_seg_chunks, LANES), F32),
        ],
        name="s5_prep",
    )(par, bt, ct, dsk)


def _cmul_step(s, s_sw, v, v_sw, re, ims):
    return re * s + ims * s_sw + v, re * s_sw - ims * s + v_sw


def _s5_kernel(n_ctx_chunks, n_lat0, ug_ref, m_ref, bp_ref, cp_ref, tab_ref, pow_ref, y_ref,
               vf_ref, vfs_ref, vb_ref, vbs_ref):
    gb, n_rows, _ = ug_ref.shape
    n_seg = (n_rows - n_lat0) // SUBLANES
    pitch = vf_ref.shape[1] // SUBLANES
    half = S5_STATE

    def put_segments(ref, g, val):
        for r in range(SUBLANES):
            ref[g, r * pitch:r * pitch + n_seg, :] = val[r * n_seg:(r + 1) * n_seg]

    tab = tab_ref[...]
    ctx_end = []
    sp_ctx = []
    for g in range(gb):
        v = _dot(ug_ref[g], bp_ref[g])
        vl = v[n_lat0:]
        put_segments(vf_ref, g, vl[:, :LANES])
        put_segments(vb_ref, g, vl[:, LANES:])
        put_segments(vfs_ref, g, pltpu.roll(vl[:, :LANES], half, 1))
        put_segments(vbs_ref, g, pltpu.roll(vl[:, LANES:], half, 1))

        vc = v[:n_ctx_chunks]
        vcs = jnp.concatenate([pltpu.roll(vc[:, :LANES], half, 1), pltpu.roll(vc[:, LANES:], half, 1)], axis=1)
        zero = jnp.zeros((1, LANES), F32)
        sf, sfs, sb, sbs = zero, zero, zero, zero
        rows_f, rows_b = [], [None] * n_ctx_chunks
        for j in range(n_ctx_chunks):
            jb = n_ctx_chunks - 1 - j
            rows_f.append(sf)
            rows_b[jb] = sb
            sf, sfs = _cmul_step(sf, sfs, vc[j:j + 1, :LANES], vcs[j:j + 1, :LANES], tab[g, 0:1], tab[g, 1:2])
            sb, sbs = _cmul_step(sb, sbs, vc[jb:jb + 1, LANES:], vcs[jb:jb + 1, LANES:], tab[g, 4:5], tab[g, 5:6])
        ctx_end.append((sf, sfs, sb, sbs))
        sp_ctx.append(jnp.concatenate([jnp.concatenate(rows_f, axis=0), jnp.concatenate(rows_b, axis=0)], axis=1))

    coef = [[jnp.broadcast_to(tab[g, q:q + 1], (SUBLANES, LANES)) for q in (0, 1, 4, 5)] for g in range(gb)]

    def step(k, carry):
        kb = n_seg - 1 - k
        out = []
        for g in range(gb):
            sf, sfs, sb, sbs = carry[g]
            are_f, aim_f, are_b, aim_b = coef[g]
            at_f = pl.ds(k, SUBLANES, stride=pitch)
            at_b = pl.ds(kb, SUBLANES, stride=pitch)
            v_f, v_fs = vf_ref[g, at_f, :], vfs_ref[g, at_f, :]
            v_b, v_bs = vb_ref[g, at_b, :], vbs_ref[g, at_b, :]
            vf_ref[g, at_f, :] = sf
            vb_ref[g, at_b, :] = sb
            sf, sfs = _cmul_step(sf, sfs, v_f, v_fs, are_f, aim_f)
            sb, sbs = _cmul_step(sb, sbs, v_b, v_bs, are_b, aim_b)
            out.append((sf, sfs, sb, sbs))
        return tuple(out)

    zero8 = jnp.zeros((SUBLANES, LANES), F32)
    ends = lax.fori_loop(0, n_seg, step, tuple((zero8, zero8, zero8, zero8) for _ in range(gb)))

    for g in range(gb):
        ef, efs, eb, ebs = ends[g]
        sf, sfs, sb, sbs = ctx_end[g]
        ent_f, ent_b = [], [None] * SUBLANES
        for r in range(SUBLANES):
            rb = SUBLANES - 1 - r
            ent_f.append((sf, sfs))
            ent_b[rb] = (sb, sbs)
            sf, sfs = _cmul_step(sf, sfs, ef[r:r + 1], efs[r:r + 1], tab[g, 2:3], tab[g, 3:4])
            sb, sbs = _cmul_step(sb, sbs, eb[rb:rb + 1], ebs[rb:rb + 1], tab[g, 6:7], tab[g, 7:8])

        segs = []
        for r in range(SUBLANES):
            seg_rows = slice(r * pitch, r * pitch + n_seg)
            f = vf_ref[g, seg_rows, :] + pow_ref[g, 0] * ent_f[r][0] + pow_ref[g, 1] * ent_f[r][1]
            b = vb_ref[g, seg_rows, :] + pow_ref[g, 2] * ent_b[r][0] + pow_ref[g, 3] * ent_b[r][1]
            segs.append(jnp.concatenate([f, b], axis=1))
        pad = jnp.zeros((n_lat0 - n_ctx_chunks, 2 * LANES), F32)
        sprev = jnp.concatenate([sp_ctx[g], pad] + segs, axis=0).astype(BF16)
        y_ref[g] = _dot(ug_ref[g], m_ref[g]) + _dot(sprev, cp_ref[g])


def _s5_apply(ug, m, bp, cp, tab, pw, n_ctx_chunks, n_lat0):
    n_groups, n_rows, cw = ug.shape
    n_seg = pw.shape[-2]
    assert n_rows - n_lat0 == SUBLANES * n_seg and n_lat0 % SUBLANES == 0
    blk = lambda *shape: pl.BlockSpec((S5_GB,) + shape, lambda i: (i,) + (0,) * len(shape))
    seg = pltpu.VMEM((S5_GB, SUBLANES * (n_seg + SUBLANES), LANES), F32)
    return pl.pallas_call(
        functools.partial(_s5_kernel, n_ctx_chunks, n_lat0),
        grid=(n_groups // S5_GB,),
        in_specs=[blk(n_rows, cw), blk(cw, cw), blk(cw, 2 * LANES), blk(2 * LANES, cw), blk(8, LANES),
                  blk(4, n_seg, LANES)],
        out_specs=blk(n_rows, cw),
        out_shape=jax.ShapeDtypeStruct((n_groups, n_rows, cw), F32),
        scratch_shapes=[seg, seg, seg, seg],
        compiler_params=pltpu.CompilerParams(vmem_limit_bytes=VMEM_LIMIT),
        name="s5_scan",
    )(ug, m, bp, cp, tab, pw)


def _mixer_kernel(n_ctx_tiles, ctx_row_len, z_ref, y_ref, mod_ref, gmix_ref, gffn_ref, win_ref, wglu_ref,
                  lng_ref, lnb_ref, wsg_ref, bsg_ref, wconv_ref, wbr_ref, wmg_ref, bmg_ref, wo_ref, wr_ref,
                  br_ref, z1_ref, xs_ref, route_ref, pc_ref, yscr_ref):
    d = D_MODEL
    n_chunks = TM // S5_T
    for q in range(MIX_W // LANES):
        for half in range(S5_T // SUBLANES):
            cols = [y_ref[q * SUBLANES + b, :, half * LANES:(half + 1) * LANES] for b in range(SUBLANES)]
            for a, rows in enumerate(_block_transpose(cols)):
                yscr_ref[q, pl.ds(half * SUBLANES + a, n_chunks, stride=S5_T), :] = rows
    y_tok = jnp.concatenate([yscr_ref[q] for q in range(MIX_W // LANES)], axis=1)
    z = z_ref[...]
    mod = mod_ref[...]
    sh1, sc1, ga1 = mod[:, 0:d], mod[:, d:2 * d], mod[:, 2 * d:3 * d]
    sh2, sc2 = mod[:, 3 * d:4 * d], mod[:, 4 * d:5 * d]
    h = _rms(z, gmix_ref[...]) * (1.0 + sc1) + sh1
    hb = h.astype(BF16)
    proj = _dot(hb, win_ref[...])
    su, sv, cgb, cgc, chh = [proj[:, k * MIX_W:(k + 1) * MIX_W] for k in range(5)]

    def merge_term(k, y_k):
        gate = _sigmoid(_dot(hb, wmg_ref[:, k * d:(k + 1) * d]) + bmg_ref[:, k * d:(k + 1) * d])
        return gate * _dot(y_k.astype(BF16), wbr_ref[k])

    yg = jax.nn.gelu(y_tok)
    merged = merge_term(0, yg * _sigmoid(_dot(yg.astype(BF16), wglu_ref[...])))

    gv = jax.nn.gelu(sv)
    mu = jnp.mean(gv, axis=-1, keepdims=True)
    var = jnp.mean(jnp.square(gv - mu), axis=-1, keepdims=True)
    vn = ((gv - mu) * lax.rsqrt(var + NORM_EPS) * lng_ref[...] + lnb_ref[...]).astype(BF16)
    bsg = bsg_ref[...]
    chunks = []
    for n in range(TM // SG_CHUNK):
        rows = slice(n * SG_CHUNK, (n + 1) * SG_CHUNK)
        cols = [_dot(wsg_ref[g], vn[rows, g * SG_GROUP_W:(g + 1) * SG_GROUP_W]) for g in range(SG_GROUPS)]
        chunks.append(jnp.concatenate(cols, axis=1) + bsg)
    merged = merged + merge_term(1, jax.nn.gelu(su) * jnp.concatenate(chunks, axis=0))

    zc = cgc * chh
    row = lax.broadcasted_iota(jnp.int32, (TM, 1), 0)
    is_ctx = pl.program_id(0) < n_ctx_tiles
    pos = jnp.where(is_ctx, row % ctx_row_len, row % GRID_W)
    prev = jnp.where(pos == 0, 0.0, pltpu.roll(zc, 1, 0))
    nxt = jnp.where(pos == jnp.where(is_ctx, ctx_row_len - 1, GRID_W - 1), 0.0, pltpu.roll(zc, TM - 1, 0))
    wc = wconv_ref[...]
    merged = merged + merge_term(2, cgb * (wc[0:1] * prev + wc[1:2] * zc + wc[2:3] * nxt))

    z1 = z + ga1 * _dot(merged.astype(BF16), wo_ref[...])
    z1_ref[...] = z1

    h2 = _rms(z1, gffn_ref[...]) * (1.0 + sc2) + sh2
    h2_hi = h2.astype(BF16)
    h2_lo = (h2 - h2_hi.astype(F32)).astype(BF16)
    logits = _dot(h2_hi, wr_ref[0]) + _dot(h2_lo, wr_ref[0]) + _dot(h2_hi, wr_ref[1]) + br_ref[...]
    lane = lax.broadcasted_iota(jnp.int32, (TM, LANES), 1)
    big = jnp.int32(LANES)
    neg = jnp.float32(-jnp.inf)
    is_grp = (lane >= N_EXPERTS) & (lane < N_EXPERTS + N_EXP_GROUPS)
    gl = jnp.where(is_grp, logits, neg)
    ge = jnp.exp(gl - jnp.max(gl, axis=-1, keepdims=True))
    gp = ge / jnp.sum(ge, axis=-1, keepdims=True)
    g_p = jnp.max(gp, axis=-1, keepdims=True)
    g_idx = jnp.min(jnp.where(is_grp & (gp == g_p), lane, big), axis=-1, keepdims=True) - N_EXPERTS
    in_grp = (lane >= g_idx * EXPERTS_PER_GROUP) & (lane < (g_idx + 1) * EXPERTS_PER_GROUP)
    el = jnp.where(in_grp, logits, neg)
    ee = jnp.exp(el - jnp.max(el, axis=-1, keepdims=True))
    ep = ee / jnp.sum(ee, axis=-1, keepdims=True)
    p1 = jnp.max(ep, axis=-1, keepdims=True)
    i1 = jnp.min(jnp.where(in_grp & (ep == p1), lane, big), axis=-1, keepdims=True)
    rest = in_grp & (lane != i1)
    ep2 = jnp.where(rest, ep, -1.0)
    p2 = jnp.max(ep2, axis=-1, keepdims=True)
    i2 = jnp.min(jnp.where(rest & (ep2 == p2), lane, big), axis=-1, keepdims=True)
    tot = p1 + p2
    w_1, w_2 = g_p * (p1 / tot), g_p * (p2 / tot)

    oh1, oh2 = lane == i1, lane == i2
    oh = jnp.where(oh1 | oh2, 1.0, 0.0)
    cnt = jnp.broadcast_to(jnp.sum(oh, axis=0, keepdims=True), (SUBLANES, LANES))
    pc = jnp.floor((cnt + (MOE_BLK - 1)) * (1.0 / MOE_BLK)) * MOE_BLK
    lane8 = lax.broadcasted_iota(jnp.int32, (SUBLANES, LANES), 1)
    incl = pc
    sh = 1
    while sh < N_EXPERTS:
        incl = incl + jnp.where(lane8 >= sh, pltpu.roll(incl, sh, 1), 0.0)
        sh *= 2
    start = (incl - pc)[0:1]
    tri = jnp.where(lax.broadcasted_iota(jnp.int32, (TM, TM), 0) > lax.broadcasted_iota(jnp.int32, (TM, TM), 1),
                    1.0, 0.0).astype(BF16)
    rank = _dot(tri, oh.astype(BF16))
    slot = start + rank
    slot1 = jnp.sum(jnp.where(oh1, slot, 0.0), axis=-1, keepdims=True)
    slot2 = jnp.sum(jnp.where(oh2, slot, 0.0), axis=-1, keepdims=True)
    route_ref[...] = jnp.where(lane == 0, slot1, jnp.where(lane == 1, slot2, jnp.where(lane == 2, w_1,
                               jnp.where(lane == 3, w_2, 0.0))))
    pc_ref[...] = pc.astype(jnp.int32)
    sl = xs_ref.shape[0]
    slot_lane = lax.broadcasted_iota(jnp.int32, (TM, sl), 1).astype(F32)
    perm_t = jnp.where((slot_lane == slot1) | (slot_lane == slot2), 1.0, 0.0).astype(BF16)
    xs_ref[...] = lax.dot_general(perm_t, h2_hi, (((0,), (0,)), ((), ())), preferred_element_type=F32).astype(BF16)


def _mixer(z, ytm, mod, p, n_ctx_tiles, ctx_row_len):
    n = z.shape[0]
    d = D_MODEL
    tile = lambda w: pl.BlockSpec((TM, w), lambda i: (i, 0))
    return pl.pallas_call(
        functools.partial(_mixer_kernel, n_ctx_tiles, ctx_row_len),
        grid=(n // TM,),
        in_specs=[
            tile(d), pl.BlockSpec((S5_GROUPS, TM // S5_T, S5_CW), lambda i: (0, i, 0)), _mod_spec(n_ctx_tiles),
            _const_spec((1, d)), _const_spec((1, d)),
            _const_spec((d, 5 * MIX_W)), _const_spec((MIX_W, MIX_W)),
            _const_spec((1, MIX_W)), _const_spec((1, MIX_W)),
            _const_spec((SG_GROUPS, SG_CHUNK, SG_CHUNK)), _const_spec((SG_CHUNK, MIX_W)),
            _const_spec((CONV_K, MIX_W)), _const_spec((N_BRANCH, MIX_W, d)),
            _const_spec((d, N_BRANCH * d)), _const_spec((1, N_BRANCH * d)),
            _const_spec((d, d)), _const_spec((2, d, LANES)), _const_spec((1, LANES)),
        ],
        out_specs=[tile(d), pl.BlockSpec((None, MOE_SLOTS, d), lambda i: (i, 0, 0)), tile(LANES),
                   pl.BlockSpec((None, SUBLANES, LANES), lambda i: (i, 0, 0))],
        out_shape=[
            jax.ShapeDtypeStruct((n, d), F32),
            jax.ShapeDtypeStruct((n // TM, MOE_SLOTS, d), BF16),
            jax.ShapeDtypeStruct((n, LANES), F32),
            jax.ShapeDtypeStruct((n // TM, SUBLANES, LANES), jnp.int32),
        ],
        scratch_shapes=[pltpu.VMEM((MIX_W // LANES, TM, LANES), F32)],
        compiler_params=pltpu.CompilerParams(vmem_limit_bytes=VMEM_LIMIT),
        name="mixer",
    )(z, ytm, mod, p["g_mix"], p["g_ffn"], p["w_in5"], p["w_glu"], p["sg_ln_g"], p["sg_ln_b"],
      p["w_sg"], p["b_sg_full"], p["w_conv"], p["w_branch"], p["w_merge"], p["b_merge"], p["w_o"],
      p["w_router"], p["b_router"])


def _plan_kernel(nt, n_work, pc_ref, src_ref, exp_ref, run_ref):
    bpt = MOE_SLOTS // MOE_BLK

    def clear(i, c):
        run_ref[i] = 0
        return c

    lax.fori_loop(0, nt, clear, 0)

    def no_block(b, c):
        src_ref[b] = -1
        return c

    def expert(e, g0):
        def tile(i, pos):
            nb = pc_ref[i, e] // MOE_BLK
            s0 = i * bpt + run_ref[i] // MOE_BLK

            def blk(b, c):
                src_ref[g0 * MOE_NB + pos + b] = s0 + b
                return c

            lax.fori_loop(0, nb, blk, 0)
            run_ref[i] = run_ref[i] + pc_ref[i, e]
            return pos + nb

        pos = lax.fori_loop(0, nt, tile, 0)
        n_tiles = (pos + MOE_NB - 1) // MOE_NB
        lax.fori_loop(g0 * MOE_NB + pos, (g0 + n_tiles) * MOE_NB, no_block, 0)

        def owner(t, c):
            exp_ref[g0 + t] = e
            return c

        lax.fori_loop(0, n_tiles, owner, 0)
        return g0 + n_tiles

    g_end = lax.fori_loop(0, N_EXPERTS, expert, 0)
    lax.fori_loop(g_end * MOE_NB, n_work * MOE_NB, no_block, 0)

    def idle(g, c):
        exp_ref[g] = N_EXPERTS - 1
        return c

    lax.fori_loop(g_end, n_work, idle, 0)


def _moe_plan(pc, n_work):
    nt = pc.shape[0]
    smem = pl.BlockSpec(memory_space=pltpu.SMEM)
    return pl.pallas_call(
        functools.partial(_plan_kernel, nt, n_work),
        in_specs=[smem],
        out_specs=[smem, smem],
        out_shape=[jax.ShapeDtypeStruct((n_work * MOE_NB,), jnp.int32), jax.ShapeDtypeStruct((n_work,), jnp.int32)],
        scratch_shapes=[pltpu.SMEM((nt,), jnp.int32)],
        name="moe_plan",
    )(pc)


def _expert_kernel(src_ref, exp_ref, xs_hbm, w1_ref, w3_ref, w2_ref, yt_in_hbm, yt_hbm, lhs_ref, out_ref,
                   sem_in, sem_out):
    del exp_ref, yt_in_hbm
    g = pl.program_id(0)
    n_work = pl.num_programs(0)

    def gather_copy(step, b):
        slot = step % 2
        return pltpu.make_async_copy(xs_hbm.at[src_ref[step * MOE_NB + b]],
                                     lhs_ref.at[slot, pl.ds(b * MOE_BLK, MOE_BLK)], sem_in.at[slot])

    def scatter_copy(step, b):
        slot = step % 2
        return pltpu.make_async_copy(out_ref.at[slot, pl.ds(b * MOE_BLK, MOE_BLK)],
                                     yt_hbm.at[src_ref[step * MOE_NB + b]], sem_out.at[slot])

    def for_blocks(step, fn):
        def body(b, c):
            @pl.when(src_ref[step * MOE_NB + b] >= 0)
            def _():
                fn(b)
            return c
        lax.fori_loop(0, MOE_NB, body, 0)

    @pl.when(g == 0)
    def _():
        for_blocks(g, lambda b: gather_copy(g, b).start())

    @pl.when(g + 1 < n_work)
    def _():
        for_blocks(g + 1, lambda b: gather_copy(g + 1, b).start())

    for_blocks(g, lambda b: gather_copy(g, b).wait())

    @pl.when(g >= 2)
    def _():
        for_blocks(g - 2, lambda b: scatter_copy(g - 2, b).wait())

    slot = g % 2

    @pl.when(src_ref[g * MOE_NB] >= 0)
    def _():
        def zero_missing(b, c):
            @pl.when(src_ref[g * MOE_NB + b] < 0)
            def _():
                lhs_ref[slot, pl.ds(b * MOE_BLK, MOE_BLK), :] = jnp.zeros((MOE_BLK, D_MODEL), BF16)
            return c
        lax.fori_loop(0, MOE_NB, zero_missing, 0)
        x = lhs_ref[slot]
        a = _dot(x, w1_ref[...].astype(BF16))
        act = a * _sigmoid(a) * _dot(x, w3_ref[...].astype(BF16))
        out_ref[slot] = _dot(act.astype(BF16), w2_ref[...].astype(BF16)).astype(BF16)

    for_blocks(g, lambda b: scatter_copy(g, b).start())

    @pl.when(g == n_work - 1)
    def _():
        @pl.when(g >= 1)
        def _():
            for_blocks(g - 1, lambda b: scatter_copy(g - 1, b).wait())
        for_blocks(g, lambda b: scatter_copy(g, b).wait())


def _moe_experts(xs, src, owner, w1, w3, w2):
    nt, sl, d = xs.shape
    n_work = owner.shape[0]
    xs_blocks = xs.reshape(nt * sl // MOE_BLK, MOE_BLK, d)
    any_spec = pl.BlockSpec(memory_space=pl.ANY)
    by_owner = lambda shape: pl.BlockSpec((None,) + shape, lambda g, src, own: (own[g], 0, 0))
    yt = pl.pallas_call(
        _expert_kernel,
        grid_spec=pltpu.PrefetchScalarGridSpec(
            num_scalar_prefetch=2,
            grid=(n_work,),
            in_specs=[any_spec, by_owner((d, D_FF_EXPERT)), by_owner((d, D_FF_EXPERT)), by_owner((D_FF_EXPERT, d)),
                      any_spec],
            out_specs=any_spec,
            scratch_shapes=[
                pltpu.VMEM((2, MOE_TE, d), BF16), pltpu.VMEM((2, MOE_TE, d), BF16),
                pltpu.SemaphoreType.DMA((2,)), pltpu.SemaphoreType.DMA((2,)),
            ],
        ),
        out_shape=jax.ShapeDtypeStruct(xs_blocks.shape, BF16),
        input_output_aliases={6: 0},
        compiler_params=pltpu.CompilerParams(dimension_semantics=("arbitrary",), vmem_limit_bytes=VMEM_LIMIT),
        name="moe_experts",
    )(src, owner, xs_blocks, w1, w3, w2, jnp.zeros(xs_blocks.shape, BF16))
    return yt.reshape(nt, sl, d)


def _combine_kernel(final, yt_ref, route_ref, z1_ref, mod_ref, g_ref, o_ref):
    route = route_ref[...]
    slot1, slot2, w_1, w_2 = route[:, 0:1], route[:, 1:2], route[:, 2:3], route[:, 3:4]
    yt = yt_ref[...]
    slot_lane = lax.broadcasted_iota(jnp.int32, (TM, MOE_SLOTS), 1).astype(F32)
    pick = lambda s: _dot(jnp.where(slot_lane == s, 1.0, 0.0).astype(BF16), yt)
    moe = w_1 * pick(slot1) + w_2 * pick(slot2)
    ga2 = mod_ref[...][:, 5 * D_MODEL:6 * D_MODEL]
    z2 = z1_ref[...] + ga2 * moe
    o_ref[...] = _rms(z2, g_ref[...]) if final else z2


def _combine(yt, route, z1, mod, g_final, final, n_ctx_tiles):
    n, d = z1.shape
    tile = lambda w: pl.BlockSpec((TM, w), lambda i: (i, 0))
    return pl.pallas_call(
        functools.partial(_combine_kernel, final),
        grid=(n // TM,),
        in_specs=[pl.BlockSpec((None, MOE_SLOTS, d), lambda i: (i, 0, 0)), tile(LANES), tile(d),
                  _mod_spec(n_ctx_tiles), _const_spec((1, d))],
        out_specs=tile(d),
        out_shape=jax.ShapeDtypeStruct((n, d), F32),
        compiler_params=pltpu.CompilerParams(vmem_limit_bytes=VMEM_LIMIT),
        name="combine",
    )(yt, route, z1, mod, g_final)


def kernel(x, c, ctx, c_ctx, w_ada, b_ada, g_mix, g_ffn, w_in, lam_re, lam_im, log_dt, b_re, b_im, c_re, c_im, d_skip, w_glu, sg_ln_g, sg_ln_b, w_sg, b_sg, w_conv, w_branch, w_merge, b_merge, w_o, w_grp, b_grp, w_exp, b_exp, w1, w3, w2, g_final):
    depth = w_ada.shape[0]
    bsz, seq, d = x.shape
    n_ctx = ctx.shape[1]
    ctx_rows = -(-n_ctx // CTX_ALIGN) * CTX_ALIGN
    n = ctx_rows + seq
    assert bsz == 1 and d == D_MODEL
    assert ctx_rows % TM == 0 and seq % TM == 0 and (TM % n_ctx == 0 or n_ctx % TM == 0)
    assert n_ctx % S5_T == 0 and seq % (S5_T * SUBLANES * SUBLANES) == 0
    n_ctx_tiles = ctx_rows // TM
    n_seg_chunks = seq // (S5_T * SUBLANES)
    n_work = (n // TM) * MOE_SLOTS // MOE_TE + N_EXPERTS

    z = jnp.concatenate([ctx[0], jnp.zeros((ctx_rows - n_ctx, d), F32), x[0]], axis=0)
    cc = jnp.zeros((8, d), F32).at[0].set(c_ctx).at[1].set(c[0])
    mods = _ada(cc, w_ada, b_ada)[:, :2].reshape(depth, 2, 1, 6 * d)
    s5_m, s5_bp, s5_cp, s5_tab, s5_pow = _s5_prep(lam_re, lam_im, log_dt, b_re, b_im, c_re, c_im, d_skip, n_seg_chunks)

    for l in range(depth):
        pad = LANES - N_EXPERTS - N_EXP_GROUPS
        p = dict(
            g_mix=g_mix[l].reshape(1, d), g_ffn=g_ffn[l].reshape(1, d),
            w_in5=w_in[l][:, MIX_W:].astype(BF16), w_glu=w_glu[l].astype(BF16),
            sg_ln_g=sg_ln_g[l].reshape(1, MIX_W), sg_ln_b=sg_ln_b[l].reshape(1, MIX_W),
            w_sg=w_sg[l].astype(BF16),
            b_sg_full=jnp.repeat(b_sg[l].T, SG_GROUP_W, axis=1),
            w_conv=w_conv[l], w_branch=w_branch[l].astype(BF16), w_merge=w_merge[l].astype(BF16),
            b_merge=b_merge[l].reshape(1, N_BRANCH * d), w_o=w_o[l].astype(BF16),
            w_router=_split_bf16(jnp.pad(jnp.concatenate([w_exp[l], w_grp[l]], axis=1), ((0, 0), (0, pad)))),
            b_router=jnp.pad(jnp.concatenate([b_exp[l], b_grp[l]]), (0, pad)).reshape(1, LANES),
        )
        mod = mods[l]

        ug = _u_proj(z, mod, p["g_mix"], w_in[l][:, :MIX_W].astype(BF16), n_ctx_tiles)
        yg = _s5_apply(ug, s5_m[l], s5_bp[l], s5_cp[l], s5_tab[l], s5_pow[l], n_ctx // S5_T, ctx_rows // S5_T)
        z1, xs, route, pc = _mixer(z, yg, mod, p, n_ctx_tiles, min(n_ctx, TM))
        src, owner = _moe_plan(pc[:, 0, :N_EXPERTS], n_work)
        yt = _moe_experts(xs, src, owner, w1[l], w3[l], w2[l])
        last = l == depth - 1
        z = _combine(yt, route, z1, mod, g_final.reshape(1, d) if last else p["g_ffn"], last, n_ctx_tiles)

    return z[ctx_rows:].reshape(bsz, seq, d)
```

```python
import functools

import jax
import jax.numpy as jnp
from jax import lax
from jax.experimental import pallas as pl
from jax.experimental.pallas import tpu as pltpu

F32 = jnp.float32
BF16 = jnp.bfloat16
HIGHEST = lax.Precision.HIGHEST

D_MODEL = 1024
GRID_W = 64
MIX_W = D_MODEL // 2
N_BRANCH = 3
S5_GROUP = 16
S5_GROUPS = MIX_W // S5_GROUP
S5_STATE = 64
SG_CHUNK = 128
SG_GROUPS = 4
SG_GROUP_W = MIX_W // SG_GROUPS
CONV_K = 3
N_EXP_GROUPS = 4
EXPERTS_PER_GROUP = 8
N_EXPERTS = N_EXP_GROUPS * EXPERTS_PER_GROUP
D_FF_EXPERT = D_MODEL // 4
NORM_EPS = 1e-6

LANES = 128
SUBLANES = 8
S5_T = 16
S5_CW = S5_T * S5_GROUP
S5_GB = 4
CTX_ALIGN = 512
TM = 512
MOE_BLK = 16
MOE_SLOTS = 2 * TM + N_EXPERTS * MOE_BLK
MOE_TE = 512
MOE_NB = MOE_TE // MOE_BLK
VMEM_LIMIT = 56 * 1024 * 1024


def _dot(a, b):
    return jnp.dot(a, b, preferred_element_type=F32)


def _dot_hi(a, b):
    return jnp.dot(a, b, preferred_element_type=F32, precision=HIGHEST)


def _sigmoid(x):
    return 0.5 * jnp.tanh(0.5 * x) + 0.5


def _split_bf16(w):
    hi = w.astype(BF16)
    return jnp.stack([hi, (w - hi.astype(F32)).astype(BF16)])


def _rms(x, g):
    return x * lax.rsqrt(jnp.mean(x * x, axis=-1, keepdims=True) + NORM_EPS) * g


def _block_transpose(xs):
    nblk = LANES // S5_GROUP
    assert len(xs) == nblk == 8
    blk = lax.broadcasted_iota(jnp.int32, xs[0].shape, 1) // S5_GROUP
    for k in range(3):
        dist = 1 << k
        hi_half = (blk & dist) != 0
        out = list(xs)
        for a in range(nblk):
            if a & dist == 0:
                p = a + dist
                out[a] = jnp.where(hi_half, pltpu.roll(xs[p], dist * S5_GROUP, 1), xs[a])
                out[p] = jnp.where(hi_half, xs[p], pltpu.roll(xs[a], LANES - dist * S5_GROUP, 1))
        xs = out
    return xs


def _const_spec(shape):
    nd = len(shape)
    return pl.BlockSpec(shape, lambda *_: (0,) * nd, pipeline_mode=pl.Buffered(1))


def _ada_kernel(cc_ref, w_ref, b_ref, o_ref):
    cc = cc_ref[...]
    o_ref[...] = _dot_hi(cc * _sigmoid(cc), w_ref[...]) + b_ref[...]


def _ada(cc, w_ada, b_ada):
    depth = w_ada.shape[0]
    nblk = 6
    return pl.pallas_call(
        _ada_kernel,
        grid=(depth, nblk),
        in_specs=[
            pl.BlockSpec((8, D_MODEL), lambda l, j: (0, 0)),
            pl.BlockSpec((None, D_MODEL, D_MODEL), lambda l, j: (l, 0, j)),
            pl.BlockSpec((None, 1, D_MODEL), lambda l, j: (l, 0, j)),
        ],
        out_specs=pl.BlockSpec((None, 8, D_MODEL), lambda l, j: (l, 0, j)),
        out_shape=jax.ShapeDtypeStruct((depth, 8, 6 * D_MODEL), F32),
        name="ada",
    )(cc, w_ada, b_ada.reshape(depth, 1, 6 * D_MODEL))


def _mod_spec(n_ctx_tiles):
    return pl.BlockSpec((None, 1, 6 * D_MODEL), lambda i: (jnp.where(i < n_ctx_tiles, 0, 1), 0, 0))


def _u_kernel(z_ref, mod_ref, g_ref, w_ref, ug_ref, scr_ref):
    mod = mod_ref[...]
    h = _rms(z_ref[...], g_ref[...]) * (1.0 + mod[:, D_MODEL:2 * D_MODEL]) + mod[:, 0:D_MODEL]
    u = _dot(h.astype(BF16), w_ref[...])
    n_chunks = TM // S5_T
    for q in range(MIX_W // LANES):
        scr_ref[q] = u[:, q * LANES:(q + 1) * LANES]
    for q in range(MIX_W // LANES):
        for half in range(S5_T // SUBLANES):
            rows = [scr_ref[q, pl.ds(half * SUBLANES + a, n_chunks, stride=S5_T), :] for a in range(SUBLANES)]
            for b, y in enumerate(_block_transpose(rows)):
                ug_ref[q * SUBLANES + b, :, half * LANES:(half + 1) * LANES] = y.astype(BF16)


def _u_proj(z, mod, g_mix, w_u, n_ctx_tiles):
    n = z.shape[0]
    return pl.pallas_call(
        _u_kernel,
        grid=(n // TM,),
        in_specs=[
            pl.BlockSpec((TM, D_MODEL), lambda i: (i, 0)),
            _mod_spec(n_ctx_tiles),
            _const_spec((1, D_MODEL)),
            _const_spec((D_MODEL, MIX_W)),
        ],
        out_specs=pl.BlockSpec((S5_GROUPS, TM // S5_T, S5_CW), lambda i: (0, i, 0)),
        out_shape=jax.ShapeDtypeStruct((S5_GROUPS, n // S5_T, S5_CW), BF16),
        scratch_shapes=[pltpu.VMEM((MIX_W // LANES, TM, LANES), F32)],
        name="u_proj",
    )(z, mod, g_mix, w_u)


def _s5_prep_kernel(n_seg_chunks, par_ref, bt_ref, ct_ref, dsk_ref, m_ref, bp_ref, cp_ref, tab_ref, pow_ref):
    t = S5_T
    par = par_ref[...]
    lo = lax.broadcasted_iota(jnp.int32, (1, LANES), 1) < S5_STATE
    sgn = jnp.where(lo, -1.0, 1.0)

    def rows(n):
        return lax.broadcasted_iota(jnp.int32, (n, 1), 0).astype(F32)

    def expand(pw_rows, vec_rows):
        pr, pi = pw_rows
        vr, vi = vec_rows
        pr, pi = pr[:, None, :], pi[:, None, :]
        vr, vi = vr[None, :, :], vi[None, :, :]
        re = (pr * vr - pi * vi).reshape(-1, LANES)
        im = (pr * vi + pi * vr).reshape(-1, LANES)
        return re, im

    mt = None
    bp_cols, cp_rows, tab_rows, pow_rows = [], [], [], []
    for d in range(2):
        lre, lim = par[2 * d:2 * d + 1], par[2 * d + 1:2 * d + 2]
        dt = jnp.exp(par[4 + d:5 + d])
        xr, xi = lre * dt, lim * dt

        def cpow(m):
            mag = jnp.exp(m * xr)
            return mag * jnp.cos(m * xi), mag * jnp.sin(m * xi)

        ar, ai = cpow(1.0)
        den = lre * lre + lim * lim
        qr = ((ar - 1.0) * lre + ai * lim) / den
        qi = (ai * lre - (ar - 1.0) * lim) / den
        br, bi = bt_ref[2 * d], bt_ref[2 * d + 1]
        bb = (qr * br - qi * bi, qr * bi + qi * br)
        cc = (ct_ref[2 * d], ct_ref[2 * d + 1])

        inj_pow = cpow(t - 1.0 - rows(t)) if d == 0 else cpow(rows(t))
        re, im = expand(inj_pow, bb)
        bp_cols.append(jnp.where(lo, re, im))

        out_pow = cpow(rows(t) + 1.0) if d == 0 else cpow(t - rows(t))
        re, im = expand(out_pow, cc)
        cp_rows.append(jnp.where(lo, re, -im).T)

        lag_pow = cpow(rows(t)) if d == 0 else cpow(t - 1.0 - rows(t))
        re, im = expand(lag_pow, cc)
        ctab = jnp.where(lo, re, -im)
        bcat = jnp.broadcast_to(jnp.where(lo, bb[0], bb[1])[None], (t, S5_GROUP, LANES)).reshape(-1, LANES)
        k256 = lax.dot_general(ctab, bcat, (((1,), (1,)), ((), ())), precision=HIGHEST,
                               preferred_element_type=F32)
        col_blk = lax.broadcasted_iota(jnp.int32, (S5_CW, S5_CW), 1) // S5_GROUP
        acc = jnp.zeros((S5_CW, S5_CW), F32)
        for r in range(t):
            sh = r * S5_GROUP if d == 0 else (t - 1 - r) * S5_GROUP
            if sh == 0:
                shifted = k256
            elif d == 0:
                shifted = jnp.concatenate([jnp.zeros((sh, S5_CW), F32), k256[:S5_CW - sh]], axis=0)
            else:
                shifted = jnp.concatenate([k256[sh:], jnp.zeros((sh, S5_CW), F32)], axis=0)
            acc = acc + jnp.where(col_blk == r, shifted, 0.0)
        mt = acc if mt is None else mt + acc

        for m in (float(t), float(t * n_seg_chunks)):
            re, im = cpow(m)
            tab_rows += [re, sgn * im]
        kk = rows(n_seg_chunks)
        re, im = cpow(t * kk) if d == 0 else cpow(t * (n_seg_chunks - 1.0 - kk))
        pow_rows += [re, sgn * im]

    eye = (lax.broadcasted_iota(jnp.int32, (S5_CW, S5_CW), 0) == lax.broadcasted_iota(jnp.int32, (S5_CW, S5_CW), 1))
    mt = mt + jnp.where(eye, dsk_ref[...], 0.0)
    m_ref[...] = mt.T.astype(BF16)
    bp_ref[...] = jnp.concatenate(bp_cols, axis=1).astype(BF16)
    cp_ref[...] = jnp.concatenate(cp_rows, axis=0).astype(BF16)
    tab_ref[...] = jnp.concatenate(tab_rows, axis=0)
    for q in range(4):
        pow_ref[q] = pow_rows[q]


def _s5_prep(lam_re, lam_im, log_dt, b_re, b_im, c_re, c_im, d_skip, n_seg_chunks):
    depth = lam_re.shape[0]
    g, p_, h = S5_GROUPS, S5_STATE, S5_GROUP
    dup = lambda a: jnp.concatenate([a, a], axis=-1)
    dirs_last = lambda a: jnp.moveaxis(a, 1, 2)
    lam = jnp.stack([lam_re, lam_im], axis=3)
    lam = dirs_last(lam).reshape(depth, g, 4, p_)
    ldt = jnp.broadcast_to(dirs_last(log_dt[..., None]), (depth, g, 2, p_))
    par = dup(jnp.concatenate([lam, ldt, jnp.zeros((depth, g, 2, p_), F32)], axis=2))
    bt = jnp.stack([b_re, b_im], axis=3)
    bt = dup(jnp.swapaxes(dirs_last(bt), -1, -2).reshape(depth, g, 4, h, p_))
    ct = dup(dirs_last(jnp.stack([c_re, c_im], axis=3)).reshape(depth, g, 4, h, p_))
    dsk = jnp.tile(d_skip.reshape(depth, g, 1, h), (1, 1, 1, S5_T))

    blk = lambda *shape: pl.BlockSpec((None, None) + shape, lambda l, j: (l, j) + (0,) * len(shape))
    return pl.pallas_call(
        functools.partial(_s5_prep_kernel, n_seg_chunks),
        grid=(depth, g),
        in_specs=[blk(8, LANES), blk(4, h, LANES), blk(4, h, LANES), blk(1, S5_CW)],
        out_specs=[blk(S5_CW, S5_CW), blk(S5_CW, 2 * LANES), blk(2 * LANES, S5_CW), blk(8, LANES),
                   blk(4, n_seg_chunks, LANES)],
        out_shape=[
            jax.ShapeDtypeStruct((depth, g, S5_CW, S5_CW), BF16),
            jax.ShapeDtypeStruct((depth, g, S5_CW, 2 * LANES), BF16),
            jax.ShapeDtypeStruct((depth, g, 2 * LANES, S5_CW), BF16),
            jax.ShapeDtypeStruct((depth, g, 8, LANES), F32),
            jax.ShapeDtypeStruct((depth, g, 4, n_seg_chunks, LANES), F32),
        ],
        name="s5_prep",
    )(par, bt, ct, dsk)


def _cmul_step(s, s_sw, v, v_sw, re, ims):
    return re * s + ims * s_sw + v, re * s_sw - ims * s + v_sw


def _s5_kernel(n_ctx_chunks, n_lat0, ug_ref, m_ref, bp_ref, cp_ref, tab_ref, pow_ref, y_ref,
               vf_ref, vfs_ref, vb_ref, vbs_ref):
    gb, n_rows, _ = ug_ref.shape
    n_seg = (n_rows - n_lat0) // SUBLANES
    pitch = vf_ref.shape[1] // SUBLANES
    half = S5_STATE

    def put_segments(ref, g, val):
        for r in range(SUBLANES):
            ref[g, r * pitch:r * pitch + n_seg, :] = val[r * n_seg:(r + 1) * n_seg]

    tab = tab_ref[...]
    ctx_end = []
    sp_ctx = []
    for g in range(gb):
        v = _dot(ug_ref[g], bp_ref[g])
        vl = v[n_lat0:]
        put_segments(vf_ref, g, vl[:, :LANES])
        put_segments(vb_ref, g, vl[:, LANES:])
        put_segments(vfs_ref, g, pltpu.roll(vl[:, :LANES], half, 1))
        put_segments(vbs_ref, g, pltpu.roll(vl[:, LANES:], half, 1))

        vc = v[:n_ctx_chunks]
        vcs = jnp.concatenate([pltpu.roll(vc[:, :LANES], half, 1), pltpu.roll(vc[:, LANES:], half, 1)], axis=1)
        zero = jnp.zeros((1, LANES), F32)
        sf, sfs, sb, sbs = zero, zero, zero, zero
        rows_f, rows_b = [], [None] * n_ctx_chunks
        for j in range(n_ctx_chunks):
            jb = n_ctx_chunks - 1 - j
            rows_f.append(sf)
            rows_b[jb] = sb
            sf, sfs = _cmul_step(sf, sfs, vc[j:j + 1, :LANES], vcs[j:j + 1, :LANES], tab[g, 0:1], tab[g, 1:2])
            sb, sbs = _cmul_step(sb, sbs, vc[jb:jb + 1, LANES:], vcs[jb:jb + 1, LANES:], tab[g, 4:5], tab[g, 5:6])
        ctx_end.append((sf, sfs, sb, sbs))
        sp_ctx.append(jnp.concatenate([jnp.concatenate(rows_f, axis=0), jnp.concatenate(rows_b, axis=0)], axis=1))

    coef = [[jnp.broadcast_to(tab[g, q:q + 1], (SUBLANES, LANES)) for q in (0, 1, 4, 5)] for g in range(gb)]

    def step(k, carry):
        kb = n_seg - 1 - k
        out = []
        for g in range(gb):
            sf, sfs, sb, sbs = carry[g]
            are_f, aim_f, are_b, aim_b = coef[g]
            at_f = pl.ds(k, SUBLANES, stride=pitch)
            at_b = pl.ds(kb, SUBLANES, stride=pitch)
            v_f, v_fs = vf_ref[g, at_f, :], vfs_ref[g, at_f, :]
            v_b, v_bs = vb_ref[g, at_b, :], vbs_ref[g, at_b, :]
            vf_ref[g, at_f, :] = sf
            vb_ref[g, at_b, :] = sb
            sf, sfs = _cmul_step(sf, sfs, v_f, v_fs, are_f, aim_f)
            sb, sbs = _cmul_step(sb, sbs, v_b, v_bs, are_b, aim_b)
            out.append((sf, sfs, sb, sbs))
        return tuple(out)

    zero8 = jnp.zeros((SUBLANES, LANES), F32)
    ends = lax.fori_loop(0, n_seg, step, tuple((zero8, zero8, zero8, zero8) for _ in range(gb)))

    for g in range(gb):
        ef, efs, eb, ebs = ends[g]
        sf, sfs, sb, sbs = ctx_end[g]
        ent_f, ent_b = [], [None] * SUBLANES
        for r in range(SUBLANES):
            rb = SUBLANES - 1 - r
            ent_f.append((sf, sfs))
            ent_b[rb] = (sb, sbs)
            sf, sfs = _cmul_step(sf, sfs, ef[r:r + 1], efs[r:r + 1], tab[g, 2:3], tab[g, 3:4])
            sb, sbs = _cmul_step(sb, sbs, eb[rb:rb + 1], ebs[rb:rb + 1], tab[g, 6:7], tab[g, 7:8])

        segs = []
        for r in range(SUBLANES):
            seg_rows = slice(r * pitch, r * pitch + n_seg)
            f = vf_ref[g, seg_rows, :] + pow_ref[g, 0] * ent_f[r][0] + pow_ref[g, 1] * ent_f[r][1]
            b = vb_ref[g, seg_rows, :] + pow_ref[g, 2] * ent_b[r][0] + pow_ref[g, 3] * ent_b[r][1]
            segs.append(jnp.concatenate([f, b], axis=1))
        pad = jnp.zeros((n_lat0 - n_ctx_chunks, 2 * LANES), F32)
        sprev = jnp.concatenate([sp_ctx[g], pad] + segs, axis=0).astype(BF16)
        y_ref[g] = _dot(ug_ref[g], m_ref[g]) + _dot(sprev, cp_ref[g])


def _s5_apply(ug, m, bp, cp, tab, pw, n_ctx_chunks, n_lat0):
    n_groups, n_rows, cw = ug.shape
    n_seg = pw.shape[-2]
    assert n_rows - n_lat0 == SUBLANES * n_seg and n_lat0 % SUBLANES == 0
    blk = lambda *shape: pl.BlockSpec((S5_GB,) + shape, lambda i: (i,) + (0,) * len(shape))
    seg = pltpu.VMEM((S5_GB, SUBLANES * (n_seg + SUBLANES), LANES), F32)
    return pl.pallas_call(
        functools.partial(_s5_kernel, n_ctx_chunks, n_lat0),
        grid=(n_groups // S5_GB,),
        in_specs=[blk(n_rows, cw), blk(cw, cw), blk(cw, 2 * LANES), blk(2 * LANES, cw), blk(8, LANES),
                  blk(4, n_seg, LANES)],
        out_specs=blk(n_rows, cw),
        out_shape=jax.ShapeDtypeStruct((n_groups, n_rows, cw), F32),
        scratch_shapes=[seg, seg, seg, seg],
        compiler_params=pltpu.CompilerParams(vmem_limit_bytes=VMEM_LIMIT),
        name="s5_scan",
    )(ug, m, bp, cp, tab, pw)


def _mixer_kernel(n_ctx_tiles, ctx_row_len, z_ref, y_ref, mod_ref, gmix_ref, gffn_ref, win_ref, wglu_ref,
                  lng_ref, lnb_ref, wsg_ref, bsg_ref, wconv_ref, wbr_ref, wmg_ref, bmg_ref, wo_ref, wr_ref,
                  br_ref, z1_ref, xs_ref, route_ref, pc_ref, yscr_ref):
    d = D_MODEL
    n_chunks = TM // S5_T
    for q in range(MIX_W // LANES):
        for half in range(S5_T // SUBLANES):
            cols = [y_ref[q * SUBLANES + b, :, half * LANES:(half + 1) * LANES] for b in range(SUBLANES)]
            for a, rows in enumerate(_block_transpose(cols)):
                yscr_ref[q, pl.ds(half * SUBLANES + a, n_chunks, stride=S5_T), :] = rows
    y_tok = jnp.concatenate([yscr_ref[q] for q in range(MIX_W // LANES)], axis=1)
    z = z_ref[...]
    mod = mod_ref[...]
    sh1, sc1, ga1 = mod[:, 0:d], mod[:, d:2 * d], mod[:, 2 * d:3 * d]
    sh2, sc2 = mod[:, 3 * d:4 * d], mod[:, 4 * d:5 * d]
    h = _rms(z, gmix_ref[...]) * (1.0 + sc1) + sh1
    hb = h.astype(BF16)
    proj = _dot(hb, win_ref[...])
    su, sv, cgb, cgc, chh = [proj[:, k * MIX_W:(k + 1) * MIX_W] for k in range(5)]

    def merge_term(k, y_k):
        gate = _sigmoid(_dot(hb, wmg_ref[:, k * d:(k + 1) * d]) + bmg_ref[:, k * d:(k + 1) * d])
        return gate * _dot(y_k.astype(BF16), wbr_ref[k])

    yg = jax.nn.gelu(y_tok)
    merged = merge_term(0, yg * _sigmoid(_dot(yg.astype(BF16), wglu_ref[...])))

    gv = jax.nn.gelu(sv)
    mu = jnp.mean(gv, axis=-1, keepdims=True)
    var = jnp.mean(jnp.square(gv - mu), axis=-1, keepdims=True)
    vn = ((gv - mu) * lax.rsqrt(var + NORM_EPS) * lng_ref[...] + lnb_ref[...]).astype(BF16)
    bsg = bsg_ref[...]
    chunks = []
    for n in range(TM // SG_CHUNK):
        rows = slice(n * SG_CHUNK, (n + 1) * SG_CHUNK)
        cols = [_dot(wsg_ref[g], vn[rows, g * SG_GROUP_W:(g + 1) * SG_GROUP_W]) for g in range(SG_GROUPS)]
        chunks.append(jnp.concatenate(cols, axis=1) + bsg)
    merged = merged + merge_term(1, jax.nn.gelu(su) * jnp.concatenate(chunks, axis=0))

    zc = cgc * chh
    row = lax.broadcasted_iota(jnp.int32, (TM, 1), 0)
    is_ctx = pl.program_id(0) < n_ctx_tiles
    pos = jnp.where(is_ctx, row % ctx_row_len, row % GRID_W)
    prev = jnp.where(pos == 0, 0.0, pltpu.roll(zc, 1, 0))
    nxt = jnp.where(pos == jnp.where(is_ctx, ctx_row_len - 1, GRID_W - 1), 0.0, pltpu.roll(zc, TM - 1, 0))
    wc = wconv_ref[...]
    merged = merged + merge_term(2, cgb * (wc[0:1] * prev + wc[1:2] * zc + wc[2:3] * nxt))

    z1 = z + ga1 * _dot(merged.astype(BF16), wo_ref[...])
    z1_ref[...] = z1

    h2 = _rms(z1, gffn_ref[...]) * (1.0 + sc2) + sh2
    h2_hi = h2.astype(BF16)
    h2_lo = (h2 - h2_hi.astype(F32)).astype(BF16)
    logits = _dot(h2_hi, wr_ref[0]) + _dot(h2_lo, wr_ref[0]) + _dot(h2_hi, wr_ref[1]) + br_ref[...]
    lane = lax.broadcasted_iota(jnp.int32, (TM, LANES), 1)
    big = jnp.int32(LANES)
    neg = jnp.float32(-jnp.inf)
    is_grp = (lane >= N_EXPERTS) & (lane < N_EXPERTS + N_EXP_GROUPS)
    gl = jnp.where(is_grp, logits, neg)
    ge = jnp.exp(gl - jnp.max(gl, axis=-1, keepdims=True))
    gp = ge / jnp.sum(ge, axis=-1, keepdims=True)
    g_p = jnp.max(gp, axis=-1, keepdims=True)
    g_idx = jnp.min(jnp.where(is_grp & (gp == g_p), lane, big), axis=-1, keepdims=True) - N_EXPERTS
    in_grp = (lane >= g_idx * EXPERTS_PER_GROUP) & (lane < (g_idx + 1) * EXPERTS_PER_GROUP)
    el = jnp.where(in_grp, logits, neg)
    ee = jnp.exp(el - jnp.max(el, axis=-1, keepdims=True))
    ep = ee / jnp.sum(ee, axis=-1, keepdims=True)
    p1 = jnp.max(ep, axis=-1, keepdims=True)
    i1 = jnp.min(jnp.where(in_grp & (ep == p1), lane, big), axis=-1, keepdims=True)
    rest = in_grp & (lane != i1)
    ep2 = jnp.where(rest, ep, -1.0)
    p2 = jnp.max(ep2, axis=-1, keepdims=True)
    i2 = jnp.min(jnp.where(rest & (ep2 == p2), lane, big), axis=-1, keepdims=True)
    tot = p1 + p2
    w_1, w_2 = g_p * (p1 / tot), g_p * (p2 / tot)

    oh1, oh2 = lane == i1, lane == i2
    oh = jnp.where(oh1 | oh2, 1.0, 0.0)
    cnt = jnp.broadcast_to(jnp.sum(oh, axis=0, keepdims=True), (SUBLANES, LANES))
    pc = jnp.floor((cnt + (MOE_BLK - 1)) * (1.0 / MOE_BLK)) * MOE_BLK
    lane8 = lax.broadcasted_iota(jnp.int32, (SUBLANES, LANES), 1)
    incl = pc
    sh = 1
    while sh < N_EXPERTS:
        incl = incl + jnp.where(lane8 >= sh, pltpu.roll(incl, sh, 1), 0.0)
        sh *= 2
    start = (incl - pc)[0:1]
    tri = jnp.where(lax.broadcasted_iota(jnp.int32, (TM, TM), 0) > lax.broadcasted_iota(jnp.int32, (TM, TM), 1),
                    1.0, 0.0).astype(BF16)
    rank = _dot(tri, oh.astype(BF16))
    slot = start + rank
    slot1 = jnp.sum(jnp.where(oh1, slot, 0.0), axis=-1, keepdims=True)
    slot2 = jnp.sum(jnp.where(oh2, slot, 0.0), axis=-1, keepdims=True)
    route_ref[...] = jnp.where(lane == 0, slot1, jnp.where(lane == 1, slot2, jnp.where(lane == 2, w_1,
                               jnp.where(lane == 3, w_2, 0.0))))
    pc_ref[...] = pc.astype(jnp.int32)
    sl = xs_ref.shape[0]
    slot_lane = lax.broadcasted_iota(jnp.int32, (TM, sl), 1).astype(F32)
    perm_t = jnp.where((slot_lane == slot1) | (slot_lane == slot2), 1.0, 0.0).astype(BF16)
    xs_ref[...] = lax.dot_general(perm_t, h2_hi, (((0,), (0,)), ((), ())), preferred_element_type=F32).astype(BF16)


def _mixer(z, ytm, mod, p, n_ctx_tiles, ctx_row_len):
    n = z.shape[0]
    d = D_MODEL
    tile = lambda w: pl.BlockSpec((TM, w), lambda i: (i, 0))
    return pl.pallas_call(
        functools.partial(_mixer_kernel, n_ctx_tiles, ctx_row_len),
        grid=(n // TM,),
        in_specs=[
            tile(d), pl.BlockSpec((S5_GROUPS, TM // S5_T, S5_CW), lambda i: (0, i, 0)), _mod_spec(n_ctx_tiles),
            _const_spec((1, d)), _const_spec((1, d)),
            _const_spec((d, 5 * MIX_W)), _const_spec((MIX_W, MIX_W)),
            _const_spec((1, MIX_W)), _const_spec((1, MIX_W)),
            _const_spec((SG_GROUPS, SG_CHUNK, SG_CHUNK)), _const_spec((SG_CHUNK, MIX_W)),
            _const_spec((CONV_K, MIX_W)), _const_spec((N_BRANCH, MIX_W, d)),
            _const_spec((d, N_BRANCH * d)), _const_spec((1, N_BRANCH * d)),
            _const_spec((d, d)), _const_spec((2, d, LANES)), _const_spec((1, LANES)),
        ],
        out_specs=[tile(d), pl.BlockSpec((None, MOE_SLOTS, d), lambda i: (i, 0, 0)), tile(LANES),
                   pl.BlockSpec((None, SUBLANES, LANES), lambda i: (i, 0, 0))],
        out_shape=[
            jax.ShapeDtypeStruct((n, d), F32),
            jax.ShapeDtypeStruct((n // TM, MOE_SLOTS, d), BF16),
            jax.ShapeDtypeStruct((n, LANES), F32),
            jax.ShapeDtypeStruct((n // TM, SUBLANES, LANES), jnp.int32),
        ],
        scratch_shapes=[pltpu.VMEM((MIX_W // LANES, TM, LANES), F32)],
        compiler_params=pltpu.CompilerParams(vmem_limit_bytes=VMEM_LIMIT),
        name="mixer",
    )(z, ytm, mod, p["g_mix"], p["g_ffn"], p["w_in5"], p["w_glu"], p["sg_ln_g"], p["sg_ln_b"],
      p["w_sg"], p["b_sg_full"], p["w_conv"], p["w_branch"], p["w_merge"], p["b_merge"], p["w_o"],
      p["w_router"], p["b_router"])


def _plan_kernel(nt, n_work, pc_ref, src_ref, exp_ref, run_ref):
    bpt = MOE_SLOTS // MOE_BLK

    def clear(i, c):
        run_ref[i] = 0
        return c

    lax.fori_loop(0, nt, clear, 0)

    def no_block(b, c):
        src_ref[b] = -1
        return c

    def expert(e, g0):
        def tile(i, pos):
            nb = pc_ref[i, e] // MOE_BLK
            s0 = i * bpt + run_ref[i] // MOE_BLK

            def blk(b, c):
                src_ref[g0 * MOE_NB + pos + b] = s0 + b
                return c

            lax.fori_loop(0, nb, blk, 0)
            run_ref[i] = run_ref[i] + pc_ref[i, e]
            return pos + nb

        pos = lax.fori_loop(0, nt, tile, 0)
        n_tiles = (pos + MOE_NB - 1) // MOE_NB
        lax.fori_loop(g0 * MOE_NB + pos, (g0 + n_tiles) * MOE_NB, no_block, 0)

        def owner(t, c):
            exp_ref[g0 + t] = e
            return c

        lax.fori_loop(0, n_tiles, owner, 0)
        return g0 + n_tiles

    g_end = lax.fori_loop(0, N_EXPERTS, expert, 0)
    lax.fori_loop(g_end * MOE_NB, n_work * MOE_NB, no_block, 0)

    def idle(g, c):
        exp_ref[g] = N_EXPERTS - 1
        return c

    lax.fori_loop(g_end, n_work, idle, 0)


def _moe_plan(pc, n_work):
    nt = pc.shape[0]
    smem = pl.BlockSpec(memory_space=pltpu.SMEM)
    return pl.pallas_call(
        functools.partial(_plan_kernel, nt, n_work),
        in_specs=[smem],
        out_specs=[smem, smem],
        out_shape=[jax.ShapeDtypeStruct((n_work * MOE_NB,), jnp.int32), jax.ShapeDtypeStruct((n_work,), jnp.int32)],
        scratch_shapes=[pltpu.SMEM((nt,), jnp.int32)],
        name="moe_plan",
    )(pc)


def _expert_kernel(src_ref, exp_ref, xs_hbm, w1_ref, w3_ref, w2_ref, yt_hbm, lhs_ref, out_ref, wb1_ref, wb3_ref,
                   wb2_ref, sem_in, sem_out):
    g = pl.program_id(0)
    n_work = pl.num_programs(0)

    def first(step):
        return src_ref[step * MOE_NB]

    def last(step):
        return src_ref[step * MOE_NB + MOE_NB - 1]

    def gather_copy(step, b):
        return pltpu.make_async_copy(xs_hbm.at[src_ref[step * MOE_NB + b]], lhs_ref.at[step % 2, b],
                                     sem_in.at[step % 2])

    def scatter_copy(step, b):
        return pltpu.make_async_copy(out_ref.at[step % 2, b], yt_hbm.at[src_ref[step * MOE_NB + b]],
                                     sem_out.at[step % 2])

    def gather_all(step):
        return pltpu.make_async_copy(xs_hbm.at[pl.ds(0, MOE_NB)], lhs_ref.at[step % 2], sem_in.at[step % 2])

    def scatter_all(step):
        return pltpu.make_async_copy(out_ref.at[step % 2], yt_hbm.at[pl.ds(0, MOE_NB)], sem_out.at[step % 2])

    def per_block(step, fn):
        def body(b, c):
            @pl.when(src_ref[step * MOE_NB + b] >= 0)
            def _():
                fn(b)
            return c
        lax.fori_loop(0, MOE_NB, body, 0)

    def start(step, copy):
        @pl.when(last(step) >= 0)
        def _():
            for b in range(MOE_NB):
                copy(step, b).start()

        @pl.when((last(step) < 0) & (first(step) >= 0))
        def _():
            per_block(step, lambda b: copy(step, b).start())

    def wait(step, copy, copy_all):
        @pl.when(last(step) >= 0)
        def _():
            copy_all(step).wait()

        @pl.when((last(step) < 0) & (first(step) >= 0))
        def _():
            per_block(step, lambda b: copy(step, b).wait())

    @pl.when(g == 0)
    def _():
        start(g, gather_copy)

    @pl.when(g + 1 < n_work)
    def _():
        start(g + 1, gather_copy)

    wait(g, gather_copy, gather_all)

    @pl.when(g >= 2)
    def _():
        wait(g - 2, scatter_copy, scatter_all)

    slot = g % 2

    @pl.when(first(g) >= 0)
    def _():
        @pl.when((g == 0) | (exp_ref[g] != exp_ref[jnp.maximum(g - 1, 0)]))
        def _():
            wb1_ref[...] = w1_ref[...].astype(BF16)
            wb3_ref[...] = w3_ref[...].astype(BF16)
            wb2_ref[...] = w2_ref[...].astype(BF16)

        @pl.when(last(g) < 0)
        def _():
            def zero_missing(b, c):
                @pl.when(src_ref[g * MOE_NB + b] < 0)
                def _():
                    lhs_ref[slot, b] = jnp.zeros((MOE_BLK, D_MODEL), BF16)
                return c
            lax.fori_loop(0, MOE_NB, zero_missing, 0)

        x = lhs_ref[slot].reshape(MOE_TE, D_MODEL)
        a = _dot(x, wb1_ref[...])
        act = a * _sigmoid(a) * _dot(x, wb3_ref[...])
        out_ref[slot] = _dot(act.astype(BF16), wb2_ref[...]).astype(BF16).reshape(MOE_NB, MOE_BLK, D_MODEL)

    start(g, scatter_copy)

    @pl.when(g == n_work - 1)
    def _():
        @pl.when(g >= 1)
        def _():
            wait(g - 1, scatter_copy, scatter_all)
        wait(g, scatter_copy, scatter_all)


def _moe_experts(xs, src, owner, w1, w3, w2, layer):
    nt, sl, d = xs.shape
    n_work = owner.shape[0]
    xs_blocks = xs.reshape(nt * sl // MOE_BLK, MOE_BLK, d)
    any_spec = pl.BlockSpec(memory_space=pl.ANY)
    by_owner = lambda *shape: pl.BlockSpec((None, None) + shape, lambda g, src, own: (layer, own[g], 0, 0))
    buf = pltpu.VMEM((2, MOE_NB, MOE_BLK, d), BF16)
    yt = pl.pallas_call(
        _expert_kernel,
        grid_spec=pltpu.PrefetchScalarGridSpec(
            num_scalar_prefetch=2,
            grid=(n_work,),
            in_specs=[any_spec, by_owner(d, D_FF_EXPERT), by_owner(d, D_FF_EXPERT), by_owner(D_FF_EXPERT, d)],
            out_specs=any_spec,
            scratch_shapes=[
                buf, buf,
                pltpu.VMEM((d, D_FF_EXPERT), BF16), pltpu.VMEM((d, D_FF_EXPERT), BF16),
                pltpu.VMEM((D_FF_EXPERT, d), BF16),
                pltpu.SemaphoreType.DMA((2,)), pltpu.SemaphoreType.DMA((2,)),
            ],
        ),
        out_shape=jax.ShapeDtypeStruct(xs_blocks.shape, BF16),
        input_output_aliases={2: 0},
        compiler_params=pltpu.CompilerParams(dimension_semantics=("arbitrary",), vmem_limit_bytes=VMEM_LIMIT),
        name="moe_experts",
    )(src, owner, xs_blocks, w1, w3, w2)
    return yt.reshape(nt, sl, d)


def _combine_kernel(final, yt_ref, route_ref, z1_ref, mod_ref, g_ref, o_ref):
    route = route_ref[...]
    slot1, slot2, w_1, w_2 = route[:, 0:1], route[:, 1:2], route[:, 2:3], route[:, 3:4]
    yt = yt_ref[...]
    slot_lane = lax.broadcasted_iota(jnp.int32, (TM, MOE_SLOTS), 1).astype(F32)
    pick = lambda s: _dot(jnp.where(slot_lane == s, 1.0, 0.0).astype(BF16), yt)
    moe = w_1 * pick(slot1) + w_2 * pick(slot2)
    ga2 = mod_ref[...][:, 5 * D_MODEL:6 * D_MODEL]
    z2 = z1_ref[...] + ga2 * moe
    o_ref[...] = _rms(z2, g_ref[...]) if final else z2


def _combine(yt, route, z1, mod, g_final, final, n_ctx_tiles, first_tile):
    n, d = z1.shape
    tile = lambda w: pl.BlockSpec((TM, w), lambda i: (i + first_tile, 0))
    return pl.pallas_call(
        functools.partial(_combine_kernel, final),
        grid=(n // TM - first_tile,),
        in_specs=[pl.BlockSpec((None, MOE_SLOTS, d), lambda i: (i + first_tile, 0, 0)), tile(LANES), tile(d),
                  pl.BlockSpec((None, 1, 6 * d), lambda i: (jnp.where(i + first_tile < n_ctx_tiles, 0, 1), 0, 0)),
                  _const_spec((1, d))],
        out_specs=pl.BlockSpec((TM, d), lambda i: (i, 0)),
        out_shape=jax.ShapeDtypeStruct((n - first_tile * TM, d), F32),
        compiler_params=pltpu.CompilerParams(vmem_limit_bytes=VMEM_LIMIT),
        name="combine",
    )(yt, route, z1, mod, g_final)


def kernel(x, c, ctx, c_ctx, w_ada, b_ada, g_mix, g_ffn, w_in, lam_re, lam_im, log_dt, b_re, b_im, c_re, c_im, d_skip, w_glu, sg_ln_g, sg_ln_b, w_sg, b_sg, w_conv, w_branch, w_merge, b_merge, w_o, w_grp, b_grp, w_exp, b_exp, w1, w3, w2, g_final):
    depth = w_ada.shape[0]
    bsz, seq, d = x.shape
    n_ctx = ctx.shape[1]
    ctx_rows = -(-n_ctx // CTX_ALIGN) * CTX_ALIGN
    n = ctx_rows + seq
    assert bsz == 1 and d == D_MODEL
    assert ctx_rows % TM == 0 and seq % TM == 0 and (TM % n_ctx == 0 or n_ctx % TM == 0)
    assert n_ctx % S5_T == 0 and seq % (S5_T * SUBLANES * SUBLANES) == 0
    n_ctx_tiles = ctx_rows // TM
    n_seg_chunks = seq // (S5_T * SUBLANES)
    n_work = (n // TM) * MOE_SLOTS // MOE_TE + N_EXPERTS

    z = jnp.concatenate([ctx[0], jnp.zeros((ctx_rows - n_ctx, d), F32), x[0]], axis=0)
    cc = jnp.zeros((8, d), F32).at[0].set(c_ctx).at[1].set(c[0])
    mods = _ada(cc, w_ada, b_ada)[:, :2].reshape(depth, 2, 1, 6 * d)
    s5_m, s5_bp, s5_cp, s5_tab, s5_pow = _s5_prep(lam_re, lam_im, log_dt, b_re, b_im, c_re, c_im, d_skip, n_seg_chunks)

    for l in range(depth):
        pad = LANES - N_EXPERTS - N_EXP_GROUPS
        p = dict(
            g_mix=g_mix[l].reshape(1, d), g_ffn=g_ffn[l].reshape(1, d),
            w_in5=w_in[l][:, MIX_W:].astype(BF16), w_glu=w_glu[l].astype(BF16),
            sg_ln_g=sg_ln_g[l].reshape(1, MIX_W), sg_ln_b=sg_ln_b[l].reshape(1, MIX_W),
            w_sg=w_sg[l].astype(BF16),
            b_sg_full=jnp.repeat(b_sg[l].T, SG_GROUP_W, axis=1),
            w_conv=w_conv[l], w_branch=w_branch[l].astype(BF16), w_merge=w_merge[l].astype(BF16),
            b_merge=b_merge[l].reshape(1, N_BRANCH * d), w_o=w_o[l].astype(BF16),
            w_router=_split_bf16(jnp.pad(jnp.concatenate([w_exp[l], w_grp[l]], axis=1), ((0, 0), (0, pad)))),
            b_router=jnp.pad(jnp.concatenate([b_exp[l], b_grp[l]]), (0, pad)).reshape(1, LANES),
        )
        mod = mods[l]

        ug = _u_proj(z, mod, p["g_mix"], w_in[l][:, :MIX_W].astype(BF16), n_ctx_tiles)
        yg = _s5_apply(ug, s5_m[l], s5_bp[l], s5_cp[l], s5_tab[l], s5_pow[l], n_ctx // S5_T, ctx_rows // S5_T)
        z1, xs, route, pc = _mixer(z, yg, mod, p, n_ctx_tiles, min(n_ctx, TM))
        src, owner = _moe_plan(pc[:, 0, :N_EXPERTS], n_work)
        yt = _moe_experts(xs, src, owner, w1, w3, w2, l)
        last = l == depth - 1
        z = _combine(yt, route, z1, mod, g_final.reshape(1, d) if last else p["g_ffn"], last, n_ctx_tiles,
                     n_ctx_tiles if last else 0)

    return z.reshape(bsz, seq, d)
```

```python
import functools

import jax
import jax.numpy as jnp
from jax import lax
from jax.experimental import pallas as pl
from jax.experimental.pallas import tpu as pltpu

F32 = jnp.float32
BF16 = jnp.bfloat16
HIGHEST = lax.Precision.HIGHEST

D_MODEL = 1024
GRID_W = 64
MIX_W = D_MODEL // 2
N_BRANCH = 3
S5_GROUP = 16
S5_GROUPS = MIX_W // S5_GROUP
S5_STATE = 64
SG_CHUNK = 128
SG_GROUPS = 4
SG_GROUP_W = MIX_W // SG_GROUPS
CONV_K = 3
N_EXP_GROUPS = 4
EXPERTS_PER_GROUP = 8
N_EXPERTS = N_EXP_GROUPS * EXPERTS_PER_GROUP
D_FF_EXPERT = D_MODEL // 4
NORM_EPS = 1e-6

LANES = 128
SUBLANES = 8
S5_T = 16
S5_CW = S5_T * S5_GROUP
S5_GB = 4
CTX_ALIGN = 512
TM = 512
MOE_BLK = 16
MOE_SLOTS = 2 * TM + N_EXPERTS * MOE_BLK
MOE_TE = 512
MOE_NB = MOE_TE // MOE_BLK
PLAN_SLACK = 4
VMEM_LIMIT = 56 * 1024 * 1024


def _dot(a, b):
    return jnp.dot(a, b, preferred_element_type=F32)


def _dot_hi(a, b):
    return jnp.dot(a, b, preferred_element_type=F32, precision=HIGHEST)


def _sigmoid(x):
    return 0.5 * jnp.tanh(0.5 * x) + 0.5


def _split_bf16(w):
    hi = w.astype(BF16)
    return jnp.stack([hi, (w - hi.astype(F32)).astype(BF16)])


def _rms(x, g):
    return x * lax.rsqrt(jnp.mean(x * x, axis=-1, keepdims=True) + NORM_EPS) * g


def _block_transpose(xs):
    nblk = LANES // S5_GROUP
    assert len(xs) == nblk == 8
    blk = lax.broadcasted_iota(jnp.int32, xs[0].shape, 1) // S5_GROUP
    for k in range(3):
        dist = 1 << k
        hi_half = (blk & dist) != 0
        out = list(xs)
        for a in range(nblk):
            if a & dist == 0:
                p = a + dist
                out[a] = jnp.where(hi_half, pltpu.roll(xs[p], dist * S5_GROUP, 1), xs[a])
                out[p] = jnp.where(hi_half, xs[p], pltpu.roll(xs[a], LANES - dist * S5_GROUP, 1))
        xs = out
    return xs


def _const_spec(shape):
    nd = len(shape)
    return pl.BlockSpec(shape, lambda *_: (0,) * nd, pipeline_mode=pl.Buffered(1))


def _ada_kernel(cc_ref, w_ref, b_ref, o_ref):
    cc = cc_ref[...]
    o_ref[...] = _dot_hi(cc * _sigmoid(cc), w_ref[...]) + b_ref[...]


def _ada(cc, w_ada, b_ada):
    depth = w_ada.shape[0]
    nblk = 6
    return pl.pallas_call(
        _ada_kernel,
        grid=(depth, nblk),
        in_specs=[
            pl.BlockSpec((8, D_MODEL), lambda l, j: (0, 0)),
            pl.BlockSpec((None, D_MODEL, D_MODEL), lambda l, j: (l, 0, j)),
            pl.BlockSpec((None, 1, D_MODEL), lambda l, j: (l, 0, j)),
        ],
        out_specs=pl.BlockSpec((None, 8, D_MODEL), lambda l, j: (l, 0, j)),
        out_shape=jax.ShapeDtypeStruct((depth, 8, 6 * D_MODEL), F32),
        name="ada",
    )(cc, w_ada, b_ada.reshape(depth, 1, 6 * D_MODEL))


def _mod_spec(n_ctx_tiles, first_tile=0):
    return pl.BlockSpec((None, 1, 6 * D_MODEL), lambda i: (jnp.where(i + first_tile < n_ctx_tiles, 0, 1), 0, 0))


def _token_specs(n_ctx_tiles, lat_off, first_tile=0):
    lat = pl.BlockSpec((TM, D_MODEL), lambda i: (jnp.maximum(i + first_tile - lat_off, 0), 0))
    ctx = pl.BlockSpec((TM, D_MODEL), lambda i: (jnp.minimum(i + first_tile, n_ctx_tiles - 1), 0))
    return lat, ctx


def _u_kernel(n_ctx_tiles, zl_ref, zc_ref, mod_ref, g_ref, w_ref, ug_ref, scr_ref):
    mod = mod_ref[...]
    z = jnp.where(pl.program_id(0) < n_ctx_tiles, zc_ref[...], zl_ref[...])
    h = _rms(z, g_ref[...]) * (1.0 + mod[:, D_MODEL:2 * D_MODEL]) + mod[:, 0:D_MODEL]
    u = _dot(h.astype(BF16), w_ref[...])
    n_chunks = TM // S5_T
    for q in range(MIX_W // LANES):
        scr_ref[q] = u[:, q * LANES:(q + 1) * LANES]
    for q in range(MIX_W // LANES):
        for half in range(S5_T // SUBLANES):
            rows = [scr_ref[q, pl.ds(half * SUBLANES + a, n_chunks, stride=S5_T), :] for a in range(SUBLANES)]
            for b, y in enumerate(_block_transpose(rows)):
                ug_ref[q * SUBLANES + b, :, half * LANES:(half + 1) * LANES] = y.astype(BF16)


def _u_proj(z_lat, z_ctx, lat_off, n, mod, g_mix, w_u, n_ctx_tiles):
    return pl.pallas_call(
        functools.partial(_u_kernel, n_ctx_tiles),
        grid=(n // TM,),
        in_specs=[
            *_token_specs(n_ctx_tiles, lat_off),
            _mod_spec(n_ctx_tiles),
            _const_spec((1, D_MODEL)),
            _const_spec((D_MODEL, MIX_W)),
        ],
        out_specs=pl.BlockSpec((S5_GROUPS, TM // S5_T, S5_CW), lambda i: (0, i, 0)),
        out_shape=jax.ShapeDtypeStruct((S5_GROUPS, n // S5_T, S5_CW), BF16),
        scratch_shapes=[pltpu.VMEM((MIX_W // LANES, TM, LANES), F32)],
        name="u_proj",
    )(z_lat, z_ctx, mod, g_mix, w_u)


def _s5_prep_kernel(n_seg_chunks, par_ref, bt_ref, ct_ref, dsk_ref, m_ref, bp_ref, cp_ref, tab_ref, pow_ref):
    t = S5_T
    par = par_ref[...]
    lo = lax.broadcasted_iota(jnp.int32, (1, LANES), 1) < S5_STATE
    sgn = jnp.where(lo, -1.0, 1.0)

    def rows(n):
        return lax.broadcasted_iota(jnp.int32, (n, 1), 0).astype(F32)

    def expand(pw_rows, vec_rows):
        pr, pi = pw_rows
        vr, vi = vec_rows
        pr, pi = pr[:, None, :], pi[:, None, :]
        vr, vi = vr[None, :, :], vi[None, :, :]
        re = (pr * vr - pi * vi).reshape(-1, LANES)
        im = (pr * vi + pi * vr).reshape(-1, LANES)
        return re, im

    mt = None
    bp_cols, cp_rows, tab_rows, pow_rows = [], [], [], []
    for d in range(2):
        lre, lim = par[2 * d:2 * d + 1], par[2 * d + 1:2 * d + 2]
        dt = jnp.exp(par[4 + d:5 + d])
        xr, xi = lre * dt, lim * dt

        def cpow(m):
            mag = jnp.exp(m * xr)
            return mag * jnp.cos(m * xi), mag * jnp.sin(m * xi)

        ar, ai = cpow(1.0)
        den = lre * lre + lim * lim
        qr = ((ar - 1.0) * lre + ai * lim) / den
        qi = (ai * lre - (ar - 1.0) * lim) / den
        br, bi = bt_ref[2 * d], bt_ref[2 * d + 1]
        bb = (qr * br - qi * bi, qr * bi + qi * br)
        cc = (ct_ref[2 * d], ct_ref[2 * d + 1])

        inj_pow = cpow(t - 1.0 - rows(t)) if d == 0 else cpow(rows(t))
        re, im = expand(inj_pow, bb)
        bp_cols.append(jnp.where(lo, re, im))

        out_pow = cpow(rows(t) + 1.0) if d == 0 else cpow(t - rows(t))
        re, im = expand(out_pow, cc)
        cp_rows.append(jnp.where(lo, re, -im).T)

        lag_pow = cpow(rows(t)) if d == 0 else cpow(t - 1.0 - rows(t))
        re, im = expand(lag_pow, cc)
        ctab = jnp.where(lo, re, -im)
        bcat = jnp.broadcast_to(jnp.where(lo, bb[0], bb[1])[None], (t, S5_GROUP, LANES)).reshape(-1, LANES)
        k256 = lax.dot_general(ctab, bcat, (((1,), (1,)), ((), ())), precision=HIGHEST,
                               preferred_element_type=F32)
        col_blk = lax.broadcasted_iota(jnp.int32, (S5_CW, S5_CW), 1) // S5_GROUP
        acc = jnp.zeros((S5_CW, S5_CW), F32)
        for r in range(t):
            sh = r * S5_GROUP if d == 0 else (t - 1 - r) * S5_GROUP
            if sh == 0:
                shifted = k256
            elif d == 0:
                shifted = jnp.concatenate([jnp.zeros((sh, S5_CW), F32), k256[:S5_CW - sh]], axis=0)
            else:
                shifted = jnp.concatenate([k256[sh:], jnp.zeros((sh, S5_CW), F32)], axis=0)
            acc = acc + jnp.where(col_blk == r, shifted, 0.0)
        mt = acc if mt is None else mt + acc

        for m in (float(t), float(t * n_seg_chunks)):
            re, im = cpow(m)
            tab_rows += [re, sgn * im]
        kk = rows(n_seg_chunks)
        re, im = cpow(t * kk) if d == 0 else cpow(t * (n_seg_chunks - 1.0 - kk))
        pow_rows += [re, sgn * im]

    eye = (lax.broadcasted_iota(jnp.int32, (S5_CW, S5_CW), 0) == lax.broadcasted_iota(jnp.int32, (S5_CW, S5_CW), 1))
    mt = mt + jnp.where(eye, dsk_ref[...], 0.0)
    m_ref[...] = mt.T.astype(BF16)
    bp_ref[...] = jnp.concatenate(bp_cols, axis=1).astype(BF16)
    cp_ref[...] = jnp.concatenate(cp_rows, axis=0).astype(BF16)
    tab_ref[...] = jnp.concatenate(tab_rows, axis=0)
    for q in range(4):
        pow_ref[q] = pow_rows[q]


def _s5_prep(lam_re, lam_im, log_dt, b_re, b_im, c_re, c_im, d_skip, n_seg_chunks):
    depth = lam_re.shape[0]
    g, p_, h = S5_GROUPS, S5_STATE, S5_GROUP
    dup = lambda a: jnp.concatenate([a, a], axis=-1)
    dirs_last = lambda a: jnp.moveaxis(a, 1, 2)
    lam = jnp.stack([lam_re, lam_im], axis=3)
    lam = dirs_last(lam).reshape(depth, g, 4, p_)
    ldt = jnp.broadcast_to(dirs_last(log_dt[..., None]), (depth, g, 2, p_))
    par = dup(jnp.concatenate([lam, ldt, jnp.zeros((depth, g, 2, p_), F32)], axis=2))
    bt = jnp.stack([b_re, b_im], axis=3)
    bt = dup(jnp.swapaxes(dirs_last(bt), -1, -2).reshape(depth, g, 4, h, p_))
    ct = dup(dirs_last(jnp.stack([c_re, c_im], axis=3)).reshape(depth, g, 4, h, p_))
    dsk = jnp.tile(d_skip.reshape(depth, g, 1, h), (1, 1, 1, S5_T))

    blk = lambda *shape: pl.BlockSpec((None, None) + shape, lambda l, j: (l, j) + (0,) * len(shape))
    return pl.pallas_call(
        functools.partial(_s5_prep_kernel, n_seg_chunks),
        grid=(depth, g),
        in_specs=[blk(8, LANES), blk(4, h, LANES), blk(4, h, LANES), blk(1, S5_CW)],
        out_specs=[blk(S5_CW, S5_CW), blk(S5_CW, 2 * LANES), blk(2 * LANES, S5_CW), blk(8, LANES),
                   blk(4, n_seg_chunks, LANES)],
        out_shape=[
            jax.ShapeDtypeStruct((depth, g, S5_CW, S5_CW), BF16),
            jax.ShapeDtypeStruct((depth, g, S5_CW, 2 * LANES), BF16),
            jax.ShapeDtypeStruct((depth, g, 2 * LANES, S5_CW), BF16),
            jax.ShapeDtypeStruct((depth, g, 8, LANES), F32),
            jax.ShapeDtypeStruct((depth, g, 4, n_seg_chunks, LANES), F32),
        ],
        name="s5_prep",
    )(par, bt, ct, dsk)


def _cmul_step(s, s_sw, v, v_sw, re, ims):
    return re * s + ims * s_sw + v, re * s_sw - ims * s + v_sw


def _s5_kernel(n_ctx_chunks, n_lat0, ug_ref, m_ref, bp_ref, cp_ref, tab_ref, pow_ref, y_ref,
               vf_ref, vfs_ref, vb_ref, vbs_ref):
    gb, n_rows, _ = ug_ref.shape
    n_seg = (n_rows - n_lat0) // SUBLANES
    pitch = vf_ref.shape[1] // SUBLANES
    half = S5_STATE

    def put_segments(ref, g, val):
        for r in range(SUBLANES):
            ref[g, r * pitch:r * pitch + n_seg, :] = val[r * n_seg:(r + 1) * n_seg]

    tab = tab_ref[...]
    ctx_end = []
    sp_ctx = []
    for g in range(gb):
        v = _dot(ug_ref[g], bp_ref[g])
        vl = v[n_lat0:]
        put_segments(vf_ref, g, vl[:, :LANES])
        put_segments(vb_ref, g, vl[:, LANES:])
        put_segments(vfs_ref, g, pltpu.roll(vl[:, :LANES], half, 1))
        put_segments(vbs_ref, g, pltpu.roll(vl[:, LANES:], half, 1))

        vc = v[:n_ctx_chunks]
        vcs = jnp.concatenate([pltpu.roll(vc[:, :LANES], half, 1), pltpu.roll(vc[:, LANES:], half, 1)], axis=1)
        zero = jnp.zeros((1, LANES), F32)
        sf, sfs, sb, sbs = zero, zero, zero, zero
        rows_f, rows_b = [], [None] * n_ctx_chunks
        for j in range(n_ctx_chunks):
            jb = n_ctx_chunks - 1 - j
            rows_f.append(sf)
            rows_b[jb] = sb
            sf, sfs = _cmul_step(sf, sfs, vc[j:j + 1, :LANES], vcs[j:j + 1, :LANES], tab[g, 0:1], tab[g, 1:2])
            sb, sbs = _cmul_step(sb, sbs, vc[jb:jb + 1, LANES:], vcs[jb:jb + 1, LANES:], tab[g, 4:5], tab[g, 5:6])
        ctx_end.append((sf, sfs, sb, sbs))
        sp_ctx.append(jnp.concatenate([jnp.concatenate(rows_f, axis=0), jnp.concatenate(rows_b, axis=0)], axis=1))

    coef = [[jnp.broadcast_to(tab[g, q:q + 1], (SUBLANES, LANES)) for q in (0, 1, 4, 5)] for g in range(gb)]

    def step(k, carry):
        kb = n_seg - 1 - k
        out = []
        for g in range(gb):
            sf, sfs, sb, sbs = carry[g]
            are_f, aim_f, are_b, aim_b = coef[g]
            at_f = pl.ds(k, SUBLANES, stride=pitch)
            at_b = pl.ds(kb, SUBLANES, stride=pitch)
            v_f, v_fs = vf_ref[g, at_f, :], vfs_ref[g, at_f, :]
            v_b, v_bs = vb_ref[g, at_b, :], vbs_ref[g, at_b, :]
            vf_ref[g, at_f, :] = sf
            vb_ref[g, at_b, :] = sb
            sf, sfs = _cmul_step(sf, sfs, v_f, v_fs, are_f, aim_f)
            sb, sbs = _cmul_step(sb, sbs, v_b, v_bs, are_b, aim_b)
            out.append((sf, sfs, sb, sbs))
        return tuple(out)

    zero8 = jnp.zeros((SUBLANES, LANES), F32)
    ends = lax.fori_loop(0, n_seg, step, tuple((zero8, zero8, zero8, zero8) for _ in range(gb)))

    for g in range(gb):
        ef, efs, eb, ebs = ends[g]
        sf, sfs, sb, sbs = ctx_end[g]
        ent_f, ent_b = [], [None] * SUBLANES
        for r in range(SUBLANES):
            rb = SUBLANES - 1 - r
            ent_f.append((sf, sfs))
            ent_b[rb] = (sb, sbs)
            sf, sfs = _cmul_step(sf, sfs, ef[r:r + 1], efs[r:r + 1], tab[g, 2:3], tab[g, 3:4])
            sb, sbs = _cmul_step(sb, sbs, eb[rb:rb + 1], ebs[rb:rb + 1], tab[g, 6:7], tab[g, 7:8])

        segs = []
        for r in range(SUBLANES):
            seg_rows = slice(r * pitch, r * pitch + n_seg)
            f = vf_ref[g, seg_rows, :] + pow_ref[g, 0] * ent_f[r][0] + pow_ref[g, 1] * ent_f[r][1]
            b = vb_ref[g, seg_rows, :] + pow_ref[g, 2] * ent_b[r][0] + pow_ref[g, 3] * ent_b[r][1]
            segs.append(jnp.concatenate([f, b], axis=1))
        pad = jnp.zeros((n_lat0 - n_ctx_chunks, 2 * LANES), F32)
        sprev = jnp.concatenate([sp_ctx[g], pad] + segs, axis=0).astype(BF16)
        y_ref[g] = _dot(ug_ref[g], m_ref[g]) + _dot(sprev, cp_ref[g])


def _s5_apply(ug, m, bp, cp, tab, pw, n_ctx_chunks, n_lat0):
    n_groups, n_rows, cw = ug.shape
    n_seg = pw.shape[-2]
    assert n_rows - n_lat0 == SUBLANES * n_seg and n_lat0 % SUBLANES == 0
    blk = lambda *shape: pl.BlockSpec((S5_GB,) + shape, lambda i: (i,) + (0,) * len(shape))
    seg = pltpu.VMEM((S5_GB, SUBLANES * (n_seg + SUBLANES), LANES), F32)
    return pl.pallas_call(
        functools.partial(_s5_kernel, n_ctx_chunks, n_lat0),
        grid=(n_groups // S5_GB,),
        in_specs=[blk(n_rows, cw), blk(cw, cw), blk(cw, 2 * LANES), blk(2 * LANES, cw), blk(8, LANES),
                  blk(4, n_seg, LANES)],
        out_specs=blk(n_rows, cw),
        out_shape=jax.ShapeDtypeStruct((n_groups, n_rows, cw), F32),
        scratch_shapes=[seg, seg, seg, seg],
        compiler_params=pltpu.CompilerParams(vmem_limit_bytes=VMEM_LIMIT),
        name="s5_scan",
    )(ug, m, bp, cp, tab, pw)


def _mixer_kernel(n_ctx_tiles, ctx_row_len, first_tile, zl_ref, zc_ref, y_ref, mod_ref, gmix_ref, gffn_ref, win_ref, wglu_ref,
                  lng_ref, lnb_ref, wsg_ref, bsg_ref, wconv_ref, wbr_ref, wmg_ref, bmg_ref, wo_ref, wr_ref,
                  br_ref, z1_ref, xs_ref, route_ref, pc_ref, yscr_ref):
    d = D_MODEL
    n_chunks = TM // S5_T
    for q in range(MIX_W // LANES):
        for half in range(S5_T // SUBLANES):
            cols = [y_ref[q * SUBLANES + b, :, half * LANES:(half + 1) * LANES] for b in range(SUBLANES)]
            for a, rows in enumerate(_block_transpose(cols)):
                yscr_ref[q, pl.ds(half * SUBLANES + a, n_chunks, stride=S5_T), :] = rows
    y_tok = jnp.concatenate([yscr_ref[q] for q in range(MIX_W // LANES)], axis=1)
    is_ctx = pl.program_id(0) + first_tile < n_ctx_tiles
    z = jnp.where(is_ctx, zc_ref[...], zl_ref[...])
    mod = mod_ref[...]
    sh1, sc1, ga1 = mod[:, 0:d], mod[:, d:2 * d], mod[:, 2 * d:3 * d]
    sh2, sc2 = mod[:, 3 * d:4 * d], mod[:, 4 * d:5 * d]
    h = _rms(z, gmix_ref[...]) * (1.0 + sc1) + sh1
    hb = h.astype(BF16)
    proj = _dot(hb, win_ref[...])
    su, sv, cgb, cgc, chh = [proj[:, k * MIX_W:(k + 1) * MIX_W] for k in range(5)]

    def merge_term(k, y_k):
        gate = _sigmoid(_dot(hb, wmg_ref[:, k * d:(k + 1) * d]) + bmg_ref[:, k * d:(k + 1) * d])
        return gate * _dot(y_k.astype(BF16), wbr_ref[k])

    yg = jax.nn.gelu(y_tok)
    merged = merge_term(0, yg * _sigmoid(_dot(yg.astype(BF16), wglu_ref[...])))

    gv = jax.nn.gelu(sv)
    mu = jnp.mean(gv, axis=-1, keepdims=True)
    var = jnp.mean(jnp.square(gv - mu), axis=-1, keepdims=True)
    vn = ((gv - mu) * lax.rsqrt(var + NORM_EPS) * lng_ref[...] + lnb_ref[...]).astype(BF16)
    bsg = bsg_ref[...]
    chunks = []
    for n in range(TM // SG_CHUNK):
        rows = slice(n * SG_CHUNK, (n + 1) * SG_CHUNK)
        cols = [_dot(wsg_ref[g], vn[rows, g * SG_GROUP_W:(g + 1) * SG_GROUP_W]) for g in range(SG_GROUPS)]
        chunks.append(jnp.concatenate(cols, axis=1) + bsg)
    merged = merged + merge_term(1, jax.nn.gelu(su) * jnp.concatenate(chunks, axis=0))

    zc = cgc * chh
    row = lax.broadcasted_iota(jnp.int32, (TM, 1), 0)
    pos = jnp.where(is_ctx, row % ctx_row_len, row % GRID_W)
    prev = jnp.where(pos == 0, 0.0, pltpu.roll(zc, 1, 0))
    nxt = jnp.where(pos == jnp.where(is_ctx, ctx_row_len - 1, GRID_W - 1), 0.0, pltpu.roll(zc, TM - 1, 0))
    wc = wconv_ref[...]
    merged = merged + merge_term(2, cgb * (wc[0:1] * prev + wc[1:2] * zc + wc[2:3] * nxt))

    z1 = z + ga1 * _dot(merged.astype(BF16), wo_ref[...])
    z1_ref[...] = z1

    h2 = _rms(z1, gffn_ref[...]) * (1.0 + sc2) + sh2
    h2_hi = h2.astype(BF16)
    h2_lo = (h2 - h2_hi.astype(F32)).astype(BF16)
    logits = _dot(h2_hi, wr_ref[0]) + _dot(h2_lo, wr_ref[0]) + _dot(h2_hi, wr_ref[1]) + br_ref[...]
    lane = lax.broadcasted_iota(jnp.int32, (TM, LANES), 1)
    big = jnp.int32(LANES)
    neg = jnp.float32(-jnp.inf)
    is_grp = (lane >= N_EXPERTS) & (lane < N_EXPERTS + N_EXP_GROUPS)
    gl = jnp.where(is_grp, logits, neg)
    ge = jnp.exp(gl - jnp.max(gl, axis=-1, keepdims=True))
    gp = ge / jnp.sum(ge, axis=-1, keepdims=True)
    g_p = jnp.max(gp, axis=-1, keepdims=True)
    g_idx = jnp.min(jnp.where(is_grp & (gp == g_p), lane, big), axis=-1, keepdims=True) - N_EXPERTS
    in_grp = (lane >= g_idx * EXPERTS_PER_GROUP) & (lane < (g_idx + 1) * EXPERTS_PER_GROUP)
    el = jnp.where(in_grp, logits, neg)
    ee = jnp.exp(el - jnp.max(el, axis=-1, keepdims=True))
    ep = ee / jnp.sum(ee, axis=-1, keepdims=True)
    p1 = jnp.max(ep, axis=-1, keepdims=True)
    i1 = jnp.min(jnp.where(in_grp & (ep == p1), lane, big), axis=-1, keepdims=True)
    rest = in_grp & (lane != i1)
    ep2 = jnp.where(rest, ep, -1.0)
    p2 = jnp.max(ep2, axis=-1, keepdims=True)
    i2 = jnp.min(jnp.where(rest & (ep2 == p2), lane, big), axis=-1, keepdims=True)
    tot = p1 + p2
    w_1, w_2 = g_p * (p1 / tot), g_p * (p2 / tot)

    oh1, oh2 = lane == i1, lane == i2
    oh = jnp.where(oh1 | oh2, 1.0, 0.0)
    cnt = jnp.broadcast_to(jnp.sum(oh, axis=0, keepdims=True), (SUBLANES, LANES))
    pc = jnp.floor((cnt + (MOE_BLK - 1)) * (1.0 / MOE_BLK)) * MOE_BLK
    lane8 = lax.broadcasted_iota(jnp.int32, (SUBLANES, LANES), 1)
    incl = pc
    sh = 1
    while sh < N_EXPERTS:
        incl = incl + jnp.where(lane8 >= sh, pltpu.roll(incl, sh, 1), 0.0)
        sh *= 2
    start = (incl - pc)[0:1]
    tri = jnp.where(lax.broadcasted_iota(jnp.int32, (TM, TM), 0) > lax.broadcasted_iota(jnp.int32, (TM, TM), 1),
                    1.0, 0.0).astype(BF16)
    rank = _dot(tri, oh.astype(BF16))
    slot = start + rank
    slot1 = jnp.sum(jnp.where(oh1, slot, 0.0), axis=-1, keepdims=True)
    slot2 = jnp.sum(jnp.where(oh2, slot, 0.0), axis=-1, keepdims=True)
    route_ref[...] = jnp.where(lane == 0, slot1, jnp.where(lane == 1, slot2, jnp.where(lane == 2, w_1,
                               jnp.where(lane == 3, w_2, 0.0))))
    pc_ref[...] = pc.astype(jnp.int32)
    sl = xs_ref.shape[0]
    slot_lane = lax.broadcasted_iota(jnp.int32, (TM, sl), 1).astype(F32)
    perm_t = jnp.where((slot_lane == slot1) | (slot_lane == slot2), 1.0, 0.0).astype(BF16)
    xs_ref[...] = lax.dot_general(perm_t, h2_hi, (((0,), (0,)), ((), ())), preferred_element_type=F32).astype(BF16)


def _mixer(z_lat, z_ctx, lat_off, n, yg, mod, p, n_ctx_tiles, ctx_row_len, first_tile):
    d = D_MODEL
    nt = n // TM - first_tile
    tile = lambda w: pl.BlockSpec((TM, w), lambda i: (i, 0))
    return pl.pallas_call(
        functools.partial(_mixer_kernel, n_ctx_tiles, ctx_row_len, first_tile),
        grid=(nt,),
        in_specs=[
            *_token_specs(n_ctx_tiles, lat_off, first_tile),
            pl.BlockSpec((S5_GROUPS, TM // S5_T, S5_CW), lambda i: (0, i + first_tile, 0)),
            _mod_spec(n_ctx_tiles, first_tile),
            _const_spec((1, d)), _const_spec((1, d)),
            _const_spec((d, 5 * MIX_W)), _const_spec((MIX_W, MIX_W)),
            _const_spec((1, MIX_W)), _const_spec((1, MIX_W)),
            _const_spec((SG_GROUPS, SG_CHUNK, SG_CHUNK)), _const_spec((SG_CHUNK, MIX_W)),
            _const_spec((CONV_K, MIX_W)), _const_spec((N_BRANCH, MIX_W, d)),
            _const_spec((d, N_BRANCH * d)), _const_spec((1, N_BRANCH * d)),
            _const_spec((d, d)), _const_spec((2, d, LANES)), _const_spec((1, LANES)),
        ],
        out_specs=[tile(d), pl.BlockSpec((None, MOE_SLOTS, d), lambda i: (i, 0, 0)), tile(LANES),
                   pl.BlockSpec((None, SUBLANES, LANES), lambda i: (i, 0, 0))],
        out_shape=[
            jax.ShapeDtypeStruct((nt * TM, d), F32),
            jax.ShapeDtypeStruct((nt, MOE_SLOTS, d), BF16),
            jax.ShapeDtypeStruct((nt * TM, LANES), F32),
            jax.ShapeDtypeStruct((nt, SUBLANES, LANES), jnp.int32),
        ],
        scratch_shapes=[pltpu.VMEM((MIX_W // LANES, TM, LANES), F32)],
        compiler_params=pltpu.CompilerParams(vmem_limit_bytes=VMEM_LIMIT),
        name="mixer",
    )(z_lat, z_ctx, yg, mod, p["g_mix"], p["g_ffn"], p["w_in5"], p["w_glu"], p["sg_ln_g"], p["sg_ln_b"],
      p["w_sg"], p["b_sg_full"], p["w_conv"], p["w_branch"], p["w_merge"], p["b_merge"], p["w_o"],
      p["w_router"], p["b_router"])


def _plan_kernel(nt, n_work, pc_ref, src_ref, exp_ref, run_ref):
    bpt = MOE_SLOTS // MOE_BLK

    def clear(i, c):
        run_ref[i] = 0
        return c

    lax.fori_loop(0, nt, clear, 0)

    def no_block(b, c):
        src_ref[b] = -1
        return c

    def expert(e, g0):
        def tile(i, pos):
            nb = pc_ref[i, e] // MOE_BLK
            s0 = i * bpt + run_ref[i] // MOE_BLK
            base = g0 * MOE_NB + pos

            def blk(b, c):
                src_ref[base + b] = s0 + b
                return c

            for b in range(PLAN_SLACK):
                src_ref[base + b] = s0 + b
            lax.fori_loop(PLAN_SLACK, nb, blk, 0)
            run_ref[i] = run_ref[i] + pc_ref[i, e]
            return pos + nb

        pos = lax.fori_loop(0, nt, tile, 0)
        n_tiles = (pos + MOE_NB - 1) // MOE_NB
        lax.fori_loop(g0 * MOE_NB + pos, (g0 + n_tiles) * MOE_NB, no_block, 0)

        def owner(t, c):
            exp_ref[g0 + t] = e
            return c

        lax.fori_loop(0, n_tiles, owner, 0)
        return g0 + n_tiles

    g_end = lax.fori_loop(0, N_EXPERTS, expert, 0)
    lax.fori_loop(g_end * MOE_NB, n_work * MOE_NB + PLAN_SLACK, no_block, 0)

    def idle(g, c):
        exp_ref[g] = N_EXPERTS - 1
        return c

    lax.fori_loop(g_end, n_work, idle, 0)


def _moe_plan(pc, n_work):
    nt = pc.shape[0]
    smem = pl.BlockSpec(memory_space=pltpu.SMEM)
    return pl.pallas_call(
        functools.partial(_plan_kernel, nt, n_work),
        in_specs=[smem],
        out_specs=[smem, smem],
        out_shape=[jax.ShapeDtypeStruct((n_work * MOE_NB + PLAN_SLACK,), jnp.int32),
                   jax.ShapeDtypeStruct((n_work,), jnp.int32)],
        scratch_shapes=[pltpu.SMEM((nt,), jnp.int32)],
        name="moe_plan",
    )(pc)


def _expert_kernel(src_ref, exp_ref, xs_hbm, w1_ref, w3_ref, w2_ref, yt_hbm, lhs_ref, out_ref, wb1_ref, wb3_ref,
                   wb2_ref, sem_in, sem_out):
    g = pl.program_id(0)
    n_work = pl.num_programs(0)

    def first(step):
        return src_ref[step * MOE_NB]

    def last(step):
        return src_ref[step * MOE_NB + MOE_NB - 1]

    def gather_copy(step, b):
        return pltpu.make_async_copy(xs_hbm.at[src_ref[step * MOE_NB + b]], lhs_ref.at[step % 2, b],
                                     sem_in.at[step % 2])

    def scatter_copy(step, b):
        return pltpu.make_async_copy(out_ref.at[step % 2, b], yt_hbm.at[src_ref[step * MOE_NB + b]],
                                     sem_out.at[step % 2])

    def gather_all(step):
        return pltpu.make_async_copy(xs_hbm.at[pl.ds(0, MOE_NB)], lhs_ref.at[step % 2], sem_in.at[step % 2])

    def scatter_all(step):
        return pltpu.make_async_copy(out_ref.at[step % 2], yt_hbm.at[pl.ds(0, MOE_NB)], sem_out.at[step % 2])

    def per_block(step, fn):
        def body(b, c):
            @pl.when(src_ref[step * MOE_NB + b] >= 0)
            def _():
                fn(b)
            return c
        lax.fori_loop(0, MOE_NB, body, 0)

    def start(step, copy):
        @pl.when(last(step) >= 0)
        def _():
            for b in range(MOE_NB):
                copy(step, b).start()

        @pl.when((last(step) < 0) & (first(step) >= 0))
        def _():
            per_block(step, lambda b: copy(step, b).start())

    def wait(step, copy, copy_all):
        @pl.when(last(step) >= 0)
        def _():
            copy_all(step).wait()

        @pl.when((last(step) < 0) & (first(step) >= 0))
        def _():
            per_block(step, lambda b: copy(step, b).wait())

    @pl.when(g == 0)
    def _():
        start(g, gather_copy)

    @pl.when(g + 1 < n_work)
    def _():
        start(g + 1, gather_copy)

    wait(g, gather_copy, gather_all)

    @pl.when(g >= 2)
    def _():
        wait(g - 2, scatter_copy, scatter_all)

    slot = g % 2

    @pl.when(first(g) >= 0)
    def _():
        @pl.when((g == 0) | (exp_ref[g] != exp_ref[jnp.maximum(g - 1, 0)]))
        def _():
            wb1_ref[...] = w1_ref[...].astype(BF16)
            wb3_ref[...] = w3_ref[...].astype(BF16)
            wb2_ref[...] = w2_ref[...].astype(BF16)

        @pl.when(last(g) < 0)
        def _():
            def zero_missing(b, c):
                @pl.when(src_ref[g * MOE_NB + b] < 0)
                def _():
                    lhs_ref[slot, b] = jnp.zeros((MOE_BLK, D_MODEL), BF16)
                return c
            lax.fori_loop(0, MOE_NB, zero_missing, 0)

        x = lhs_ref[slot].reshape(MOE_TE, D_MODEL)
        a = _dot(x, wb1_ref[...])
        act = a * _sigmoid(a) * _dot(x, wb3_ref[...])
        out_ref[slot] = _dot(act.astype(BF16), wb2_ref[...]).astype(BF16).reshape(MOE_NB, MOE_BLK, D_MODEL)

    start(g, scatter_copy)

    @pl.when(g == n_work - 1)
    def _():
        @pl.when(g >= 1)
        def _():
            wait(g - 1, scatter_copy, scatter_all)
        wait(g, scatter_copy, scatter_all)


def _moe_experts(xs, src, owner, w1, w3, w2, layer):
    nt, sl, d = xs.shape
    n_work = owner.shape[0]
    xs_blocks = xs.reshape(nt * sl // MOE_BLK, MOE_BLK, d)
    any_spec = pl.BlockSpec(memory_space=pl.ANY)
    by_owner = lambda *shape: pl.BlockSpec((None, None) + shape, lambda g, src, own: (layer, own[g], 0, 0))
    buf = pltpu.VMEM((2, MOE_NB, MOE_BLK, d), BF16)
    yt = pl.pallas_call(
        _expert_kernel,
        grid_spec=pltpu.PrefetchScalarGridSpec(
            num_scalar_prefetch=2,
            grid=(n_work,),
            in_specs=[any_spec, by_owner(d, D_FF_EXPERT), by_owner(d, D_FF_EXPERT), by_owner(D_FF_EXPERT, d)],
            out_specs=any_spec,
            scratch_shapes=[
                buf, buf,
                pltpu.VMEM((d, D_FF_EXPERT), BF16), pltpu.VMEM((d, D_FF_EXPERT), BF16),
                pltpu.VMEM((D_FF_EXPERT, d), BF16),
                pltpu.SemaphoreType.DMA((2,)), pltpu.SemaphoreType.DMA((2,)),
            ],
        ),
        out_shape=jax.ShapeDtypeStruct(xs_blocks.shape, BF16),
        input_output_aliases={2: 0},
        compiler_params=pltpu.CompilerParams(dimension_semantics=("arbitrary",), vmem_limit_bytes=VMEM_LIMIT),
        name="moe_experts",
    )(src, owner, xs_blocks, w1, w3, w2)
    return yt.reshape(nt, sl, d)


def _combine_kernel(final, yt_ref, route_ref, z1_ref, mod_ref, g_ref, o_ref):
    route = route_ref[...]
    slot1, slot2, w_1, w_2 = route[:, 0:1], route[:, 1:2], route[:, 2:3], route[:, 3:4]
    yt = yt_ref[...]
    slot_lane = lax.broadcasted_iota(jnp.int32, (TM, MOE_SLOTS), 1).astype(F32)
    pick = lambda s: _dot(jnp.where(slot_lane == s, 1.0, 0.0).astype(BF16), yt)
    moe = w_1 * pick(slot1) + w_2 * pick(slot2)
    ga2 = mod_ref[...][:, 5 * D_MODEL:6 * D_MODEL]
    z2 = z1_ref[...] + ga2 * moe
    o_ref[...] = _rms(z2, g_ref[...]) if final else z2


def _combine(yt, route, z1, mod, g_final, final, n_ctx_tiles):
    n, d = z1.shape
    tile = lambda w: pl.BlockSpec((TM, w), lambda i: (i, 0))
    return pl.pallas_call(
        functools.partial(_combine_kernel, final),
        grid=(n // TM,),
        in_specs=[pl.BlockSpec((None, MOE_SLOTS, d), lambda i: (i, 0, 0)), tile(LANES), tile(d),
                  _mod_spec(n_ctx_tiles), _const_spec((1, d))],
        out_specs=tile(d),
        out_shape=jax.ShapeDtypeStruct((n, d), F32),
        compiler_params=pltpu.CompilerParams(vmem_limit_bytes=VMEM_LIMIT),
        name="combine",
    )(yt, route, z1, mod, g_final)


def kernel(x, c, ctx, c_ctx, w_ada, b_ada, g_mix, g_ffn, w_in, lam_re, lam_im, log_dt, b_re, b_im, c_re, c_im, d_skip, w_glu, sg_ln_g, sg_ln_b, w_sg, b_sg, w_conv, w_branch, w_merge, b_merge, w_o, w_grp, b_grp, w_exp, b_exp, w1, w3, w2, g_final):
    depth = w_ada.shape[0]
    bsz, seq, d = x.shape
    n_ctx = ctx.shape[1]
    ctx_rows = -(-n_ctx // CTX_ALIGN) * CTX_ALIGN
    n = ctx_rows + seq
    assert bsz == 1 and d == D_MODEL
    assert ctx_rows % TM == 0 and seq % TM == 0 and (TM % n_ctx == 0 or n_ctx % TM == 0)
    assert n_ctx % S5_T == 0 and seq % (S5_T * SUBLANES * SUBLANES) == 0
    n_ctx_tiles = ctx_rows // TM
    n_seg_chunks = seq // (S5_T * SUBLANES)
    nt = n // TM

    z_lat, z_ctx, lat_off = x[0], jnp.pad(ctx[0], ((0, ctx_rows - n_ctx), (0, 0))), n_ctx_tiles
    cc = jnp.zeros((8, d), F32).at[0].set(c_ctx).at[1].set(c[0])
    mods = _ada(cc, w_ada, b_ada)[:, :2].reshape(depth, 2, 1, 6 * d)
    s5_m, s5_bp, s5_cp, s5_tab, s5_pow = _s5_prep(lam_re, lam_im, log_dt, b_re, b_im, c_re, c_im, d_skip, n_seg_chunks)

    for l in range(depth):
        pad = LANES - N_EXPERTS - N_EXP_GROUPS
        p = dict(
            g_mix=g_mix[l].reshape(1, d), g_ffn=g_ffn[l].reshape(1, d),
            w_in5=w_in[l][:, MIX_W:].astype(BF16), w_glu=w_glu[l].astype(BF16),
            sg_ln_g=sg_ln_g[l].reshape(1, MIX_W), sg_ln_b=sg_ln_b[l].reshape(1, MIX_W),
            w_sg=w_sg[l].astype(BF16),
            b_sg_full=jnp.repeat(b_sg[l].T, SG_GROUP_W, axis=1),
            w_conv=w_conv[l], w_branch=w_branch[l].astype(BF16), w_merge=w_merge[l].astype(BF16),
            b_merge=b_merge[l].reshape(1, N_BRANCH * d), w_o=w_o[l].astype(BF16),
            w_router=_split_bf16(jnp.pad(jnp.concatenate([w_exp[l], w_grp[l]], axis=1), ((0, 0), (0, pad)))),
            b_router=jnp.pad(jnp.concatenate([b_exp[l], b_grp[l]]), (0, pad)).reshape(1, LANES),
        )
        mod = mods[l]

        last = l == depth - 1
        first_tile = n_ctx_tiles if last else 0
        ug = _u_proj(z_lat, z_ctx, lat_off, n, mod, p["g_mix"], w_in[l][:, :MIX_W].astype(BF16), n_ctx_tiles)
        yg = _s5_apply(ug, s5_m[l], s5_bp[l], s5_cp[l], s5_tab[l], s5_pow[l], n_ctx // S5_T, ctx_rows // S5_T)
        z1, xs, route, pc = _mixer(z_lat, z_ctx, lat_off, n, yg, mod, p, n_ctx_tiles, min(n_ctx, TM), first_tile)
        n_work = (nt - first_tile) * MOE_SLOTS // MOE_TE + N_EXPERTS
        src, owner = _moe_plan(pc[:, 0, :N_EXPERTS], n_work)
        yt = _moe_experts(xs, src, owner, w1, w3, w2, l)
        z = _combine(yt, route, z1, mod, g_final.reshape(1, d) if last else p["g_ffn"], last,
                     n_ctx_tiles - first_tile)
        z_lat, z_ctx, lat_off = z, z, 0

    return z.reshape(bsz, seq, d)
```

```python
import functools

import jax
import jax.numpy as jnp
from jax import lax
from jax.experimental import pallas as pl
from jax.experimental.pallas import tpu as pltpu

F32 = jnp.float32
BF16 = jnp.bfloat16
HIGHEST = lax.Precision.HIGHEST

D_MODEL = 1024
GRID_W = 64
MIX_W = D_MODEL // 2
N_BRANCH = 3
S5_GROUP = 16
S5_GROUPS = MIX_W // S5_GROUP
S5_STATE = 64
SG_CHUNK = 128
SG_GROUPS = 4
SG_GROUP_W = MIX_W // SG_GROUPS
CONV_K = 3
N_EXP_GROUPS = 4
EXPERTS_PER_GROUP = 8
N_EXPERTS = N_EXP_GROUPS * EXPERTS_PER_GROUP
D_FF_EXPERT = D_MODEL // 4
NORM_EPS = 1e-6

LANES = 128
SUBLANES = 8
S5_T = 16
S5_CW = S5_T * S5_GROUP
S5_GB = 4
CTX_ALIGN = 512
TM = 512
MOE_BLK = 16
MOE_SLOTS = 2 * TM + N_EXPERTS * MOE_BLK
MOE_TE = 512
MOE_NB = MOE_TE // MOE_BLK
MOE_W = D_MODEL + LANES
PLAN_SLACK = 4
PLAN_UNROLL = 3
VMEM_LIMIT = 56 * 1024 * 1024


def _dot(a, b):
    return jnp.dot(a, b, preferred_element_type=F32)


def _dot_hi(a, b):
    return jnp.dot(a, b, preferred_element_type=F32, precision=HIGHEST)


def _sigmoid(x):
    return 0.5 * jnp.tanh(0.5 * x) + 0.5


def _split_bf16(w):
    hi = w.astype(BF16)
    return jnp.stack([hi, (w - hi.astype(F32)).astype(BF16)])


def _rms(x, g):
    return x * lax.rsqrt(jnp.mean(x * x, axis=-1, keepdims=True) + NORM_EPS) * g


def _block_transpose(xs):
    nblk = LANES // S5_GROUP
    assert len(xs) == nblk == 8
    blk = lax.broadcasted_iota(jnp.int32, xs[0].shape, 1) // S5_GROUP
    for k in range(3):
        dist = 1 << k
        hi_half = (blk & dist) != 0
        out = list(xs)
        for a in range(nblk):
            if a & dist == 0:
                p = a + dist
                out[a] = jnp.where(hi_half, pltpu.roll(xs[p], dist * S5_GROUP, 1), xs[a])
                out[p] = jnp.where(hi_half, xs[p], pltpu.roll(xs[a], LANES - dist * S5_GROUP, 1))
        xs = out
    return xs


def _const_spec(shape):
    nd = len(shape)
    return pl.BlockSpec(shape, lambda *_: (0,) * nd, pipeline_mode=pl.Buffered(1))


def _ada_kernel(cc_ref, w_ref, b_ref, o_ref):
    cc = cc_ref[...]
    o_ref[...] = _dot_hi(cc * _sigmoid(cc), w_ref[...]) + b_ref[...]


def _ada(cc, w_ada, b_ada):
    depth = w_ada.shape[0]
    nblk = 6
    return pl.pallas_call(
        _ada_kernel,
        grid=(depth, nblk),
        in_specs=[
            pl.BlockSpec((8, D_MODEL), lambda l, j: (0, 0)),
            pl.BlockSpec((None, D_MODEL, D_MODEL), lambda l, j: (l, 0, j)),
            pl.BlockSpec((None, 1, D_MODEL), lambda l, j: (l, 0, j)),
        ],
        out_specs=pl.BlockSpec((None, 8, D_MODEL), lambda l, j: (l, 0, j)),
        out_shape=jax.ShapeDtypeStruct((depth, 8, 6 * D_MODEL), F32),
        name="ada",
    )(cc, w_ada, b_ada.reshape(depth, 1, 6 * D_MODEL))


def _mod_spec(n_ctx_tiles, first_tile=0):
    return pl.BlockSpec((None, 1, 6 * D_MODEL), lambda i: (jnp.where(i + first_tile < n_ctx_tiles, 0, 1), 0, 0))


def _token_specs(n_ctx_tiles, lat_off, first_tile=0):
    lat = pl.BlockSpec((TM, D_MODEL), lambda i: (jnp.maximum(i + first_tile - lat_off, 0), 0))
    ctx = pl.BlockSpec((TM, D_MODEL), lambda i: (jnp.minimum(i + first_tile, n_ctx_tiles - 1), 0))
    return lat, ctx


def _u_kernel(n_ctx_tiles, zl_ref, zc_ref, mod_ref, g_ref, w_ref, ug_ref, scr_ref):
    mod = mod_ref[...]
    z = jnp.where(pl.program_id(0) < n_ctx_tiles, zc_ref[...], zl_ref[...])
    h = _rms(z, g_ref[...]) * (1.0 + mod[:, D_MODEL:2 * D_MODEL]) + mod[:, 0:D_MODEL]
    u = _dot(h.astype(BF16), w_ref[...])
    n_chunks = TM // S5_T
    for q in range(MIX_W // LANES):
        scr_ref[q] = u[:, q * LANES:(q + 1) * LANES]
    for q in range(MIX_W // LANES):
        for half in range(S5_T // SUBLANES):
            rows = [scr_ref[q, pl.ds(half * SUBLANES + a, n_chunks, stride=S5_T), :] for a in range(SUBLANES)]
            for b, y in enumerate(_block_transpose(rows)):
                ug_ref[q * SUBLANES + b, :, half * LANES:(half + 1) * LANES] = y.astype(BF16)


def _u_proj(z_lat, z_ctx, lat_off, n, mod, g_mix, w_u, n_ctx_tiles):
    return pl.pallas_call(
        functools.partial(_u_kernel, n_ctx_tiles),
        grid=(n // TM,),
        in_specs=[
            *_token_specs(n_ctx_tiles, lat_off),
            _mod_spec(n_ctx_tiles),
            _const_spec((1, D_MODEL)),
            _const_spec((D_MODEL, MIX_W)),
        ],
        out_specs=pl.BlockSpec((S5_GROUPS, TM // S5_T, S5_CW), lambda i: (0, i, 0)),
        out_shape=jax.ShapeDtypeStruct((S5_GROUPS, n // S5_T, S5_CW), BF16),
        scratch_shapes=[pltpu.VMEM((MIX_W // LANES, TM, LANES), F32)],
        name="u_proj",
    )(z_lat, z_ctx, mod, g_mix, w_u)


def _s5_prep_kernel(n_seg_chunks, par_ref, bt_ref, ct_ref, dsk_ref, m_ref, bp_ref, cp_ref, tab_ref, pow_ref):
    t = S5_T
    par = par_ref[...]
    lo = lax.broadcasted_iota(jnp.int32, (1, LANES), 1) < S5_STATE
    sgn = jnp.where(lo, -1.0, 1.0)

    def rows(n):
        return lax.broadcasted_iota(jnp.int32, (n, 1), 0).astype(F32)

    def expand(pw_rows, vec_rows):
        pr, pi = pw_rows
        vr, vi = vec_rows
        pr, pi = pr[:, None, :], pi[:, None, :]
        vr, vi = vr[None, :, :], vi[None, :, :]
        re = (pr * vr - pi * vi).reshape(-1, LANES)
        im = (pr * vi + pi * vr).reshape(-1, LANES)
        return re, im

    mt = None
    bp_cols, cp_rows, tab_rows, pow_rows = [], [], [], []
    for d in range(2):
        lre, lim = par[2 * d:2 * d + 1], par[2 * d + 1:2 * d + 2]
        dt = jnp.exp(par[4 + d:5 + d])
        xr, xi = lre * dt, lim * dt

        def cpow(m):
            mag = jnp.exp(m * xr)
            return mag * jnp.cos(m * xi), mag * jnp.sin(m * xi)

        ar, ai = cpow(1.0)
        den = lre * lre + lim * lim
        qr = ((ar - 1.0) * lre + ai * lim) / den
        qi = (ai * lre - (ar - 1.0) * lim) / den
        br, bi = bt_ref[2 * d], bt_ref[2 * d + 1]
        bb = (qr * br - qi * bi, qr * bi + qi * br)
        cc = (ct_ref[2 * d], ct_ref[2 * d + 1])

        inj_pow = cpow(t - 1.0 - rows(t)) if d == 0 else cpow(rows(t))
        re, im = expand(inj_pow, bb)
        bp_cols.append(jnp.where(lo, re, im))

        out_pow = cpow(rows(t) + 1.0) if d == 0 else cpow(t - rows(t))
        re, im = expand(out_pow, cc)
        cp_rows.append(jnp.where(lo, re, -im).T)

        lag_pow = cpow(rows(t)) if d == 0 else cpow(t - 1.0 - rows(t))
        re, im = expand(lag_pow, cc)
        ctab = jnp.where(lo, re, -im)
        bcat = jnp.broadcast_to(jnp.where(lo, bb[0], bb[1])[None], (t, S5_GROUP, LANES)).reshape(-1, LANES)
        k256 = lax.dot_general(ctab, bcat, (((1,), (1,)), ((), ())), precision=HIGHEST,
                               preferred_element_type=F32)
        col_blk = lax.broadcasted_iota(jnp.int32, (S5_CW, S5_CW), 1) // S5_GROUP
        acc = jnp.zeros((S5_CW, S5_CW), F32)
        for r in range(t):
            sh = r * S5_GROUP if d == 0 else (t - 1 - r) * S5_GROUP
            if sh == 0:
                shifted = k256
            elif d == 0:
                shifted = jnp.concatenate([jnp.zeros((sh, S5_CW), F32), k256[:S5_CW - sh]], axis=0)
            else:
                shifted = jnp.concatenate([k256[sh:], jnp.zeros((sh, S5_CW), F32)], axis=0)
            acc = acc + jnp.where(col_blk == r, shifted, 0.0)
        mt = acc if mt is None else mt + acc

        for m in (float(t), float(t * n_seg_chunks)):
            re, im = cpow(m)
            tab_rows += [re, sgn * im]
        kk = rows(n_seg_chunks)
        re, im = cpow(t * kk) if d == 0 else cpow(t * (n_seg_chunks - 1.0 - kk))
        pow_rows += [re, sgn * im]

    eye = (lax.broadcasted_iota(jnp.int32, (S5_CW, S5_CW), 0) == lax.broadcasted_iota(jnp.int32, (S5_CW, S5_CW), 1))
    mt = mt + jnp.where(eye, dsk_ref[...], 0.0)
    m_ref[...] = mt.T.astype(BF16)
    bp_ref[...] = jnp.concatenate(bp_cols, axis=1).astype(BF16)
    cp_ref[...] = jnp.concatenate(cp_rows, axis=0).astype(BF16)
    tab_ref[...] = jnp.concatenate(tab_rows, axis=0)
    for q in range(4):
        pow_ref[q] = pow_rows[q]


def _s5_prep(lam_re, lam_im, log_dt, b_re, b_im, c_re, c_im, d_skip, n_seg_chunks):
    depth = lam_re.shape[0]
    g, p_, h = S5_GROUPS, S5_STATE, S5_GROUP
    dup = lambda a: jnp.concatenate([a, a], axis=-1)
    dirs_last = lambda a: jnp.moveaxis(a, 1, 2)
    lam = jnp.stack([lam_re, lam_im], axis=3)
    lam = dirs_last(lam).reshape(depth, g, 4, p_)
    ldt = jnp.broadcast_to(dirs_last(log_dt[..., None]), (depth, g, 2, p_))
    par = dup(jnp.concatenate([lam, ldt, jnp.zeros((depth, g, 2, p_), F32)], axis=2))
    bt = jnp.stack([b_re, b_im], axis=3)
    bt = dup(jnp.swapaxes(dirs_last(bt), -1, -2).reshape(depth, g, 4, h, p_))
    ct = dup(dirs_last(jnp.stack([c_re, c_im], axis=3)).reshape(depth, g, 4, h, p_))
    dsk = jnp.tile(d_skip.reshape(depth, g, 1, h), (1, 1, 1, S5_T))

    blk = lambda *shape: pl.BlockSpec((None, None) + shape, lambda l, j: (l, j) + (0,) * len(shape))
    return pl.pallas_call(
        functools.partial(_s5_prep_kernel, n_seg_chunks),
        grid=(depth, g),
        in_specs=[blk(8, LANES), blk(4, h, LANES), blk(4, h, LANES), blk(1, S5_CW)],
        out_specs=[blk(S5_CW, S5_CW), blk(S5_CW, 2 * LANES), blk(2 * LANES, S5_CW), blk(8, LANES),
                   blk(4, n_seg_chunks, LANES)],
        out_shape=[
            jax.ShapeDtypeStruct((depth, g, S5_CW, S5_CW), BF16),
            jax.ShapeDtypeStruct((depth, g, S5_CW, 2 * LANES), BF16),
            jax.ShapeDtypeStruct((depth, g, 2 * LANES, S5_CW), BF16),
            jax.ShapeDtypeStruct((depth, g, 8, LANES), F32),
            jax.ShapeDtypeStruct((depth, g, 4, n_seg_chunks, LANES), F32),
        ],
        name="s5_prep",
    )(par, bt, ct, dsk)


def _cmul_step(s, s_sw, v, v_sw, re, ims):
    return re * s + ims * s_sw + v, re * s_sw - ims * s + v_sw


def _s5_kernel(n_ctx_chunks, n_lat0, ug_ref, m_ref, bp_ref, cp_ref, tab_ref, pow_ref, y_ref,
               vf_ref, vfs_ref, vb_ref, vbs_ref):
    gb, n_rows, _ = ug_ref.shape
    n_seg = (n_rows - n_lat0) // SUBLANES
    pitch = vf_ref.shape[1] // SUBLANES
    half = S5_STATE

    def put_segments(ref, g, val):
        for r in range(SUBLANES):
            ref[g, r * pitch:r * pitch + n_seg, :] = val[r * n_seg:(r + 1) * n_seg]

    tab = tab_ref[...]
    ctx_end = []
    sp_ctx = []
    for g in range(gb):
        v = _dot(ug_ref[g], bp_ref[g])
        vl = v[n_lat0:]
        put_segments(vf_ref, g, vl[:, :LANES])
        put_segments(vb_ref, g, vl[:, LANES:])
        put_segments(vfs_ref, g, pltpu.roll(vl[:, :LANES], half, 1))
        put_segments(vbs_ref, g, pltpu.roll(vl[:, LANES:], half, 1))

        vc = v[:n_ctx_chunks]
        vcs = jnp.concatenate([pltpu.roll(vc[:, :LANES], half, 1), pltpu.roll(vc[:, LANES:], half, 1)], axis=1)
        zero = jnp.zeros((1, LANES), F32)
        sf, sfs, sb, sbs = zero, zero, zero, zero
        rows_f, rows_b = [], [None] * n_ctx_chunks
        for j in range(n_ctx_chunks):
            jb = n_ctx_chunks - 1 - j
            rows_f.append(sf)
            rows_b[jb] = sb
            sf, sfs = _cmul_step(sf, sfs, vc[j:j + 1, :LANES], vcs[j:j + 1, :LANES], tab[g, 0:1], tab[g, 1:2])
            sb, sbs = _cmul_step(sb, sbs, vc[jb:jb + 1, LANES:], vcs[jb:jb + 1, LANES:], tab[g, 4:5], tab[g, 5:6])
        ctx_end.append((sf, sfs, sb, sbs))
        sp_ctx.append(jnp.concatenate([jnp.concatenate(rows_f, axis=0), jnp.concatenate(rows_b, axis=0)], axis=1))

    coef = [[jnp.broadcast_to(tab[g, q:q + 1], (SUBLANES, LANES)) for q in (0, 1, 4, 5)] for g in range(gb)]

    def step(k, carry):
        kb = n_seg - 1 - k
        out = []
        for g in range(gb):
            sf, sfs, sb, sbs = carry[g]
            are_f, aim_f, are_b, aim_b = coef[g]
            at_f = pl.ds(k, SUBLANES, stride=pitch)
            at_b = pl.ds(kb, SUBLANES, stride=pitch)
            v_f, v_fs = vf_ref[g, at_f, :], vfs_ref[g, at_f, :]
            v_b, v_bs = vb_ref[g, at_b, :], vbs_ref[g, at_b, :]
            vf_ref[g, at_f, :] = sf
            vb_ref[g, at_b, :] = sb
            sf, sfs = _cmul_step(sf, sfs, v_f, v_fs, are_f, aim_f)
            sb, sbs = _cmul_step(sb, sbs, v_b, v_bs, are_b, aim_b)
            out.append((sf, sfs, sb, sbs))
        return tuple(out)

    zero8 = jnp.zeros((SUBLANES, LANES), F32)
    ends = lax.fori_loop(0, n_seg, step, tuple((zero8, zero8, zero8, zero8) for _ in range(gb)))

    for g in range(gb):
        ef, efs, eb, ebs = ends[g]
        sf, sfs, sb, sbs = ctx_end[g]
        ent_f, ent_b = [], [None] * SUBLANES
        for r in range(SUBLANES):
            rb = SUBLANES - 1 - r
            ent_f.append((sf, sfs))
            ent_b[rb] = (sb, sbs)
            sf, sfs = _cmul_step(sf, sfs, ef[r:r + 1], efs[r:r + 1], tab[g, 2:3], tab[g, 3:4])
            sb, sbs = _cmul_step(sb, sbs, eb[rb:rb + 1], ebs[rb:rb + 1], tab[g, 6:7], tab[g, 7:8])

        segs = []
        for r in range(SUBLANES):
            seg_rows = slice(r * pitch, r * pitch + n_seg)
            f = vf_ref[g, seg_rows, :] + pow_ref[g, 0] * ent_f[r][0] + pow_ref[g, 1] * ent_f[r][1]
            b = vb_ref[g, seg_rows, :] + pow_ref[g, 2] * ent_b[r][0] + pow_ref[g, 3] * ent_b[r][1]
            segs.append(jnp.concatenate([f, b], axis=1))
        pad = jnp.zeros((n_lat0 - n_ctx_chunks, 2 * LANES), F32)
        sprev = jnp.concatenate([sp_ctx[g], pad] + segs, axis=0).astype(BF16)
        y_ref[g] = _dot(ug_ref[g], m_ref[g]) + _dot(sprev, cp_ref[g])


def _s5_apply(ug, m, bp, cp, tab, pw, n_ctx_chunks, n_lat0):
    n_groups, n_rows, cw = ug.shape
    n_seg = pw.shape[-2]
    assert n_rows - n_lat0 == SUBLANES * n_seg and n_lat0 % SUBLANES == 0
    blk = lambda *shape: pl.BlockSpec((S5_GB,) + shape, lambda i: (i,) + (0,) * len(shape))
    seg = pltpu.VMEM((S5_GB, SUBLANES * (n_seg + SUBLANES), LANES), F32)
    return pl.pallas_call(
        functools.partial(_s5_kernel, n_ctx_chunks, n_lat0),
        grid=(n_groups // S5_GB,),
        in_specs=[blk(n_rows, cw), blk(cw, cw), blk(cw, 2 * LANES), blk(2 * LANES, cw), blk(8, LANES),
                  blk(4, n_seg, LANES)],
        out_specs=blk(n_rows, cw),
        out_shape=jax.ShapeDtypeStruct((n_groups, n_rows, cw), F32),
        scratch_shapes=[seg, seg, seg, seg],
        compiler_params=pltpu.CompilerParams(vmem_limit_bytes=VMEM_LIMIT),
        name="s5_scan",
    )(ug, m, bp, cp, tab, pw)


def _mixer_kernel(n_ctx_tiles, ctx_row_len, first_tile, zl_ref, zc_ref, y_ref, mod_ref, gmix_ref, gffn_ref, win_ref, wglu_ref,
                  lng_ref, lnb_ref, wsg_ref, bsg_ref, wconv_ref, wbr_ref, wmg_ref, bmg_ref, wo_ref, wr_ref,
                  br_ref, z1_ref, xs_ref, route_ref, pc_ref, yscr_ref):
    d = D_MODEL
    n_chunks = TM // S5_T
    for q in range(MIX_W // LANES):
        for half in range(S5_T // SUBLANES):
            cols = [y_ref[q * SUBLANES + b, :, half * LANES:(half + 1) * LANES] for b in range(SUBLANES)]
            for a, rows in enumerate(_block_transpose(cols)):
                yscr_ref[q, pl.ds(half * SUBLANES + a, n_chunks, stride=S5_T), :] = rows
    y_tok = jnp.concatenate([yscr_ref[q] for q in range(MIX_W // LANES)], axis=1)
    is_ctx = pl.program_id(0) + first_tile < n_ctx_tiles
    z = jnp.where(is_ctx, zc_ref[...], zl_ref[...])
    mod = mod_ref[...]
    sh1, sc1, ga1 = mod[:, 0:d], mod[:, d:2 * d], mod[:, 2 * d:3 * d]
    sh2, sc2 = mod[:, 3 * d:4 * d], mod[:, 4 * d:5 * d]
    h = _rms(z, gmix_ref[...]) * (1.0 + sc1) + sh1
    hb = h.astype(BF16)
    proj = _dot(hb, win_ref[...])
    su, sv, cgb, cgc, chh = [proj[:, k * MIX_W:(k + 1) * MIX_W] for k in range(5)]

    def merge_term(k, y_k):
        gate = _sigmoid(_dot(hb, wmg_ref[:, k * d:(k + 1) * d]) + bmg_ref[:, k * d:(k + 1) * d])
        return gate * _dot(y_k.astype(BF16), wbr_ref[k])

    yg = jax.nn.gelu(y_tok)
    merged = merge_term(0, yg * _sigmoid(_dot(yg.astype(BF16), wglu_ref[...])))

    gv = jax.nn.gelu(sv)
    mu = jnp.mean(gv, axis=-1, keepdims=True)
    var = jnp.mean(jnp.square(gv - mu), axis=-1, keepdims=True)
    vn = ((gv - mu) * lax.rsqrt(var + NORM_EPS) * lng_ref[...] + lnb_ref[...]).astype(BF16)
    bsg = bsg_ref[...]
    chunks = []
    for n in range(TM // SG_CHUNK):
        rows = slice(n * SG_CHUNK, (n + 1) * SG_CHUNK)
        cols = [_dot(wsg_ref[g], vn[rows, g * SG_GROUP_W:(g + 1) * SG_GROUP_W]) for g in range(SG_GROUPS)]
        chunks.append(jnp.concatenate(cols, axis=1) + bsg)
    merged = merged + merge_term(1, jax.nn.gelu(su) * jnp.concatenate(chunks, axis=0))

    zc = cgc * chh
    row = lax.broadcasted_iota(jnp.int32, (TM, 1), 0)
    pos = jnp.where(is_ctx, row % ctx_row_len, row % GRID_W)
    prev = jnp.where(pos == 0, 0.0, pltpu.roll(zc, 1, 0))
    nxt = jnp.where(pos == jnp.where(is_ctx, ctx_row_len - 1, GRID_W - 1), 0.0, pltpu.roll(zc, TM - 1, 0))
    wc = wconv_ref[...]
    merged = merged + merge_term(2, cgb * (wc[0:1] * prev + wc[1:2] * zc + wc[2:3] * nxt))

    z1 = z + ga1 * _dot(merged.astype(BF16), wo_ref[...])
    z1_ref[...] = z1

    h2 = _rms(z1, gffn_ref[...]) * (1.0 + sc2) + sh2
    h2_hi = h2.astype(BF16)
    h2_lo = (h2 - h2_hi.astype(F32)).astype(BF16)
    logits = _dot(h2_hi, wr_ref[0]) + _dot(h2_lo, wr_ref[0]) + _dot(h2_hi, wr_ref[1]) + br_ref[...]
    lane = lax.broadcasted_iota(jnp.int32, (TM, LANES), 1)
    big = jnp.int32(LANES)
    neg = jnp.float32(-jnp.inf)
    is_grp = (lane >= N_EXPERTS) & (lane < N_EXPERTS + N_EXP_GROUPS)
    gl = jnp.where(is_grp, logits, neg)
    ge = jnp.exp(gl - jnp.max(gl, axis=-1, keepdims=True))
    gp = ge / jnp.sum(ge, axis=-1, keepdims=True)
    g_p = jnp.max(gp, axis=-1, keepdims=True)
    g_idx = jnp.min(jnp.where(is_grp & (gp == g_p), lane, big), axis=-1, keepdims=True) - N_EXPERTS
    in_grp = (lane >= g_idx * EXPERTS_PER_GROUP) & (lane < (g_idx + 1) * EXPERTS_PER_GROUP)
    el = jnp.where(in_grp, logits, neg)
    ee = jnp.exp(el - jnp.max(el, axis=-1, keepdims=True))
    ep = ee / jnp.sum(ee, axis=-1, keepdims=True)
    p1 = jnp.max(ep, axis=-1, keepdims=True)
    i1 = jnp.min(jnp.where(in_grp & (ep == p1), lane, big), axis=-1, keepdims=True)
    rest = in_grp & (lane != i1)
    ep2 = jnp.where(rest, ep, -1.0)
    p2 = jnp.max(ep2, axis=-1, keepdims=True)
    i2 = jnp.min(jnp.where(rest & (ep2 == p2), lane, big), axis=-1, keepdims=True)
    tot = p1 + p2
    w_1, w_2 = g_p * (p1 / tot), g_p * (p2 / tot)

    oh1, oh2 = lane == i1, lane == i2
    oh = jnp.where(oh1 | oh2, 1.0, 0.0)
    cnt = jnp.broadcast_to(jnp.sum(oh, axis=0, keepdims=True), (SUBLANES, LANES))
    pc = jnp.floor((cnt + (MOE_BLK - 1)) * (1.0 / MOE_BLK)) * MOE_BLK
    lane8 = lax.broadcasted_iota(jnp.int32, (SUBLANES, LANES), 1)
    incl = pc
    sh = 1
    while sh < N_EXPERTS:
        incl = incl + jnp.where(lane8 >= sh, pltpu.roll(incl, sh, 1), 0.0)
        sh *= 2
    start = (incl - pc)[0:1]
    tri = jnp.where(lax.broadcasted_iota(jnp.int32, (TM, TM), 0) > lax.broadcasted_iota(jnp.int32, (TM, TM), 1),
                    1.0, 0.0).astype(BF16)
    rank = _dot(tri, oh.astype(BF16))
    slot = start + rank
    slot1 = jnp.sum(jnp.where(oh1, slot, 0.0), axis=-1, keepdims=True)
    slot2 = jnp.sum(jnp.where(oh2, slot, 0.0), axis=-1, keepdims=True)
    route_ref[...] = jnp.where(lane == 0, slot1, jnp.where(lane == 1, slot2, jnp.where(lane == 2, w_1,
                               jnp.where(lane == 3, w_2, 0.0))))
    pc_ref[...] = pc.astype(jnp.int32)
    sl = xs_ref.shape[0]
    slot_lane = lax.broadcasted_iota(jnp.int32, (TM, sl), 1).astype(F32)
    perm1_t = jnp.where(slot_lane == slot1, 1.0, 0.0).astype(BF16)
    perm2_t = jnp.where(slot_lane == slot2, 1.0, 0.0).astype(BF16)
    to_slots = lambda perm, rows: lax.dot_general(perm, rows, (((0,), (0,)), ((), ())), preferred_element_type=F32)
    xs_ref[:, 0:d] = to_slots(perm1_t + perm2_t, h2_hi).astype(BF16)

    def weight_lanes(w):
        hi = w.astype(BF16).astype(F32)
        return jnp.where(lane == 0, hi, jnp.where(lane == 1, w - hi, 0.0)).astype(BF16)

    xs_ref[:, d:] = (to_slots(perm1_t, weight_lanes(w_1)) + to_slots(perm2_t, weight_lanes(w_2))).astype(BF16)


def _mixer(z_lat, z_ctx, lat_off, n, yg, mod, p, n_ctx_tiles, ctx_row_len, first_tile):
    d = D_MODEL
    nt = n // TM - first_tile
    tile = lambda w: pl.BlockSpec((TM, w), lambda i: (i, 0))
    return pl.pallas_call(
        functools.partial(_mixer_kernel, n_ctx_tiles, ctx_row_len, first_tile),
        grid=(nt,),
        in_specs=[
            *_token_specs(n_ctx_tiles, lat_off, first_tile),
            pl.BlockSpec((S5_GROUPS, TM // S5_T, S5_CW), lambda i: (0, i + first_tile, 0)),
            _mod_spec(n_ctx_tiles, first_tile),
            _const_spec((1, d)), _const_spec((1, d)),
            _const_spec((d, 5 * MIX_W)), _const_spec((MIX_W, MIX_W)),
            _const_spec((1, MIX_W)), _const_spec((1, MIX_W)),
            _const_spec((SG_GROUPS, SG_CHUNK, SG_CHUNK)), _const_spec((SG_CHUNK, MIX_W)),
            _const_spec((CONV_K, MIX_W)), _const_spec((N_BRANCH, MIX_W, d)),
            _const_spec((d, N_BRANCH * d)), _const_spec((1, N_BRANCH * d)),
            _const_spec((d, d)), _const_spec((2, d, LANES)), _const_spec((1, LANES)),
        ],
        out_specs=[tile(d), pl.BlockSpec((None, MOE_SLOTS, MOE_W), lambda i: (i, 0, 0)), tile(LANES),
                   pl.BlockSpec((None, SUBLANES, LANES), lambda i: (i, 0, 0))],
        out_shape=[
            jax.ShapeDtypeStruct((nt * TM, d), F32),
            jax.ShapeDtypeStruct((nt, MOE_SLOTS, MOE_W), BF16),
            jax.ShapeDtypeStruct((nt * TM, LANES), F32),
            jax.ShapeDtypeStruct((nt, SUBLANES, LANES), jnp.int32),
        ],
        scratch_shapes=[pltpu.VMEM((MIX_W // LANES, TM, LANES), F32)],
        compiler_params=pltpu.CompilerParams(vmem_limit_bytes=VMEM_LIMIT),
        name="mixer",
    )(z_lat, z_ctx, yg, mod, p["g_mix"], p["g_ffn"], p["w_in5"], p["w_glu"], p["sg_ln_g"], p["sg_ln_b"],
      p["w_sg"], p["b_sg_full"], p["w_conv"], p["w_branch"], p["w_merge"], p["b_merge"], p["w_o"],
      p["w_router"], p["b_router"])


def _plan_kernel(nt, n_work, pc_ref, src_ref, exp_ref, run_ref):
    bpt = MOE_SLOTS // MOE_BLK

    def clear(i, c):
        run_ref[i] = 0
        return c

    lax.fori_loop(0, nt, clear, 0)

    def no_block(b, c):
        src_ref[b] = -1
        return c

    def expert(e, g0):
        def tile(i, pos):
            nb = pc_ref[i, e] // MOE_BLK
            s0 = i * bpt + run_ref[i] // MOE_BLK
            base = g0 * MOE_NB + pos

            def blk(b, c):
                src_ref[base + b] = s0 + b
                return c

            for b in range(PLAN_SLACK):
                src_ref[base + b] = s0 + b
            lax.fori_loop(PLAN_SLACK, nb, blk, 0)
            run_ref[i] = run_ref[i] + pc_ref[i, e]
            return pos + nb

        def tiles(j, pos):
            for k in range(PLAN_UNROLL):
                pos = tile(j * PLAN_UNROLL + k, pos)
            return pos

        pos = lax.fori_loop(0, nt // PLAN_UNROLL, tiles, 0)
        for i in range(nt - nt % PLAN_UNROLL, nt):
            pos = tile(i, pos)
        n_tiles = (pos + MOE_NB - 1) // MOE_NB
        lax.fori_loop(g0 * MOE_NB + pos, (g0 + n_tiles) * MOE_NB, no_block, 0)

        def owner(t, c):
            exp_ref[g0 + t] = e
            return c

        lax.fori_loop(0, n_tiles, owner, 0)
        return g0 + n_tiles

    g_end = lax.fori_loop(0, N_EXPERTS, expert, 0)
    lax.fori_loop(g_end * MOE_NB, n_work * MOE_NB + PLAN_SLACK, no_block, 0)

    def idle(g, c):
        exp_ref[g] = N_EXPERTS - 1
        return c

    lax.fori_loop(g_end, n_work, idle, 0)


def _moe_plan(pc, n_work):
    nt = pc.shape[0]
    smem = pl.BlockSpec(memory_space=pltpu.SMEM)
    return pl.pallas_call(
        functools.partial(_plan_kernel, nt, n_work),
        in_specs=[smem],
        out_specs=[smem, smem],
        out_shape=[jax.ShapeDtypeStruct((n_work * MOE_NB + PLAN_SLACK,), jnp.int32),
                   jax.ShapeDtypeStruct((n_work,), jnp.int32)],
        scratch_shapes=[pltpu.SMEM((nt,), jnp.int32)],
        name="moe_plan",
    )(pc)


def _expert_kernel(src_ref, exp_ref, xs_hbm, w1_ref, w3_ref, w2_ref, yt_hbm, lhs_ref, out_ref, wb1_ref, wb3_ref,
                   wb2_ref, sem_in, sem_out):
    g = pl.program_id(0)
    n_work = pl.num_programs(0)

    def first(step):
        return src_ref[step * MOE_NB]

    def last(step):
        return src_ref[step * MOE_NB + MOE_NB - 1]

    def gather_copy(step, b):
        return pltpu.make_async_copy(xs_hbm.at[src_ref[step * MOE_NB + b]], lhs_ref.at[step % 2, b],
                                     sem_in.at[step % 2])

    def scatter_copy(step, b):
        return pltpu.make_async_copy(out_ref.at[step % 2, b], yt_hbm.at[src_ref[step * MOE_NB + b]],
                                     sem_out.at[step % 2])

    def gather_all(step):
        return pltpu.make_async_copy(xs_hbm.at[pl.ds(0, MOE_NB)], lhs_ref.at[step % 2], sem_in.at[step % 2])

    def scatter_all(step):
        return pltpu.make_async_copy(out_ref.at[step % 2], yt_hbm.at[pl.ds(0, MOE_NB)], sem_out.at[step % 2])

    def per_block(step, fn):
        def body(b, c):
            @pl.when(src_ref[step * MOE_NB + b] >= 0)
            def _():
                fn(b)
            return c
        lax.fori_loop(0, MOE_NB, body, 0)

    def start(step, copy):
        @pl.when(last(step) >= 0)
        def _():
            for b in range(MOE_NB):
                copy(step, b).start()

        @pl.when((last(step) < 0) & (first(step) >= 0))
        def _():
            per_block(step, lambda b: copy(step, b).start())

    def wait(step, copy, copy_all):
        @pl.when(last(step) >= 0)
        def _():
            copy_all(step).wait()

        @pl.when((last(step) < 0) & (first(step) >= 0))
        def _():
            per_block(step, lambda b: copy(step, b).wait())

    @pl.when(g == 0)
    def _():
        start(g, gather_copy)

    @pl.when(g + 1 < n_work)
    def _():
        start(g + 1, gather_copy)

    wait(g, gather_copy, gather_all)

    @pl.when(g >= 2)
    def _():
        wait(g - 2, scatter_copy, scatter_all)

    slot = g % 2

    @pl.when(first(g) >= 0)
    def _():
        @pl.when((g == 0) | (exp_ref[g] != exp_ref[jnp.maximum(g - 1, 0)]))
        def _():
            wb1_ref[...] = w1_ref[...].astype(BF16)
            wb3_ref[...] = w3_ref[...].astype(BF16)
            wb2_ref[...] = w2_ref[...].astype(BF16)

        @pl.when(last(g) < 0)
        def _():
            def zero_missing(b, c):
                @pl.when(src_ref[g * MOE_NB + b] < 0)
                def _():
                    lhs_ref[slot, b] = jnp.zeros((MOE_BLK, MOE_W), BF16)
                return c
            lax.fori_loop(0, MOE_NB, zero_missing, 0)

        rows = lhs_ref[slot].reshape(MOE_TE, MOE_W)
        x = rows[:, :D_MODEL]
        w_lanes = rows[:, D_MODEL:].astype(F32)
        w_row = w_lanes[:, 0:1] + w_lanes[:, 1:2]
        a = _dot(x, wb1_ref[...])
        act = a * _sigmoid(a) * _dot(x, wb3_ref[...])
        y = (_dot(act.astype(BF16), wb2_ref[...]) * w_row).astype(BF16)
        out_ref[slot] = jnp.concatenate([y, jnp.zeros((MOE_TE, LANES), BF16)], axis=1).reshape(MOE_NB, MOE_BLK, MOE_W)

    start(g, scatter_copy)

    @pl.when(g == n_work - 1)
    def _():
        @pl.when(g >= 1)
        def _():
            wait(g - 1, scatter_copy, scatter_all)
        wait(g, scatter_copy, scatter_all)


def _moe_experts(xs, src, owner, w1, w3, w2, layer):
    nt, sl, width = xs.shape
    d = D_MODEL
    n_work = owner.shape[0]
    xs_blocks = xs.reshape(nt * sl // MOE_BLK, MOE_BLK, width)
    any_spec = pl.BlockSpec(memory_space=pl.ANY)
    by_owner = lambda *shape: pl.BlockSpec((None, None) + shape, lambda g, src, own: (layer, own[g], 0, 0))
    buf = pltpu.VMEM((2, MOE_NB, MOE_BLK, width), BF16)
    yt = pl.pallas_call(
        _expert_kernel,
        grid_spec=pltpu.PrefetchScalarGridSpec(
            num_scalar_prefetch=2,
            grid=(n_work,),
            in_specs=[any_spec, by_owner(d, D_FF_EXPERT), by_owner(d, D_FF_EXPERT), by_owner(D_FF_EXPERT, d)],
            out_specs=any_spec,
            scratch_shapes=[
                buf, buf,
                pltpu.VMEM((d, D_FF_EXPERT), BF16), pltpu.VMEM((d, D_FF_EXPERT), BF16),
                pltpu.VMEM((D_FF_EXPERT, d), BF16),
                pltpu.SemaphoreType.DMA((2,)), pltpu.SemaphoreType.DMA((2,)),
            ],
        ),
        out_shape=jax.ShapeDtypeStruct(xs_blocks.shape, BF16),
        input_output_aliases={2: 0},
        compiler_params=pltpu.CompilerParams(dimension_semantics=("arbitrary",), vmem_limit_bytes=VMEM_LIMIT),
        name="moe_experts",
    )(src, owner, xs_blocks, w1, w3, w2)
    return yt.reshape(nt, sl, width)


def _combine_kernel(final, yt_ref, route_ref, z1_ref, mod_ref, g_ref, o_ref):
    route = route_ref[...]
    slot1, slot2 = route[:, 0:1], route[:, 1:2]
    slot_lane = lax.broadcasted_iota(jnp.int32, (TM, MOE_SLOTS), 1).astype(F32)
    both = jnp.where((slot_lane == slot1) | (slot_lane == slot2), 1.0, 0.0).astype(BF16)
    moe = _dot(both, yt_ref[:, 0:D_MODEL])
    ga2 = mod_ref[...][:, 5 * D_MODEL:6 * D_MODEL]
    z2 = z1_ref[...] + ga2 * moe
    o_ref[...] = _rms(z2, g_ref[...]) if final else z2


def _combine(yt, route, z1, mod, g_final, final, n_ctx_tiles):
    n, d = z1.shape
    tile = lambda w: pl.BlockSpec((TM, w), lambda i: (i, 0))
    return pl.pallas_call(
        functools.partial(_combine_kernel, final),
        grid=(n // TM,),
        in_specs=[pl.BlockSpec((None, MOE_SLOTS, MOE_W), lambda i: (i, 0, 0)), tile(LANES), tile(d),
                  _mod_spec(n_ctx_tiles), _const_spec((1, d))],
        out_specs=tile(d),
        out_shape=jax.ShapeDtypeStruct((n, d), F32),
        compiler_params=pltpu.CompilerParams(vmem_limit_bytes=VMEM_LIMIT),
        name="combine",
    )(yt, route, z1, mod, g_final)


def kernel(x, c, ctx, c_ctx, w_ada, b_ada, g_mix, g_ffn, w_in, lam_re, lam_im, log_dt, b_re, b_im, c_re, c_im, d_skip, w_glu, sg_ln_g, sg_ln_b, w_sg, b_sg, w_conv, w_branch, w_merge, b_merge, w_o, w_grp, b_grp, w_exp, b_exp, w1, w3, w2, g_final):
    depth = w_ada.shape[0]
    bsz, seq, d = x.shape
    n_ctx = ctx.shape[1]
    ctx_rows = -(-n_ctx // CTX_ALIGN) * CTX_ALIGN
    n = ctx_rows + seq
    assert bsz == 1 and d == D_MODEL
    assert ctx_rows % TM == 0 and seq % TM == 0 and (TM % n_ctx == 0 or n_ctx % TM == 0)
    assert n_ctx % S5_T == 0 and seq % (S5_T * SUBLANES * SUBLANES) == 0
    n_ctx_tiles = ctx_rows // TM
    n_seg_chunks = seq // (S5_T * SUBLANES)
    nt = n // TM

    z_lat, z_ctx, lat_off = x[0], jnp.pad(ctx[0], ((0, ctx_rows - n_ctx), (0, 0))), n_ctx_tiles
    cc = jnp.zeros((8, d), F32).at[0].set(c_ctx).at[1].set(c[0])
    mods = _ada(cc, w_ada, b_ada)[:, :2].reshape(depth, 2, 1, 6 * d)
    s5_m, s5_bp, s5_cp, s5_tab, s5_pow = _s5_prep(lam_re, lam_im, log_dt, b_re, b_im, c_re, c_im, d_skip, n_seg_chunks)

    for l in range(depth):
        pad = LANES - N_EXPERTS - N_EXP_GROUPS
        p = dict(
            g_mix=g_mix[l].reshape(1, d), g_ffn=g_ffn[l].reshape(1, d),
            w_in5=w_in[l][:, MIX_W:].astype(BF16), w_glu=w_glu[l].astype(BF16),
            sg_ln_g=sg_ln_g[l].reshape(1, MIX_W), sg_ln_b=sg_ln_b[l].reshape(1, MIX_W),
            w_sg=w_sg[l].astype(BF16),
            b_sg_full=jnp.repeat(b_sg[l].T, SG_GROUP_W, axis=1),
            w_conv=w_conv[l], w_branch=w_branch[l].astype(BF16), w_merge=w_merge[l].astype(BF16),
            b_merge=b_merge[l].reshape(1, N_BRANCH * d), w_o=w_o[l].astype(BF16),
            w_router=_split_bf16(jnp.pad(jnp.concatenate([w_exp[l], w_grp[l]], axis=1), ((0, 0), (0, pad)))),
            b_router=jnp.pad(jnp.concatenate([b_exp[l], b_grp[l]]), (0, pad)).reshape(1, LANES),
        )
        mod = mods[l]

        last = l == depth - 1
        first_tile = n_ctx_tiles if last else 0
        ug = _u_proj(z_lat, z_ctx, lat_off, n, mod, p["g_mix"], w_in[l][:, :MIX_W].astype(BF16), n_ctx_tiles)
        yg = _s5_apply(ug, s5_m[l], s5_bp[l], s5_cp[l], s5_tab[l], s5_pow[l], n_ctx // S5_T, ctx_rows // S5_T)
        z1, xs, route, pc = _mixer(z_lat, z_ctx, lat_off, n, yg, mod, p, n_ctx_tiles, min(n_ctx, TM), first_tile)
        n_work = (nt - first_tile) * MOE_SLOTS // MOE_TE + N_EXPERTS
        src, owner = _moe_plan(pc[:, 0, :N_EXPERTS], n_work)
        yt = _moe_experts(xs, src, owner, w1, w3, w2, l)
        z = _combine(yt, route, z1, mod, g_final.reshape(1, d) if last else p["g_ffn"], last,
                     n_ctx_tiles - first_tile)
        z_lat, z_ctx, lat_off = z, z, 0

    return z.reshape(bsz, seq, d)
```

```python
import functools

import jax
import jax.numpy as jnp
from jax import lax
from jax.experimental import pallas as pl
from jax.experimental.pallas import tpu as pltpu

F32 = jnp.float32
BF16 = jnp.bfloat16
HIGHEST = lax.Precision.HIGHEST

D_MODEL = 1024
GRID_W = 64
MIX_W = D_MODEL // 2
N_BRANCH = 3
S5_GROUP = 16
S5_GROUPS = MIX_W // S5_GROUP
S5_STATE = 64
SG_CHUNK = 128
SG_GROUPS = 4
SG_GROUP_W = MIX_W // SG_GROUPS
CONV_K = 3
N_EXP_GROUPS = 4
EXPERTS_PER_GROUP = 8
N_EXPERTS = N_EXP_GROUPS * EXPERTS_PER_GROUP
D_FF_EXPERT = D_MODEL // 4
NORM_EPS = 1e-6

LANES = 128
SUBLANES = 8
S5_T = 16
S5_CW = S5_T * S5_GROUP
S5_GB = 4
CTX_ALIGN = 512
TM = 512
MOE_BLK = 16
MOE_SLOTS = 2 * TM + N_EXPERTS * MOE_BLK
MOE_TE = 512
MOE_NB = MOE_TE // MOE_BLK
ROUTER_ROWS = 40
MOE_W = D_MODEL + LANES
PLAN_SLACK = 4
PLAN_UNROLL = 3
VMEM_LIMIT = 56 * 1024 * 1024


def _dot(a, b):
    return jnp.dot(a, b, preferred_element_type=F32)


def _dot_hi(a, b):
    return jnp.dot(a, b, preferred_element_type=F32, precision=HIGHEST)


def _sigmoid(x):
    return 0.5 * jnp.tanh(0.5 * x) + 0.5


def _split_bf16(w):
    hi = w.astype(BF16)
    return jnp.stack([hi, (w - hi.astype(F32)).astype(BF16)])


def _rms(x, g):
    return x * lax.rsqrt(jnp.mean(x * x, axis=-1, keepdims=True) + NORM_EPS) * g


def _block_transpose(xs):
    nblk = LANES // S5_GROUP
    assert len(xs) == nblk == 8
    blk = lax.broadcasted_iota(jnp.int32, xs[0].shape, 1) // S5_GROUP
    for k in range(3):
        dist = 1 << k
        hi_half = (blk & dist) != 0
        out = list(xs)
        for a in range(nblk):
            if a & dist == 0:
                p = a + dist
                out[a] = jnp.where(hi_half, pltpu.roll(xs[p], dist * S5_GROUP, 1), xs[a])
                out[p] = jnp.where(hi_half, xs[p], pltpu.roll(xs[a], LANES - dist * S5_GROUP, 1))
        xs = out
    return xs


def _const_spec(shape):
    nd = len(shape)
    return pl.BlockSpec(shape, lambda *_: (0,) * nd, pipeline_mode=pl.Buffered(1))


def _ada_kernel(cc_ref, w_ref, b_ref, o_ref):
    cc = cc_ref[...]
    o_ref[...] = _dot_hi(cc * _sigmoid(cc), w_ref[...]) + b_ref[...]


def _ada(cc, w_ada, b_ada):
    depth = w_ada.shape[0]
    nblk = 6
    return pl.pallas_call(
        _ada_kernel,
        grid=(depth, nblk),
        in_specs=[
            pl.BlockSpec((8, D_MODEL), lambda l, j: (0, 0)),
            pl.BlockSpec((None, D_MODEL, D_MODEL), lambda l, j: (l, 0, j)),
            pl.BlockSpec((None, 1, D_MODEL), lambda l, j: (l, 0, j)),
        ],
        out_specs=pl.BlockSpec((None, 8, D_MODEL), lambda l, j: (l, 0, j)),
        out_shape=jax.ShapeDtypeStruct((depth, 8, 6 * D_MODEL), F32),
        name="ada",
    )(cc, w_ada, b_ada.reshape(depth, 1, 6 * D_MODEL))


def _mod_spec(n_ctx_tiles, first_tile=0):
    return pl.BlockSpec((None, 1, 6 * D_MODEL), lambda i: (jnp.where(i + first_tile < n_ctx_tiles, 0, 1), 0, 0))


def _token_specs(n_ctx_tiles, lat_off, first_tile=0):
    lat = pl.BlockSpec((TM, D_MODEL), lambda i: (jnp.maximum(i + first_tile - lat_off, 0), 0))
    ctx = pl.BlockSpec((TM, D_MODEL), lambda i: (jnp.minimum(i + first_tile, n_ctx_tiles - 1), 0))
    return lat, ctx


def _u_kernel(n_ctx_tiles, zl_ref, zc_ref, mod_ref, g_ref, w_ref, ug_ref, scr_ref):
    mod = mod_ref[...]
    z = jnp.where(pl.program_id(0) < n_ctx_tiles, zc_ref[...], zl_ref[...])
    h = _rms(z, g_ref[...]) * (1.0 + mod[:, D_MODEL:2 * D_MODEL]) + mod[:, 0:D_MODEL]
    u = _dot(h.astype(BF16), w_ref[...])
    n_chunks = TM // S5_T
    for q in range(MIX_W // LANES):
        scr_ref[q] = u[:, q * LANES:(q + 1) * LANES]
    for q in range(MIX_W // LANES):
        for half in range(S5_T // SUBLANES):
            rows = [scr_ref[q, pl.ds(half * SUBLANES + a, n_chunks, stride=S5_T), :] for a in range(SUBLANES)]
            for b, y in enumerate(_block_transpose(rows)):
                ug_ref[q * SUBLANES + b, :, half * LANES:(half + 1) * LANES] = y.astype(BF16)


def _u_proj(z_lat, z_ctx, lat_off, n, mod, g_mix, w_u, n_ctx_tiles):
    return pl.pallas_call(
        functools.partial(_u_kernel, n_ctx_tiles),
        grid=(n // TM,),
        in_specs=[
            *_token_specs(n_ctx_tiles, lat_off),
            _mod_spec(n_ctx_tiles),
            _const_spec((1, D_MODEL)),
            _const_spec((D_MODEL, MIX_W)),
        ],
        out_specs=pl.BlockSpec((S5_GROUPS, TM // S5_T, S5_CW), lambda i: (0, i, 0)),
        out_shape=jax.ShapeDtypeStruct((S5_GROUPS, n // S5_T, S5_CW), BF16),
        scratch_shapes=[pltpu.VMEM((MIX_W // LANES, TM, LANES), F32)],
        name="u_proj",
    )(z_lat, z_ctx, mod, g_mix, w_u)


def _s5_prep_kernel(n_seg_chunks, par_ref, bt_ref, ct_ref, dsk_ref, m_ref, bp_ref, cp_ref, tab_ref, pow_ref):
    t = S5_T
    par = par_ref[...]
    lo = lax.broadcasted_iota(jnp.int32, (1, LANES), 1) < S5_STATE
    sgn = jnp.where(lo, -1.0, 1.0)

    def rows(n):
        return lax.broadcasted_iota(jnp.int32, (n, 1), 0).astype(F32)

    def expand(pw_rows, vec_rows):
        pr, pi = pw_rows
        vr, vi = vec_rows
        pr, pi = pr[:, None, :], pi[:, None, :]
        vr, vi = vr[None, :, :], vi[None, :, :]
        re = (pr * vr - pi * vi).reshape(-1, LANES)
        im = (pr * vi + pi * vr).reshape(-1, LANES)
        return re, im

    mt = None
    bp_cols, cp_rows, tab_rows, pow_rows = [], [], [], []
    for d in range(2):
        lre, lim = par[2 * d:2 * d + 1], par[2 * d + 1:2 * d + 2]
        dt = jnp.exp(par[4 + d:5 + d])
        xr, xi = lre * dt, lim * dt

        def cpow(m):
            mag = jnp.exp(m * xr)
            return mag * jnp.cos(m * xi), mag * jnp.sin(m * xi)

        ar, ai = cpow(1.0)
        den = lre * lre + lim * lim
        qr = ((ar - 1.0) * lre + ai * lim) / den
        qi = (ai * lre - (ar - 1.0) * lim) / den
        br, bi = bt_ref[2 * d], bt_ref[2 * d + 1]
        bb = (qr * br - qi * bi, qr * bi + qi * br)
        cc = (ct_ref[2 * d], ct_ref[2 * d + 1])

        inj_pow = cpow(t - 1.0 - rows(t)) if d == 0 else cpow(rows(t))
        re, im = expand(inj_pow, bb)
        bp_cols.append(jnp.where(lo, re, im))

        out_pow = cpow(rows(t) + 1.0) if d == 0 else cpow(t - rows(t))
        re, im = expand(out_pow, cc)
        cp_rows.append(jnp.where(lo, re, -im).T)

        lag_pow = cpow(rows(t)) if d == 0 else cpow(t - 1.0 - rows(t))
        re, im = expand(lag_pow, cc)
        ctab = jnp.where(lo, re, -im)
        bcat = jnp.broadcast_to(jnp.where(lo, bb[0], bb[1])[None], (t, S5_GROUP, LANES)).reshape(-1, LANES)
        k256 = lax.dot_general(ctab, bcat, (((1,), (1,)), ((), ())), precision=HIGHEST,
                               preferred_element_type=F32)
        col_blk = lax.broadcasted_iota(jnp.int32, (S5_CW, S5_CW), 1) // S5_GROUP
        acc = jnp.zeros((S5_CW, S5_CW), F32)
        for r in range(t):
            sh = r * S5_GROUP if d == 0 else (t - 1 - r) * S5_GROUP
            if sh == 0:
                shifted = k256
            elif d == 0:
                shifted = jnp.concatenate([jnp.zeros((sh, S5_CW), F32), k256[:S5_CW - sh]], axis=0)
            else:
                shifted = jnp.concatenate([k256[sh:], jnp.zeros((sh, S5_CW), F32)], axis=0)
            acc = acc + jnp.where(col_blk == r, shifted, 0.0)
        mt = acc if mt is None else mt + acc

        for m in (float(t), float(t * n_seg_chunks)):
            re, im = cpow(m)
            tab_rows += [re, sgn * im]
        kk = rows(n_seg_chunks)
        re, im = cpow(t * kk) if d == 0 else cpow(t * (n_seg_chunks - 1.0 - kk))
        pow_rows += [re, sgn * im]

    eye = (lax.broadcasted_iota(jnp.int32, (S5_CW, S5_CW), 0) == lax.broadcasted_iota(jnp.int32, (S5_CW, S5_CW), 1))
    mt = mt + jnp.where(eye, dsk_ref[...], 0.0)
    m_ref[...] = mt.T.astype(BF16)
    bp_ref[...] = jnp.concatenate(bp_cols, axis=1).astype(BF16)
    cp_ref[...] = jnp.concatenate(cp_rows, axis=0).astype(BF16)
    tab_ref[...] = jnp.concatenate(tab_rows, axis=0)
    for q in range(4):
        pow_ref[q] = pow_rows[q]


def _s5_prep(lam_re, lam_im, log_dt, b_re, b_im, c_re, c_im, d_skip, n_seg_chunks):
    depth = lam_re.shape[0]
    g, p_, h = S5_GROUPS, S5_STATE, S5_GROUP
    dup = lambda a: jnp.concatenate([a, a], axis=-1)
    dirs_last = lambda a: jnp.moveaxis(a, 1, 2)
    lam = jnp.stack([lam_re, lam_im], axis=3)
    lam = dirs_last(lam).reshape(depth, g, 4, p_)
    ldt = jnp.broadcast_to(dirs_last(log_dt[..., None]), (depth, g, 2, p_))
    par = dup(jnp.concatenate([lam, ldt, jnp.zeros((depth, g, 2, p_), F32)], axis=2))
    bt = jnp.stack([b_re, b_im], axis=3)
    bt = dup(jnp.swapaxes(dirs_last(bt), -1, -2).reshape(depth, g, 4, h, p_))
    ct = dup(dirs_last(jnp.stack([c_re, c_im], axis=3)).reshape(depth, g, 4, h, p_))
    dsk = jnp.tile(d_skip.reshape(depth, g, 1, h), (1, 1, 1, S5_T))

    blk = lambda *shape: pl.BlockSpec((None, None) + shape, lambda l, j: (l, j) + (0,) * len(shape))
    return pl.pallas_call(
        functools.partial(_s5_prep_kernel, n_seg_chunks),
        grid=(depth, g),
        in_specs=[blk(8, LANES), blk(4, h, LANES), blk(4, h, LANES), blk(1, S5_CW)],
        out_specs=[blk(S5_CW, S5_CW), blk(S5_CW, 2 * LANES), blk(2 * LANES, S5_CW), blk(8, LANES),
                   blk(4, n_seg_chunks, LANES)],
        out_shape=[
            jax.ShapeDtypeStruct((depth, g, S5_CW, S5_CW), BF16),
            jax.ShapeDtypeStruct((depth, g, S5_CW, 2 * LANES), BF16),
            jax.ShapeDtypeStruct((depth, g, 2 * LANES, S5_CW), BF16),
            jax.ShapeDtypeStruct((depth, g, 8, LANES), F32),
            jax.ShapeDtypeStruct((depth, g, 4, n_seg_chunks, LANES), F32),
        ],
        name="s5_prep",
    )(par, bt, ct, dsk)


def _cmul_step(s, s_sw, v, v_sw, re, ims):
    return re * s + ims * s_sw + v, re * s_sw - ims * s + v_sw


def _s5_kernel(n_ctx_chunks, n_lat0, ug_ref, m_ref, bp_ref, cp_ref, tab_ref, pow_ref, y_ref,
               vf_ref, vfs_ref, vb_ref, vbs_ref):
    gb, n_rows, _ = ug_ref.shape
    n_seg = (n_rows - n_lat0) // SUBLANES
    pitch = vf_ref.shape[1] // SUBLANES
    half = S5_STATE

    def put_segments(ref, g, val):
        for r in range(SUBLANES):
            ref[g, r * pitch:r * pitch + n_seg, :] = val[r * n_seg:(r + 1) * n_seg]

    tab = tab_ref[...]
    ctx_end = []
    sp_ctx = []
    for g in range(gb):
        v = _dot(ug_ref[g], bp_ref[g])
        vl = v[n_lat0:]
        put_segments(vf_ref, g, vl[:, :LANES])
        put_segments(vb_ref, g, vl[:, LANES:])
        put_segments(vfs_ref, g, pltpu.roll(vl[:, :LANES], half, 1))
        put_segments(vbs_ref, g, pltpu.roll(vl[:, LANES:], half, 1))

        vc = v[:n_ctx_chunks]
        vcs = jnp.concatenate([pltpu.roll(vc[:, :LANES], half, 1), pltpu.roll(vc[:, LANES:], half, 1)], axis=1)
        zero = jnp.zeros((1, LANES), F32)
        sf, sfs, sb, sbs = zero, zero, zero, zero
        rows_f, rows_b = [], [None] * n_ctx_chunks
        for j in range(n_ctx_chunks):
            jb = n_ctx_chunks - 1 - j
            rows_f.append(sf)
            rows_b[jb] = sb
            sf, sfs = _cmul_step(sf, sfs, vc[j:j + 1, :LANES], vcs[j:j + 1, :LANES], tab[g, 0:1], tab[g, 1:2])
            sb, sbs = _cmul_step(sb, sbs, vc[jb:jb + 1, LANES:], vcs[jb:jb + 1, LANES:], tab[g, 4:5], tab[g, 5:6])
        ctx_end.append((sf, sfs, sb, sbs))
        sp_ctx.append(jnp.concatenate([jnp.concatenate(rows_f, axis=0), jnp.concatenate(rows_b, axis=0)], axis=1))

    coef = [[jnp.broadcast_to(tab[g, q:q + 1], (SUBLANES, LANES)) for q in (0, 1, 4, 5)] for g in range(gb)]

    def step(k, carry):
        kb = n_seg - 1 - k
        out = []
        for g in range(gb):
            sf, sfs, sb, sbs = carry[g]
            are_f, aim_f, are_b, aim_b = coef[g]
            at_f = pl.ds(k, SUBLANES, stride=pitch)
            at_b = pl.ds(kb, SUBLANES, stride=pitch)
            v_f, v_fs = vf_ref[g, at_f, :], vfs_ref[g, at_f, :]
            v_b, v_bs = vb_ref[g, at_b, :], vbs_ref[g, at_b, :]
            vf_ref[g, at_f, :] = sf
            vb_ref[g, at_b, :] = sb
            sf, sfs = _cmul_step(sf, sfs, v_f, v_fs, are_f, aim_f)
            sb, sbs = _cmul_step(sb, sbs, v_b, v_bs, are_b, aim_b)
            out.append((sf, sfs, sb, sbs))
        return tuple(out)

    zero8 = jnp.zeros((SUBLANES, LANES), F32)
    ends = lax.fori_loop(0, n_seg, step, tuple((zero8, zero8, zero8, zero8) for _ in range(gb)))

    for g in range(gb):
        ef, efs, eb, ebs = ends[g]
        sf, sfs, sb, sbs = ctx_end[g]
        ent_f, ent_b = [], [None] * SUBLANES
        for r in range(SUBLANES):
            rb = SUBLANES - 1 - r
            ent_f.append((sf, sfs))
            ent_b[rb] = (sb, sbs)
            sf, sfs = _cmul_step(sf, sfs, ef[r:r + 1], efs[r:r + 1], tab[g, 2:3], tab[g, 3:4])
            sb, sbs = _cmul_step(sb, sbs, eb[rb:rb + 1], ebs[rb:rb + 1], tab[g, 6:7], tab[g, 7:8])

        segs = []
        for r in range(SUBLANES):
            seg_rows = slice(r * pitch, r * pitch + n_seg)
            f = vf_ref[g, seg_rows, :] + pow_ref[g, 0] * ent_f[r][0] + pow_ref[g, 1] * ent_f[r][1]
            b = vb_ref[g, seg_rows, :] + pow_ref[g, 2] * ent_b[r][0] + pow_ref[g, 3] * ent_b[r][1]
            segs.append(jnp.concatenate([f, b], axis=1))
        pad = jnp.zeros((n_lat0 - n_ctx_chunks, 2 * LANES), F32)
        sprev = jnp.concatenate([sp_ctx[g], pad] + segs, axis=0).astype(BF16)
        y_ref[g] = _dot(ug_ref[g], m_ref[g]) + _dot(sprev, cp_ref[g])


def _s5_apply(ug, m, bp, cp, tab, pw, n_ctx_chunks, n_lat0):
    n_groups, n_rows, cw = ug.shape
    n_seg = pw.shape[-2]
    assert n_rows - n_lat0 == SUBLANES * n_seg and n_lat0 % SUBLANES == 0
    blk = lambda *shape: pl.BlockSpec((S5_GB,) + shape, lambda i: (i,) + (0,) * len(shape))
    seg = pltpu.VMEM((S5_GB, SUBLANES * (n_seg + SUBLANES), LANES), F32)
    return pl.pallas_call(
        functools.partial(_s5_kernel, n_ctx_chunks, n_lat0),
        grid=(n_groups // S5_GB,),
        in_specs=[blk(n_rows, cw), blk(cw, cw), blk(cw, 2 * LANES), blk(2 * LANES, cw), blk(8, LANES),
                  blk(4, n_seg, LANES)],
        out_specs=blk(n_rows, cw),
        out_shape=jax.ShapeDtypeStruct((n_groups, n_rows, cw), F32),
        scratch_shapes=[seg, seg, seg, seg],
        compiler_params=pltpu.CompilerParams(vmem_limit_bytes=VMEM_LIMIT),
        name="s5_scan",
    )(ug, m, bp, cp, tab, pw)


def _mixer_kernel(n_ctx_tiles, ctx_row_len, first_tile, zl_ref, zc_ref, y_ref, mod_ref, gmix_ref, gffn_ref, win_ref, wglu_ref,
                  lng_ref, lnb_ref, wsg_ref, bsg_ref, wconv_ref, wbr_ref, wmg_ref, bmg_ref, wo_ref, wr_ref,
                  br_ref, z1_ref, xs_ref, route_ref, pc_ref, yscr_ref):
    d = D_MODEL
    n_chunks = TM // S5_T
    for q in range(MIX_W // LANES):
        for half in range(S5_T // SUBLANES):
            cols = [y_ref[q * SUBLANES + b, :, half * LANES:(half + 1) * LANES] for b in range(SUBLANES)]
            for a, rows in enumerate(_block_transpose(cols)):
                yscr_ref[q, pl.ds(half * SUBLANES + a, n_chunks, stride=S5_T), :] = rows
    y_tok = jnp.concatenate([yscr_ref[q] for q in range(MIX_W // LANES)], axis=1)
    is_ctx = pl.program_id(0) + first_tile < n_ctx_tiles
    z = jnp.where(is_ctx, zc_ref[...], zl_ref[...])
    mod = mod_ref[...]
    sh1, sc1, ga1 = mod[:, 0:d], mod[:, d:2 * d], mod[:, 2 * d:3 * d]
    sh2, sc2 = mod[:, 3 * d:4 * d], mod[:, 4 * d:5 * d]
    h = _rms(z, gmix_ref[...]) * (1.0 + sc1) + sh1
    hb = h.astype(BF16)
    proj = _dot(hb, win_ref[...])
    su, sv, cgb, cgc, chh = [proj[:, k * MIX_W:(k + 1) * MIX_W] for k in range(5)]

    def merge_term(k, y_k):
        gate = _sigmoid(_dot(hb, wmg_ref[:, k * d:(k + 1) * d]) + bmg_ref[:, k * d:(k + 1) * d])
        return gate * _dot(y_k.astype(BF16), wbr_ref[k])

    yg = jax.nn.gelu(y_tok)
    merged = merge_term(0, yg * _sigmoid(_dot(yg.astype(BF16), wglu_ref[...])))

    gv = jax.nn.gelu(sv)
    mu = jnp.mean(gv, axis=-1, keepdims=True)
    var = jnp.mean(jnp.square(gv - mu), axis=-1, keepdims=True)
    vn = ((gv - mu) * lax.rsqrt(var + NORM_EPS) * lng_ref[...] + lnb_ref[...]).astype(BF16)
    bsg = bsg_ref[...]
    chunks = []
    for n in range(TM // SG_CHUNK):
        rows = slice(n * SG_CHUNK, (n + 1) * SG_CHUNK)
        cols = [_dot(wsg_ref[g], vn[rows, g * SG_GROUP_W:(g + 1) * SG_GROUP_W]) for g in range(SG_GROUPS)]
        chunks.append(jnp.concatenate(cols, axis=1) + bsg)
    merged = merged + merge_term(1, jax.nn.gelu(su) * jnp.concatenate(chunks, axis=0))

    zc = cgc * chh
    row = lax.broadcasted_iota(jnp.int32, (TM, 1), 0)
    pos = jnp.where(is_ctx, row % ctx_row_len, row % GRID_W)
    prev = jnp.where(pos == 0, 0.0, pltpu.roll(zc, 1, 0))
    nxt = jnp.where(pos == jnp.where(is_ctx, ctx_row_len - 1, GRID_W - 1), 0.0, pltpu.roll(zc, TM - 1, 0))
    wc = wconv_ref[...]
    merged = merged + merge_term(2, cgb * (wc[0:1] * prev + wc[1:2] * zc + wc[2:3] * nxt))

    z1 = z + ga1 * _dot(merged.astype(BF16), wo_ref[...])
    z1_ref[...] = z1

    h2 = _rms(z1, gffn_ref[...]) * (1.0 + sc2) + sh2
    h2_hi = h2.astype(BF16)
    h2_lo = (h2 - h2_hi.astype(F32)).astype(BF16)
    nt_dims = (((1,), (1,)), ((), ()))
    logits_t = (lax.dot_general(wr_ref[0], h2_hi, nt_dims, preferred_element_type=F32)
                + lax.dot_general(wr_ref[0], h2_lo, nt_dims, preferred_element_type=F32)
                + lax.dot_general(wr_ref[1], h2_hi, nt_dims, preferred_element_type=F32)) + br_ref[...]
    lg = logits_t[0:ROUTER_ROWS]
    row = lax.broadcasted_iota(jnp.int32, (ROUTER_ROWS, TM), 0)
    big = jnp.int32(LANES)
    neg = jnp.float32(-jnp.inf)
    top = lambda v: jnp.max(v, axis=0, keepdims=True)
    total = lambda v: jnp.sum(v, axis=0, keepdims=True)
    first = lambda m: jnp.min(jnp.where(m, row, big), axis=0, keepdims=True)
    is_grp = (row >= N_EXPERTS) & (row < N_EXPERTS + N_EXP_GROUPS)
    gl = jnp.where(is_grp, lg, neg)
    ge = jnp.exp(gl - top(gl))
    gp = ge / total(ge)
    g_p = top(gp)
    g_idx = first(is_grp & (gp == g_p)) - N_EXPERTS
    in_grp = (row >= g_idx * EXPERTS_PER_GROUP) & (row < (g_idx + 1) * EXPERTS_PER_GROUP)
    el = jnp.where(in_grp, lg, neg)
    ee = jnp.exp(el - top(el))
    ep = ee / total(ee)
    p1 = top(ep)
    i1 = first(in_grp & (ep == p1))
    rest = in_grp & (row != i1)
    ep2 = jnp.where(rest, ep, -1.0)
    p2 = top(ep2)
    i2 = first(rest & (ep2 == p2))
    tot = p1 + p2
    w_1, w_2 = g_p * (p1 / tot), g_p * (p2 / tot)

    oh1, oh2 = row == i1, row == i2
    oh = jnp.where(oh1 | oh2, 1.0, 0.0)
    cnt = jnp.sum(oh, axis=1, keepdims=True)
    pcb = jnp.floor((cnt + (MOE_BLK - 1)) * (1.0 / MOE_BLK))
    pc_ref[...] = jnp.broadcast_to(pcb * MOE_BLK, (ROUTER_ROWS, LANES)).astype(jnp.int32)
    er, ec = (lax.broadcasted_iota(jnp.int32, (ROUTER_ROWS, ROUTER_ROWS), k) for k in (0, 1))
    before = jnp.where(ec < er, 1.0, 0.0).astype(BF16)
    start = _dot(before, jnp.broadcast_to(pcb, (ROUTER_ROWS, LANES)).astype(BF16))[:, 0:1] * MOE_BLK
    tr, tc = (lax.broadcasted_iota(jnp.int32, (TM, TM), k) for k in (0, 1))
    earlier = jnp.where(tr < tc, 1.0, 0.0).astype(BF16)
    slot = start + _dot(oh.astype(BF16), earlier)
    slot1 = total(jnp.where(oh1, slot, 0.0))
    slot2 = total(jnp.where(oh2, slot, 0.0))

    def token_rows(vals):
        r8 = lax.broadcasted_iota(jnp.int32, (SUBLANES, TM), 0)
        rows8 = jnp.zeros((SUBLANES, TM), F32)
        for k, v in enumerate(vals):
            rows8 = jnp.where(r8 == k, v, rows8)
        return jnp.concatenate([rows8, jnp.zeros((LANES - SUBLANES, TM), F32)], axis=0).T

    route_ref[...] = token_rows([slot1, slot2])
    sl = xs_ref.shape[0]
    slot_row = lax.broadcasted_iota(jnp.int32, (sl, TM), 0).astype(F32)
    perm1 = jnp.where(slot_row == slot1, 1.0, 0.0).astype(BF16)
    perm2 = jnp.where(slot_row == slot2, 1.0, 0.0).astype(BF16)
    xs_ref[:, 0:d] = _dot(perm1 + perm2, h2_hi).astype(BF16)

    def weight_lanes(w):
        hi = w.astype(BF16).astype(F32)
        return token_rows([hi, w - hi]).astype(BF16)

    xs_ref[:, d:] = (_dot(perm1, weight_lanes(w_1)) + _dot(perm2, weight_lanes(w_2))).astype(BF16)


def _mixer(z_lat, z_ctx, lat_off, n, yg, mod, p, n_ctx_tiles, ctx_row_len, first_tile):
    d = D_MODEL
    nt = n // TM - first_tile
    tile = lambda w: pl.BlockSpec((TM, w), lambda i: (i, 0))
    return pl.pallas_call(
        functools.partial(_mixer_kernel, n_ctx_tiles, ctx_row_len, first_tile),
        grid=(nt,),
        in_specs=[
            *_token_specs(n_ctx_tiles, lat_off, first_tile),
            pl.BlockSpec((S5_GROUPS, TM // S5_T, S5_CW), lambda i: (0, i + first_tile, 0)),
            _mod_spec(n_ctx_tiles, first_tile),
            _const_spec((1, d)), _const_spec((1, d)),
            _const_spec((d, 5 * MIX_W)), _const_spec((MIX_W, MIX_W)),
            _const_spec((1, MIX_W)), _const_spec((1, MIX_W)),
            _const_spec((SG_GROUPS, SG_CHUNK, SG_CHUNK)), _const_spec((SG_CHUNK, MIX_W)),
            _const_spec((CONV_K, MIX_W)), _const_spec((N_BRANCH, MIX_W, d)),
            _const_spec((d, N_BRANCH * d)), _const_spec((1, N_BRANCH * d)),
            _const_spec((d, d)), _const_spec((2, LANES, d)), _const_spec((LANES, 1)),
        ],
        out_specs=[tile(d), pl.BlockSpec((None, MOE_SLOTS, MOE_W), lambda i: (i, 0, 0)), tile(LANES),
                   pl.BlockSpec((None, ROUTER_ROWS, LANES), lambda i: (i, 0, 0))],
        out_shape=[
            jax.ShapeDtypeStruct((nt * TM, d), F32),
            jax.ShapeDtypeStruct((nt, MOE_SLOTS, MOE_W), BF16),
            jax.ShapeDtypeStruct((nt * TM, LANES), F32),
            jax.ShapeDtypeStruct((nt, ROUTER_ROWS, LANES), jnp.int32),
        ],
        scratch_shapes=[pltpu.VMEM((MIX_W // LANES, TM, LANES), F32)],
        compiler_params=pltpu.CompilerParams(vmem_limit_bytes=VMEM_LIMIT),
        name="mixer",
    )(z_lat, z_ctx, yg, mod, p["g_mix"], p["g_ffn"], p["w_in5"], p["w_glu"], p["sg_ln_g"], p["sg_ln_b"],
      p["w_sg"], p["b_sg_full"], p["w_conv"], p["w_branch"], p["w_merge"], p["b_merge"], p["w_o"],
      p["w_router"], p["b_router"])


def _plan_kernel(nt, n_work, pc_ref, src_ref, exp_ref, run_ref):
    bpt = MOE_SLOTS // MOE_BLK

    def clear(i, c):
        run_ref[i] = 0
        return c

    lax.fori_loop(0, nt, clear, 0)

    def no_block(b, c):
        src_ref[b] = -1
        return c

    def expert(e, g0):
        def tile(i, pos):
            nb = pc_ref[i, e] // MOE_BLK
            s0 = i * bpt + run_ref[i] // MOE_BLK
            base = g0 * MOE_NB + pos

            def blk(b, c):
                src_ref[base + b] = s0 + b
                return c

            for b in range(PLAN_SLACK):
                src_ref[base + b] = s0 + b
            lax.fori_loop(PLAN_SLACK, nb, blk, 0)
            run_ref[i] = run_ref[i] + pc_ref[i, e]
            return pos + nb

        def tiles(j, pos):
            for k in range(PLAN_UNROLL):
                pos = tile(j * PLAN_UNROLL + k, pos)
            return pos

        pos = lax.fori_loop(0, nt // PLAN_UNROLL, tiles, 0)
        for i in range(nt - nt % PLAN_UNROLL, nt):
            pos = tile(i, pos)
        n_tiles = (pos + MOE_NB - 1) // MOE_NB
        lax.fori_loop(g0 * MOE_NB + pos, (g0 + n_tiles) * MOE_NB, no_block, 0)

        def owner(t, c):
            exp_ref[g0 + t] = e
            return c

        lax.fori_loop(0, n_tiles, owner, 0)
        return g0 + n_tiles

    g_end = lax.fori_loop(0, N_EXPERTS, expert, 0)
    lax.fori_loop(g_end * MOE_NB, n_work * MOE_NB + PLAN_SLACK, no_block, 0)

    def idle(g, c):
        exp_ref[g] = N_EXPERTS - 1
        return c

    lax.fori_loop(g_end, n_work, idle, 0)


def _moe_plan(pc, n_work):
    nt = pc.shape[0]
    smem = pl.BlockSpec(memory_space=pltpu.SMEM)
    return pl.pallas_call(
        functools.partial(_plan_kernel, nt, n_work),
        in_specs=[smem],
        out_specs=[smem, smem],
        out_shape=[jax.ShapeDtypeStruct((n_work * MOE_NB + PLAN_SLACK,), jnp.int32),
                   jax.ShapeDtypeStruct((n_work,), jnp.int32)],
        scratch_shapes=[pltpu.SMEM((nt,), jnp.int32)],
        name="moe_plan",
    )(pc)


def _expert_kernel(src_ref, exp_ref, xs_hbm, w1_ref, w3_ref, w2_ref, yt_hbm, lhs_ref, out_ref, wb1_ref, wb3_ref,
                   wb2_ref, sem_in, sem_out):
    g = pl.program_id(0)
    n_work = pl.num_programs(0)

    def first(step):
        return src_ref[step * MOE_NB]

    def last(step):
        return src_ref[step * MOE_NB + MOE_NB - 1]

    def gather_copy(step, b):
        return pltpu.make_async_copy(xs_hbm.at[src_ref[step * MOE_NB + b]], lhs_ref.at[step % 2, b],
                                     sem_in.at[step % 2])

    def scatter_copy(step, b):
        return pltpu.make_async_copy(out_ref.at[step % 2, b], yt_hbm.at[src_ref[step * MOE_NB + b]],
                                     sem_out.at[step % 2])

    def gather_all(step):
        return pltpu.make_async_copy(xs_hbm.at[pl.ds(0, MOE_NB)], lhs_ref.at[step % 2], sem_in.at[step % 2])

    def scatter_all(step):
        return pltpu.make_async_copy(out_ref.at[step % 2], yt_hbm.at[pl.ds(0, MOE_NB)], sem_out.at[step % 2])

    def per_block(step, fn):
        def body(b, c):
            @pl.when(src_ref[step * MOE_NB + b] >= 0)
            def _():
                fn(b)
            return c
        lax.fori_loop(0, MOE_NB, body, 0)

    def start(step, copy):
        @pl.when(last(step) >= 0)
        def _():
            for b in range(MOE_NB):
                copy(step, b).start()

        @pl.when((last(step) < 0) & (first(step) >= 0))
        def _():
            per_block(step, lambda b: copy(step, b).start())

    def wait(step, copy, copy_all):
        @pl.when(last(step) >= 0)
        def _():
            copy_all(step).wait()

        @pl.when((last(step) < 0) & (first(step) >= 0))
        def _():
            per_block(step, lambda b: copy(step, b).wait())

    @pl.when(g == 0)
    def _():
        start(g, gather_copy)

    @pl.when(g + 1 < n_work)
    def _():
        start(g + 1, gather_copy)

    wait(g, gather_copy, gather_all)

    @pl.when(g >= 2)
    def _():
        wait(g - 2, scatter_copy, scatter_all)

    slot = g % 2

    @pl.when(first(g) >= 0)
    def _():
        @pl.when((g == 0) | (exp_ref[g] != exp_ref[jnp.maximum(g - 1, 0)]))
        def _():
            wb1_ref[...] = w1_ref[...].astype(BF16)
            wb3_ref[...] = w3_ref[...].astype(BF16)
            wb2_ref[...] = w2_ref[...].astype(BF16)

        @pl.when(last(g) < 0)
        def _():
            def zero_missing(b, c):
                @pl.when(src_ref[g * MOE_NB + b] < 0)
                def _():
                    lhs_ref[slot, b] = jnp.zeros((MOE_BLK, MOE_W), BF16)
                return c
            lax.fori_loop(0, MOE_NB, zero_missing, 0)

        rows = lhs_ref[slot].reshape(MOE_TE, MOE_W)
        x = rows[:, :D_MODEL]
        w_lanes = rows[:, D_MODEL:].astype(F32)
        w_row = w_lanes[:, 0:1] + w_lanes[:, 1:2]
        a = _dot(x, wb1_ref[...])
        act = a * _sigmoid(a) * _dot(x, wb3_ref[...])
        y = (_dot(act.astype(BF16), wb2_ref[...]) * w_row).astype(BF16)
        out_ref[slot] = jnp.concatenate([y, jnp.zeros((MOE_TE, LANES), BF16)], axis=1).reshape(MOE_NB, MOE_BLK, MOE_W)

    start(g, scatter_copy)

    @pl.when(g == n_work - 1)
    def _():
        @pl.when(g >= 1)
        def _():
            wait(g - 1, scatter_copy, scatter_all)
        wait(g, scatter_copy, scatter_all)


def _moe_experts(xs, src, owner, w1, w3, w2, layer):
    nt, sl, width = xs.shape
    d = D_MODEL
    n_work = owner.shape[0]
    xs_blocks = xs.reshape(nt * sl // MOE_BLK, MOE_BLK, width)
    any_spec = pl.BlockSpec(memory_space=pl.ANY)
    by_owner = lambda *shape: pl.BlockSpec((None, None) + shape, lambda g, src, own: (layer, own[g], 0, 0))
    buf = pltpu.VMEM((2, MOE_NB, MOE_BLK, width), BF16)
    yt = pl.pallas_call(
        _expert_kernel,
        grid_spec=pltpu.PrefetchScalarGridSpec(
            num_scalar_prefetch=2,
            grid=(n_work,),
            in_specs=[any_spec, by_owner(d, D_FF_EXPERT), by_owner(d, D_FF_EXPERT), by_owner(D_FF_EXPERT, d)],
            out_specs=any_spec,
            scratch_shapes=[
                buf, buf,
                pltpu.VMEM((d, D_FF_EXPERT), BF16), pltpu.VMEM((d, D_FF_EXPERT), BF16),
                pltpu.VMEM((D_FF_EXPERT, d), BF16),
                pltpu.SemaphoreType.DMA((2,)), pltpu.SemaphoreType.DMA((2,)),
            ],
        ),
        out_shape=jax.ShapeDtypeStruct(xs_blocks.shape, BF16),
        input_output_aliases={2: 0},
        compiler_params=pltpu.CompilerParams(dimension_semantics=("arbitrary",), vmem_limit_bytes=VMEM_LIMIT),
        name="moe_experts",
    )(src, owner, xs_blocks, w1, w3, w2)
    return yt.reshape(nt, sl, width)


def _combine_kernel(final, yt_ref, route_ref, z1_ref, mod_ref, g_ref, o_ref):
    route = route_ref[...]
    slot1, slot2 = route[:, 0:1], route[:, 1:2]
    slot_lane = lax.broadcasted_iota(jnp.int32, (TM, MOE_SLOTS), 1).astype(F32)
    both = jnp.where((slot_lane == slot1) | (slot_lane == slot2), 1.0, 0.0).astype(BF16)
    moe = _dot(both, yt_ref[:, 0:D_MODEL])
    ga2 = mod_ref[...][:, 5 * D_MODEL:6 * D_MODEL]
    z2 = z1_ref[...] + ga2 * moe
    o_ref[...] = _rms(z2, g_ref[...]) if final else z2


def _combine(yt, route, z1, mod, g_final, final, n_ctx_tiles):
    n, d = z1.shape
    tile = lambda w: pl.BlockSpec((TM, w), lambda i: (i, 0))
    return pl.pallas_call(
        functools.partial(_combine_kernel, final),
        grid=(n // TM,),
        in_specs=[pl.BlockSpec((None, MOE_SLOTS, MOE_W), lambda i: (i, 0, 0)), tile(LANES), tile(d),
                  _mod_spec(n_ctx_tiles), _const_spec((1, d))],
        out_specs=tile(d),
        out_shape=jax.ShapeDtypeStruct((n, d), F32),
        compiler_params=pltpu.CompilerParams(vmem_limit_bytes=VMEM_LIMIT),
        name="combine",
    )(yt, route, z1, mod, g_final)


def kernel(x, c, ctx, c_ctx, w_ada, b_ada, g_mix, g_ffn, w_in, lam_re, lam_im, log_dt, b_re, b_im, c_re, c_im, d_skip, w_glu, sg_ln_g, sg_ln_b, w_sg, b_sg, w_conv, w_branch, w_merge, b_merge, w_o, w_grp, b_grp, w_exp, b_exp, w1, w3, w2, g_final):
    depth = w_ada.shape[0]
    bsz, seq, d = x.shape
    n_ctx = ctx.shape[1]
    ctx_rows = -(-n_ctx // CTX_ALIGN) * CTX_ALIGN
    n = ctx_rows + seq
    assert bsz == 1 and d == D_MODEL
    assert ctx_rows % TM == 0 and seq % TM == 0 and (TM % n_ctx == 0 or n_ctx % TM == 0)
    assert n_ctx % S5_T == 0 and seq % (S5_T * SUBLANES * SUBLANES) == 0
    n_ctx_tiles = ctx_rows // TM
    n_seg_chunks = seq // (S5_T * SUBLANES)
    nt = n // TM

    z_lat, z_ctx, lat_off = x[0], jnp.pad(ctx[0], ((0, ctx_rows - n_ctx), (0, 0))), n_ctx_tiles
    cc = jnp.zeros((8, d), F32).at[0].set(c_ctx).at[1].set(c[0])
    mods = _ada(cc, w_ada, b_ada)[:, :2].reshape(depth, 2, 1, 6 * d)
    s5_m, s5_bp, s5_cp, s5_tab, s5_pow = _s5_prep(lam_re, lam_im, log_dt, b_re, b_im, c_re, c_im, d_skip, n_seg_chunks)

    for l in range(depth):
        pad = LANES - N_EXPERTS - N_EXP_GROUPS
        p = dict(
            g_mix=g_mix[l].reshape(1, d), g_ffn=g_ffn[l].reshape(1, d),
            w_in5=w_in[l][:, MIX_W:].astype(BF16), w_glu=w_glu[l].astype(BF16),
            sg_ln_g=sg_ln_g[l].reshape(1, MIX_W), sg_ln_b=sg_ln_b[l].reshape(1, MIX_W),
            w_sg=w_sg[l].astype(BF16),
            b_sg_full=jnp.repeat(b_sg[l].T, SG_GROUP_W, axis=1),
            w_conv=w_conv[l], w_branch=w_branch[l].astype(BF16), w_merge=w_merge[l].astype(BF16),
            b_merge=b_merge[l].reshape(1, N_BRANCH * d), w_o=w_o[l].astype(BF16),
            w_router=_split_bf16(jnp.pad(jnp.concatenate([w_exp[l], w_grp[l]], axis=1).T, ((0, pad), (0, 0)))),
            b_router=jnp.pad(jnp.concatenate([b_exp[l], b_grp[l]]), (0, pad)).reshape(LANES, 1),
        )
        mod = mods[l]

        last = l == depth - 1
        first_tile = n_ctx_tiles if last else 0
        ug = _u_proj(z_lat, z_ctx, lat_off, n, mod, p["g_mix"], w_in[l][:, :MIX_W].astype(BF16), n_ctx_tiles)
        yg = _s5_apply(ug, s5_m[l], s5_bp[l], s5_cp[l], s5_tab[l], s5_pow[l], n_ctx // S5_T, ctx_rows // S5_T)
        z1, xs, route, pc = _mixer(z_lat, z_ctx, lat_off, n, yg, mod, p, n_ctx_tiles, min(n_ctx, TM), first_tile)
        n_work = (nt - first_tile) * MOE_SLOTS // MOE_TE + N_EXPERTS
        src, owner = _moe_plan(pc[:, :N_EXPERTS, 0], n_work)
        yt = _moe_experts(xs, src, owner, w1, w3, w2, l)
        z = _combine(yt, route, z1, mod, g_final.reshape(1, d) if last else p["g_ffn"], last,
                     n_ctx_tiles - first_tile)
        z_lat, z_ctx, lat_off = z, z, 0

    return z.reshape(bsz, seq, d)
```

```python
import functools

import jax
import jax.numpy as jnp
from jax import lax
from jax.experimental import pallas as pl
from jax.experimental.pallas import tpu as pltpu

F32 = jnp.float32
BF16 = jnp.bfloat16
HIGHEST = lax.Precision.HIGHEST

D_MODEL = 1024
GRID_W = 64
MIX_W = D_MODEL // 2
N_BRANCH = 3
S5_GROUP = 16
S5_GROUPS = MIX_W // S5_GROUP
S5_STATE = 64
SG_CHUNK = 128
SG_GROUPS = 4
SG_GROUP_W = MIX_W // SG_GROUPS
CONV_K = 3
N_EXP_GROUPS = 4
EXPERTS_PER_GROUP = 8
N_EXPERTS = N_EXP_GROUPS * EXPERTS_PER_GROUP
D_FF_EXPERT = D_MODEL // 4
NORM_EPS = 1e-6

LANES = 128
SUBLANES = 8
S5_T = 16
S5_CW = S5_T * S5_GROUP
S5_GB = 4
CTX_ALIGN = 512
TM = 512
MOE_BLK = 16
MOE_SLOTS = 2 * TM + N_EXPERTS * MOE_BLK
MOE_TE = 512
MOE_NB = MOE_TE // MOE_BLK
ROUTER_ROWS = 40
MOE_W = D_MODEL + LANES
PLAN_SLACK = 4
PLAN_UNROLL = 3
VMEM_LIMIT = 56 * 1024 * 1024


def _dot(a, b):
    return jnp.dot(a, b, preferred_element_type=F32)


def _dot_hi(a, b):
    return jnp.dot(a, b, preferred_element_type=F32, precision=HIGHEST)


def _sigmoid(x):
    return 0.5 * jnp.tanh(0.5 * x) + 0.5


def _split_bf16(w):
    hi = w.astype(BF16)
    return jnp.stack([hi, (w - hi.astype(F32)).astype(BF16)])


def _rms(x, g):
    return x * lax.rsqrt(jnp.mean(x * x, axis=-1, keepdims=True) + NORM_EPS) * g


def _block_transpose(xs):
    nblk = LANES // S5_GROUP
    assert len(xs) == nblk == 8
    blk = lax.broadcasted_iota(jnp.int32, xs[0].shape, 1) // S5_GROUP
    for k in range(3):
        dist = 1 << k
        hi_half = (blk & dist) != 0
        out = list(xs)
        for a in range(nblk):
            if a & dist == 0:
                p = a + dist
                out[a] = jnp.where(hi_half, pltpu.roll(xs[p], dist * S5_GROUP, 1), xs[a])
                out[p] = jnp.where(hi_half, xs[p], pltpu.roll(xs[a], LANES - dist * S5_GROUP, 1))
        xs = out
    return xs


def _const_spec(shape):
    nd = len(shape)
    return pl.BlockSpec(shape, lambda *_: (0,) * nd, pipeline_mode=pl.Buffered(1))


def _ada_kernel(cc_ref, w_ref, b_ref, o_ref):
    cc = cc_ref[...]
    o_ref[...] = _dot_hi(cc * _sigmoid(cc), w_ref[...]) + b_ref[...]


def _ada(cc, w_ada, b_ada):
    depth = w_ada.shape[0]
    nblk = 6
    return pl.pallas_call(
        _ada_kernel,
        grid=(depth, nblk),
        in_specs=[
            pl.BlockSpec((8, D_MODEL), lambda l, j: (0, 0)),
            pl.BlockSpec((None, D_MODEL, D_MODEL), lambda l, j: (l, 0, j)),
            pl.BlockSpec((None, 1, D_MODEL), lambda l, j: (l, 0, j)),
        ],
        out_specs=pl.BlockSpec((None, 8, D_MODEL), lambda l, j: (l, 0, j)),
        out_shape=jax.ShapeDtypeStruct((depth, 8, 6 * D_MODEL), F32),
        name="ada",
    )(cc, w_ada, b_ada.reshape(depth, 1, 6 * D_MODEL))


def _mod_spec(n_ctx_tiles, first_tile=0):
    return pl.BlockSpec((None, 1, 6 * D_MODEL), lambda i: (jnp.where(i + first_tile < n_ctx_tiles, 0, 1), 0, 0))


def _token_specs(n_ctx_tiles, lat_off, first_tile=0):
    lat = pl.BlockSpec((TM, D_MODEL), lambda i: (jnp.maximum(i + first_tile - lat_off, 0), 0))
    ctx = pl.BlockSpec((TM, D_MODEL), lambda i: (jnp.minimum(i + first_tile, n_ctx_tiles - 1), 0))
    return lat, ctx


def _u_kernel(n_ctx_tiles, zl_ref, zc_ref, mod_ref, g_ref, w_ref, ug_ref, scr_ref):
    mod = mod_ref[...]
    z = jnp.where(pl.program_id(0) < n_ctx_tiles, zc_ref[...], zl_ref[...])
    h = _rms(z, g_ref[...]) * (1.0 + mod[:, D_MODEL:2 * D_MODEL]) + mod[:, 0:D_MODEL]
    u = _dot(h.astype(BF16), w_ref[...])
    n_chunks = TM // S5_T
    for q in range(MIX_W // LANES):
        scr_ref[q] = u[:, q * LANES:(q + 1) * LANES]
    for q in range(MIX_W // LANES):
        for half in range(S5_T // SUBLANES):
            rows = [scr_ref[q, pl.ds(half * SUBLANES + a, n_chunks, stride=S5_T), :] for a in range(SUBLANES)]
            for b, y in enumerate(_block_transpose(rows)):
                ug_ref[q * SUBLANES + b, :, half * LANES:(half + 1) * LANES] = y.astype(BF16)


def _u_proj(z_lat, z_ctx, lat_off, n, mod, g_mix, w_u, n_ctx_tiles):
    return pl.pallas_call(
        functools.partial(_u_kernel, n_ctx_tiles),
        grid=(n // TM,),
        in_specs=[
            *_token_specs(n_ctx_tiles, lat_off),
            _mod_spec(n_ctx_tiles),
            _const_spec((1, D_MODEL)),
            _const_spec((D_MODEL, MIX_W)),
        ],
        out_specs=pl.BlockSpec((S5_GROUPS, TM // S5_T, S5_CW), lambda i: (0, i, 0)),
        out_shape=jax.ShapeDtypeStruct((S5_GROUPS, n // S5_T, S5_CW), BF16),
        scratch_shapes=[pltpu.VMEM((MIX_W // LANES, TM, LANES), F32)],
        name="u_proj",
    )(z_lat, z_ctx, mod, g_mix, w_u)


def _s5_prep_kernel(n_seg_chunks, par_ref, bt_ref, ct_ref, dsk_ref, m_ref, bp_ref, cp_ref, tab_ref, pow_ref):
    t = S5_T
    par = par_ref[...]
    lo = lax.broadcasted_iota(jnp.int32, (1, LANES), 1) < S5_STATE
    sgn = jnp.where(lo, -1.0, 1.0)

    def rows(n):
        return lax.broadcasted_iota(jnp.int32, (n, 1), 0).astype(F32)

    def expand(pw_rows, vec_rows):
        pr, pi = pw_rows
        vr, vi = vec_rows
        pr, pi = pr[:, None, :], pi[:, None, :]
        vr, vi = vr[None, :, :], vi[None, :, :]
        re = (pr * vr - pi * vi).reshape(-1, LANES)
        im = (pr * vi + pi * vr).reshape(-1, LANES)
        return re, im

    mt = None
    bp_cols, cp_rows, tab_rows, pow_rows = [], [], [], []
    for d in range(2):
        lre, lim = par[2 * d:2 * d + 1], par[2 * d + 1:2 * d + 2]
        dt = jnp.exp(par[4 + d:5 + d])
        xr, xi = lre * dt, lim * dt

        def cpow(m):
            mag = jnp.exp(m * xr)
            return mag * jnp.cos(m * xi), mag * jnp.sin(m * xi)

        ar, ai = cpow(1.0)
        den = lre * lre + lim * lim
        qr = ((ar - 1.0) * lre + ai * lim) / den
        qi = (ai * lre - (ar - 1.0) * lim) / den
        br, bi = bt_ref[2 * d], bt_ref[2 * d + 1]
        bb = (qr * br - qi * bi, qr * bi + qi * br)
        cc = (ct_ref[2 * d], ct_ref[2 * d + 1])

        inj_pow = cpow(t - 1.0 - rows(t)) if d == 0 else cpow(rows(t))
        re, im = expand(inj_pow, bb)
        bp_cols.append(jnp.where(lo, re, im))

        out_pow = cpow(rows(t) + 1.0) if d == 0 else cpow(t - rows(t))
        re, im = expand(out_pow, cc)
        cp_rows.append(jnp.where(lo, re, -im).T)

        lag_pow = cpow(rows(t)) if d == 0 else cpow(t - 1.0 - rows(t))
        re, im = expand(lag_pow, cc)
        ctab = jnp.where(lo, re, -im)
        bcat = jnp.broadcast_to(jnp.where(lo, bb[0], bb[1])[None], (t, S5_GROUP, LANES)).reshape(-1, LANES)
        k256 = lax.dot_general(ctab, bcat, (((1,), (1,)), ((), ())), precision=HIGHEST,
                               preferred_element_type=F32)
        col_blk = lax.broadcasted_iota(jnp.int32, (S5_CW, S5_CW), 1) // S5_GROUP
        acc = jnp.zeros((S5_CW, S5_CW), F32)
        for r in range(t):
            sh = r * S5_GROUP if d == 0 else (t - 1 - r) * S5_GROUP
            if sh == 0:
                shifted = k256
            elif d == 0:
                shifted = jnp.concatenate([jnp.zeros((sh, S5_CW), F32), k256[:S5_CW - sh]], axis=0)
            else:
                shifted = jnp.concatenate([k256[sh:], jnp.zeros((sh, S5_CW), F32)], axis=0)
            acc = acc + jnp.where(col_blk == r, shifted, 0.0)
        mt = acc if mt is None else mt + acc

        for m in (float(t), float(t * n_seg_chunks)):
            re, im = cpow(m)
            tab_rows += [re, sgn * im]
        blk, j = rows(n_seg_chunks // SUBLANES), rows(SUBLANES)
        if d == 0:
            re, im = expand(cpow(t * SUBLANES * blk), cpow(t * j))
        else:
            re, im = expand(cpow(t * SUBLANES * (n_seg_chunks // SUBLANES - 1.0 - blk)), cpow(t * (SUBLANES - 1.0 - j)))
        pow_rows += [re, sgn * im]

    eye = (lax.broadcasted_iota(jnp.int32, (S5_CW, S5_CW), 0) == lax.broadcasted_iota(jnp.int32, (S5_CW, S5_CW), 1))
    mt = mt + jnp.where(eye, dsk_ref[...], 0.0)
    m_ref[...] = mt.T.astype(BF16)
    bp_ref[...] = jnp.concatenate(bp_cols, axis=1).astype(BF16)
    cp_ref[...] = jnp.concatenate(cp_rows, axis=0).astype(BF16)
    tab_ref[...] = jnp.concatenate(tab_rows, axis=0)
    for q in range(4):
        pow_ref[q] = pow_rows[q]


def _s5_prep(lam_re, lam_im, log_dt, b_re, b_im, c_re, c_im, d_skip, n_seg_chunks):
    depth = lam_re.shape[0]
    g, p_, h = S5_GROUPS, S5_STATE, S5_GROUP
    dup = lambda a: jnp.concatenate([a, a], axis=-1)
    dirs_last = lambda a: jnp.moveaxis(a, 1, 2)
    lam = jnp.stack([lam_re, lam_im], axis=3)
    lam = dirs_last(lam).reshape(depth, g, 4, p_)
    ldt = jnp.broadcast_to(dirs_last(log_dt[..., None]), (depth, g, 2, p_))
    par = dup(jnp.concatenate([lam, ldt, jnp.zeros((depth, g, 2, p_), F32)], axis=2))
    bt = jnp.stack([b_re, b_im], axis=3)
    bt = dup(jnp.swapaxes(dirs_last(bt), -1, -2).reshape(depth, g, 4, h, p_))
    ct = dup(dirs_last(jnp.stack([c_re, c_im], axis=3)).reshape(depth, g, 4, h, p_))
    dsk = jnp.tile(d_skip.reshape(depth, g, 1, h), (1, 1, 1, S5_T))

    blk = lambda *shape: pl.BlockSpec((None, None) + shape, lambda l, j: (l, j) + (0,) * len(shape))
    return pl.pallas_call(
        functools.partial(_s5_prep_kernel, n_seg_chunks),
        grid=(depth, g),
        in_specs=[blk(8, LANES), blk(4, h, LANES), blk(4, h, LANES), blk(1, S5_CW)],
        out_specs=[blk(S5_CW, S5_CW), blk(S5_CW, 2 * LANES), blk(2 * LANES, S5_CW), blk(8, LANES),
                   blk(4, n_seg_chunks, LANES)],
        out_shape=[
            jax.ShapeDtypeStruct((depth, g, S5_CW, S5_CW), BF16),
            jax.ShapeDtypeStruct((depth, g, S5_CW, 2 * LANES), BF16),
            jax.ShapeDtypeStruct((depth, g, 2 * LANES, S5_CW), BF16),
            jax.ShapeDtypeStruct((depth, g, 8, LANES), F32),
            jax.ShapeDtypeStruct((depth, g, 4, n_seg_chunks, LANES), F32),
        ],
        name="s5_prep",
    )(par, bt, ct, dsk)


def _cmul_step(s, s_sw, v, v_sw, re, ims):
    return re * s + ims * s_sw + v, re * s_sw - ims * s + v_sw


def _s5_kernel(n_ctx_chunks, n_lat0, ug_ref, m_ref, bp_ref, cp_ref, tab_ref, pow_ref, y_ref,
               vf_ref, vfs_ref, vb_ref, vbs_ref):
    gb, n_rows, _ = ug_ref.shape
    n_seg = (n_rows - n_lat0) // SUBLANES
    pitch = vf_ref.shape[1] // SUBLANES
    half = S5_STATE

    def put_segments(ref, g, val):
        for r in range(SUBLANES):
            ref[g, r * pitch:r * pitch + n_seg, :] = val[r * n_seg:(r + 1) * n_seg]

    tab = tab_ref[...]
    ctx_end = []
    sp_ctx = []
    for g in range(gb):
        v = _dot(ug_ref[g], bp_ref[g])
        vl = v[n_lat0:]
        put_segments(vf_ref, g, vl[:, :LANES])
        put_segments(vb_ref, g, vl[:, LANES:])
        put_segments(vfs_ref, g, pltpu.roll(vl[:, :LANES], half, 1))
        put_segments(vbs_ref, g, pltpu.roll(vl[:, LANES:], half, 1))

        vc = v[:n_ctx_chunks]
        vcs = jnp.concatenate([pltpu.roll(vc[:, :LANES], half, 1), pltpu.roll(vc[:, LANES:], half, 1)], axis=1)
        zero = jnp.zeros((1, LANES), F32)
        sf, sfs, sb, sbs = zero, zero, zero, zero
        rows_f, rows_b = [], [None] * n_ctx_chunks
        for j in range(n_ctx_chunks):
            jb = n_ctx_chunks - 1 - j
            rows_f.append(sf)
            rows_b[jb] = sb
            sf, sfs = _cmul_step(sf, sfs, vc[j:j + 1, :LANES], vcs[j:j + 1, :LANES], tab[g, 0:1], tab[g, 1:2])
            sb, sbs = _cmul_step(sb, sbs, vc[jb:jb + 1, LANES:], vcs[jb:jb + 1, LANES:], tab[g, 4:5], tab[g, 5:6])
        ctx_end.append((sf, sfs, sb, sbs))
        sp_ctx.append(jnp.concatenate([jnp.concatenate(rows_f, axis=0), jnp.concatenate(rows_b, axis=0)], axis=1))

    coef = [[jnp.broadcast_to(tab[g, q:q + 1], (SUBLANES, LANES)) for q in (0, 1, 4, 5)] for g in range(gb)]

    def step(k, carry):
        kb = n_seg - 1 - k
        out = []
        for g in range(gb):
            sf, sfs, sb, sbs = carry[g]
            are_f, aim_f, are_b, aim_b = coef[g]
            at_f = pl.ds(k, SUBLANES, stride=pitch)
            at_b = pl.ds(kb, SUBLANES, stride=pitch)
            v_f, v_fs = vf_ref[g, at_f, :], vfs_ref[g, at_f, :]
            v_b, v_bs = vb_ref[g, at_b, :], vbs_ref[g, at_b, :]
            vf_ref[g, at_f, :] = sf
            vb_ref[g, at_b, :] = sb
            sf, sfs = _cmul_step(sf, sfs, v_f, v_fs, are_f, aim_f)
            sb, sbs = _cmul_step(sb, sbs, v_b, v_bs, are_b, aim_b)
            out.append((sf, sfs, sb, sbs))
        return tuple(out)

    zero8 = jnp.zeros((SUBLANES, LANES), F32)
    ends = lax.fori_loop(0, n_seg, step, tuple((zero8, zero8, zero8, zero8) for _ in range(gb)))

    for g in range(gb):
        ef, efs, eb, ebs = ends[g]
        sf, sfs, sb, sbs = ctx_end[g]
        ent_f, ent_b = [], [None] * SUBLANES
        for r in range(SUBLANES):
            rb = SUBLANES - 1 - r
            ent_f.append((sf, sfs))
            ent_b[rb] = (sb, sbs)
            sf, sfs = _cmul_step(sf, sfs, ef[r:r + 1], efs[r:r + 1], tab[g, 2:3], tab[g, 3:4])
            sb, sbs = _cmul_step(sb, sbs, eb[rb:rb + 1], ebs[rb:rb + 1], tab[g, 6:7], tab[g, 7:8])

        segs = []
        for r in range(SUBLANES):
            seg_rows = slice(r * pitch, r * pitch + n_seg)
            f = vf_ref[g, seg_rows, :] + pow_ref[g, 0] * ent_f[r][0] + pow_ref[g, 1] * ent_f[r][1]
            b = vb_ref[g, seg_rows, :] + pow_ref[g, 2] * ent_b[r][0] + pow_ref[g, 3] * ent_b[r][1]
            segs.append(jnp.concatenate([f, b], axis=1))
        pad = jnp.zeros((n_lat0 - n_ctx_chunks, 2 * LANES), F32)
        sprev = jnp.concatenate([sp_ctx[g], pad] + segs, axis=0).astype(BF16)
        y_ref[g] = _dot(ug_ref[g], m_ref[g]) + _dot(sprev, cp_ref[g])


def _s5_apply(ug, m, bp, cp, tab, pw, n_ctx_chunks, n_lat0):
    n_groups, n_rows, cw = ug.shape
    n_seg = pw.shape[-2]
    assert n_rows - n_lat0 == SUBLANES * n_seg and n_lat0 % SUBLANES == 0
    blk = lambda *shape: pl.BlockSpec((S5_GB,) + shape, lambda i: (i,) + (0,) * len(shape))
    seg = pltpu.VMEM((S5_GB, SUBLANES * (n_seg + SUBLANES), LANES), F32)
    return pl.pallas_call(
        functools.partial(_s5_kernel, n_ctx_chunks, n_lat0),
        grid=(n_groups // S5_GB,),
        in_specs=[blk(n_rows, cw), blk(cw, cw), blk(cw, 2 * LANES), blk(2 * LANES, cw), blk(8, LANES),
                  blk(4, n_seg, LANES)],
        out_specs=blk(n_rows, cw),
        out_shape=jax.ShapeDtypeStruct((n_groups, n_rows, cw), F32),
        scratch_shapes=[seg, seg, seg, seg],
        compiler_params=pltpu.CompilerParams(vmem_limit_bytes=VMEM_LIMIT),
        name="s5_scan",
    )(ug, m, bp, cp, tab, pw)


def _mixer_kernel(n_ctx_tiles, ctx_row_len, first_tile, zl_ref, zc_ref, y_ref, mod_ref, gmix_ref, gffn_ref, win_ref, wglu_ref,
                  lng_ref, lnb_ref, wsg_ref, bsg_ref, wconv_ref, wbr_ref, wmg_ref, bmg_ref, wo_ref, wr_ref,
                  br_ref, z1_ref, xs_ref, route_ref, pc_ref, yscr_ref):
    d = D_MODEL
    n_chunks = TM // S5_T
    for q in range(MIX_W // LANES):
        for half in range(S5_T // SUBLANES):
            cols = [y_ref[q * SUBLANES + b, :, half * LANES:(half + 1) * LANES] for b in range(SUBLANES)]
            for a, rows in enumerate(_block_transpose(cols)):
                yscr_ref[q, pl.ds(half * SUBLANES + a, n_chunks, stride=S5_T), :] = rows
    y_tok = jnp.concatenate([yscr_ref[q] for q in range(MIX_W // LANES)], axis=1)
    is_ctx = pl.program_id(0) + first_tile < n_ctx_tiles
    z = jnp.where(is_ctx, zc_ref[...], zl_ref[...])
    mod = mod_ref[...]
    sh1, sc1, ga1 = mod[:, 0:d], mod[:, d:2 * d], mod[:, 2 * d:3 * d]
    sh2, sc2 = mod[:, 3 * d:4 * d], mod[:, 4 * d:5 * d]
    h = _rms(z, gmix_ref[...]) * (1.0 + sc1) + sh1
    hb = h.astype(BF16)
    proj = _dot(hb, win_ref[...])
    su, sv, cgb, cgc, chh = [proj[:, k * MIX_W:(k + 1) * MIX_W] for k in range(5)]

    def merge_term(k, y_k):
        gate = _sigmoid(_dot(hb, wmg_ref[:, k * d:(k + 1) * d]) + bmg_ref[:, k * d:(k + 1) * d])
        return gate * _dot(y_k.astype(BF16), wbr_ref[k])

    yg = jax.nn.gelu(y_tok)
    merged = merge_term(0, yg * _sigmoid(_dot(yg.astype(BF16), wglu_ref[...])))

    gv = jax.nn.gelu(sv)
    mu = jnp.mean(gv, axis=-1, keepdims=True)
    var = jnp.mean(jnp.square(gv - mu), axis=-1, keepdims=True)
    vn = ((gv - mu) * lax.rsqrt(var + NORM_EPS) * lng_ref[...] + lnb_ref[...]).astype(BF16)
    bsg = bsg_ref[...]
    chunks = []
    for n in range(TM // SG_CHUNK):
        rows = slice(n * SG_CHUNK, (n + 1) * SG_CHUNK)
        cols = [_dot(wsg_ref[g], vn[rows, g * SG_GROUP_W:(g + 1) * SG_GROUP_W]) for g in range(SG_GROUPS)]
        chunks.append(jnp.concatenate(cols, axis=1) + bsg)
    merged = merged + merge_term(1, jax.nn.gelu(su) * jnp.concatenate(chunks, axis=0))

    zc = cgc * chh
    row = lax.broadcasted_iota(jnp.int32, (TM, 1), 0)
    pos = jnp.where(is_ctx, row % ctx_row_len, row % GRID_W)
    prev = jnp.where(pos == 0, 0.0, pltpu.roll(zc, 1, 0))
    nxt = jnp.where(pos == jnp.where(is_ctx, ctx_row_len - 1, GRID_W - 1), 0.0, pltpu.roll(zc, TM - 1, 0))
    wc = wconv_ref[...]
    merged = merged + merge_term(2, cgb * (wc[0:1] * prev + wc[1:2] * zc + wc[2:3] * nxt))

    z1 = z + ga1 * _dot(merged.astype(BF16), wo_ref[...])
    z1_ref[...] = z1

    h2 = _rms(z1, gffn_ref[...]) * (1.0 + sc2) + sh2
    h2_hi = h2.astype(BF16)
    h2_lo = (h2 - h2_hi.astype(F32)).astype(BF16)
    nt_dims = (((1,), (1,)), ((), ()))
    logits_t = (lax.dot_general(wr_ref[0], h2_hi, nt_dims, preferred_element_type=F32)
                + lax.dot_general(wr_ref[0], h2_lo, nt_dims, preferred_element_type=F32)
                + lax.dot_general(wr_ref[1], h2_hi, nt_dims, preferred_element_type=F32)) + br_ref[...]
    lg = logits_t[0:ROUTER_ROWS]
    row = lax.broadcasted_iota(jnp.int32, (ROUTER_ROWS, TM), 0)
    big = jnp.int32(LANES)
    neg = jnp.float32(-jnp.inf)
    top = lambda v: jnp.max(v, axis=0, keepdims=True)
    total = lambda v: jnp.sum(v, axis=0, keepdims=True)
    first = lambda m: jnp.min(jnp.where(m, row, big), axis=0, keepdims=True)
    is_grp = (row >= N_EXPERTS) & (row < N_EXPERTS + N_EXP_GROUPS)
    gl = jnp.where(is_grp, lg, neg)
    ge = jnp.exp(gl - top(gl))
    gp = ge / total(ge)
    g_p = top(gp)
    g_idx = first(is_grp & (gp == g_p)) - N_EXPERTS
    in_grp = (row >= g_idx * EXPERTS_PER_GROUP) & (row < (g_idx + 1) * EXPERTS_PER_GROUP)
    el = jnp.where(in_grp, lg, neg)
    ee = jnp.exp(el - top(el))
    ep = ee / total(ee)
    p1 = top(ep)
    i1 = first(in_grp & (ep == p1))
    rest = in_grp & (row != i1)
    ep2 = jnp.where(rest, ep, -1.0)
    p2 = top(ep2)
    i2 = first(rest & (ep2 == p2))
    tot = p1 + p2
    w_1, w_2 = g_p * (p1 / tot), g_p * (p2 / tot)

    oh1, oh2 = row == i1, row == i2
    oh = jnp.where(oh1 | oh2, 1.0, 0.0)
    cnt = jnp.sum(oh, axis=1, keepdims=True)
    pcb = jnp.floor((cnt + (MOE_BLK - 1)) * (1.0 / MOE_BLK))
    pc_ref[...] = jnp.broadcast_to(pcb * MOE_BLK, (ROUTER_ROWS, LANES)).astype(jnp.int32)
    er, ec = (lax.broadcasted_iota(jnp.int32, (ROUTER_ROWS, ROUTER_ROWS), k) for k in (0, 1))
    before = jnp.where(ec < er, 1.0, 0.0).astype(BF16)
    start = _dot(before, jnp.broadcast_to(pcb, (ROUTER_ROWS, LANES)).astype(BF16))[:, 0:1] * MOE_BLK
    tr, tc = (lax.broadcasted_iota(jnp.int32, (TM, TM), k) for k in (0, 1))
    earlier = jnp.where(tr < tc, 1.0, 0.0).astype(BF16)
    slot = start + _dot(oh.astype(BF16), earlier)
    slot1 = total(jnp.where(oh1, slot, 0.0))
    slot2 = total(jnp.where(oh2, slot, 0.0))

    def token_rows(vals):
        r8 = lax.broadcasted_iota(jnp.int32, (SUBLANES, TM), 0)
        rows8 = jnp.zeros((SUBLANES, TM), F32)
        for k, v in enumerate(vals):
            rows8 = jnp.where(r8 == k, v, rows8)
        return jnp.concatenate([rows8, jnp.zeros((LANES - SUBLANES, TM), F32)], axis=0).T

    route_ref[...] = token_rows([slot1, slot2])
    sl = xs_ref.shape[0]
    slot_row = lax.broadcasted_iota(jnp.int32, (sl, TM), 0).astype(F32)
    perm1 = jnp.where(slot_row == slot1, 1.0, 0.0).astype(BF16)
    perm2 = jnp.where(slot_row == slot2, 1.0, 0.0).astype(BF16)
    xs_ref[:, 0:d] = _dot(perm1 + perm2, h2_hi).astype(BF16)

    def weight_lanes(w):
        hi = w.astype(BF16).astype(F32)
        return token_rows([hi, w - hi]).astype(BF16)

    xs_ref[:, d:] = (_dot(perm1, weight_lanes(w_1)) + _dot(perm2, weight_lanes(w_2))).astype(BF16)


def _mixer(z_lat, z_ctx, lat_off, n, yg, mod, p, n_ctx_tiles, ctx_row_len, first_tile):
    d = D_MODEL
    nt = n // TM - first_tile
    tile = lambda w: pl.BlockSpec((TM, w), lambda i: (i, 0))
    return pl.pallas_call(
        functools.partial(_mixer_kernel, n_ctx_tiles, ctx_row_len, first_tile),
        grid=(nt,),
        in_specs=[
            *_token_specs(n_ctx_tiles, lat_off, first_tile),
            pl.BlockSpec((S5_GROUPS, TM // S5_T, S5_CW), lambda i: (0, i + first_tile, 0)),
            _mod_spec(n_ctx_tiles, first_tile),
            _const_spec((1, d)), _const_spec((1, d)),
            _const_spec((d, 5 * MIX_W)), _const_spec((MIX_W, MIX_W)),
            _const_spec((1, MIX_W)), _const_spec((1, MIX_W)),
            _const_spec((SG_GROUPS, SG_CHUNK, SG_CHUNK)), _const_spec((SG_CHUNK, MIX_W)),
            _const_spec((CONV_K, MIX_W)), _const_spec((N_BRANCH, MIX_W, d)),
            _const_spec((d, N_BRANCH * d)), _const_spec((1, N_BRANCH * d)),
            _const_spec((d, d)), _const_spec((2, LANES, d)), _const_spec((LANES, 1)),
        ],
        out_specs=[tile(d), pl.BlockSpec((None, MOE_SLOTS, MOE_W), lambda i: (i, 0, 0)), tile(LANES),
                   pl.BlockSpec((None, ROUTER_ROWS, LANES), lambda i: (i, 0, 0))],
        out_shape=[
            jax.ShapeDtypeStruct((nt * TM, d), F32),
            jax.ShapeDtypeStruct((nt, MOE_SLOTS, MOE_W), BF16),
            jax.ShapeDtypeStruct((nt * TM, LANES), F32),
            jax.ShapeDtypeStruct((nt, ROUTER_ROWS, LANES), jnp.int32),
        ],
        scratch_shapes=[pltpu.VMEM((MIX_W // LANES, TM, LANES), F32)],
        compiler_params=pltpu.CompilerParams(vmem_limit_bytes=VMEM_LIMIT),
        name="mixer",
    )(z_lat, z_ctx, yg, mod, p["g_mix"], p["g_ffn"], p["w_in5"], p["w_glu"], p["sg_ln_g"], p["sg_ln_b"],
      p["w_sg"], p["b_sg_full"], p["w_conv"], p["w_branch"], p["w_merge"], p["b_merge"], p["w_o"],
      p["w_router"], p["b_router"])


def _plan_kernel(nt, n_work, pc_ref, src_ref, exp_ref, run_ref):
    bpt = MOE_SLOTS // MOE_BLK

    def clear(i, c):
        run_ref[i] = 0
        return c

    lax.fori_loop(0, nt, clear, 0)

    def no_block(b, c):
        src_ref[b] = -1
        return c

    def expert(e, g0):
        def tile(i, pos):
            nb = pc_ref[i, e] // MOE_BLK
            s0 = i * bpt + run_ref[i] // MOE_BLK
            base = g0 * MOE_NB + pos

            def blk(b, c):
                src_ref[base + b] = s0 + b
                return c

            for b in range(PLAN_SLACK):
                src_ref[base + b] = s0 + b
            lax.fori_loop(PLAN_SLACK, nb, blk, 0)
            run_ref[i] = run_ref[i] + pc_ref[i, e]
            return pos + nb

        def tiles(j, pos):
            for k in range(PLAN_UNROLL):
                pos = tile(j * PLAN_UNROLL + k, pos)
            return pos

        pos = lax.fori_loop(0, nt // PLAN_UNROLL, tiles, 0)
        for i in range(nt - nt % PLAN_UNROLL, nt):
            pos = tile(i, pos)
        n_tiles = (pos + MOE_NB - 1) // MOE_NB
        lax.fori_loop(g0 * MOE_NB + pos, (g0 + n_tiles) * MOE_NB, no_block, 0)

        def owner(t, c):
            exp_ref[g0 + t] = e
            return c

        lax.fori_loop(0, n_tiles, owner, 0)
        return g0 + n_tiles

    g_end = lax.fori_loop(0, N_EXPERTS, expert, 0)
    lax.fori_loop(g_end * MOE_NB, n_work * MOE_NB + PLAN_SLACK, no_block, 0)

    def idle(g, c):
        exp_ref[g] = N_EXPERTS - 1
        return c

    lax.fori_loop(g_end, n_work, idle, 0)


def _moe_plan(pc, n_work):
    nt = pc.shape[0]
    smem = pl.BlockSpec(memory_space=pltpu.SMEM)
    return pl.pallas_call(
        functools.partial(_plan_kernel, nt, n_work),
        in_specs=[smem],
        out_specs=[smem, smem],
        out_shape=[jax.ShapeDtypeStruct((n_work * MOE_NB + PLAN_SLACK,), jnp.int32),
                   jax.ShapeDtypeStruct((n_work,), jnp.int32)],
        scratch_shapes=[pltpu.SMEM((nt,), jnp.int32)],
        name="moe_plan",
    )(pc)


def _expert_kernel(src_ref, exp_ref, xs_hbm, w1_ref, w3_ref, w2_ref, yt_hbm, lhs_ref, out_ref, wb1_ref, wb3_ref,
                   wb2_ref, sem_in, sem_out):
    g = pl.program_id(0)
    n_work = pl.num_programs(0)

    def first(step):
        return src_ref[step * MOE_NB]

    def last(step):
        return src_ref[step * MOE_NB + MOE_NB - 1]

    def gather_copy(step, b):
        return pltpu.make_async_copy(xs_hbm.at[src_ref[step * MOE_NB + b]], lhs_ref.at[step % 2, b],
                                     sem_in.at[step % 2])

    def scatter_copy(step, b):
        return pltpu.make_async_copy(out_ref.at[step % 2, b], yt_hbm.at[src_ref[step * MOE_NB + b]],
                                     sem_out.at[step % 2])

    def gather_all(step):
        return pltpu.make_async_copy(xs_hbm.at[pl.ds(0, MOE_NB)], lhs_ref.at[step % 2], sem_in.at[step % 2])

    def scatter_all(step):
        return pltpu.make_async_copy(out_ref.at[step % 2], yt_hbm.at[pl.ds(0, MOE_NB)], sem_out.at[step % 2])

    def per_block(step, fn):
        def body(b, c):
            @pl.when(src_ref[step * MOE_NB + b] >= 0)
            def _():
                fn(b)
            return c
        lax.fori_loop(0, MOE_NB, body, 0)

    def start(step, copy):
        @pl.when(last(step) >= 0)
        def _():
            for b in range(MOE_NB):
                copy(step, b).start()

        @pl.when((last(step) < 0) & (first(step) >= 0))
        def _():
            per_block(step, lambda b: copy(step, b).start())

    def wait(step, copy, copy_all):
        @pl.when(last(step) >= 0)
        def _():
            copy_all(step).wait()

        @pl.when((last(step) < 0) & (first(step) >= 0))
        def _():
            per_block(step, lambda b: copy(step, b).wait())

    @pl.when(g == 0)
    def _():
        start(g, gather_copy)

    @pl.when(g + 1 < n_work)
    def _():
        start(g + 1, gather_copy)

    wait(g, gather_copy, gather_all)

    @pl.when(g >= 2)
    def _():
        wait(g - 2, scatter_copy, scatter_all)

    slot = g % 2

    @pl.when(first(g) >= 0)
    def _():
        @pl.when((g == 0) | (exp_ref[g] != exp_ref[jnp.maximum(g - 1, 0)]))
        def _():
            wb1_ref[...] = w1_ref[...].astype(BF16)
            wb3_ref[...] = w3_ref[...].astype(BF16)
            wb2_ref[...] = w2_ref[...].astype(BF16)

        @pl.when(last(g) < 0)
        def _():
            def zero_missing(b, c):
                @pl.when(src_ref[g * MOE_NB + b] < 0)
                def _():
                    lhs_ref[slot, b] = jnp.zeros((MOE_BLK, MOE_W), BF16)
                return c
            lax.fori_loop(0, MOE_NB, zero_missing, 0)

        rows = lhs_ref[slot].reshape(MOE_TE, MOE_W)
        x = rows[:, :D_MODEL]
        w_lanes = rows[:, D_MODEL:].astype(F32)
        w_row = w_lanes[:, 0:1] + w_lanes[:, 1:2]
        a = _dot(x, wb1_ref[...])
        act = a * _sigmoid(a) * _dot(x, wb3_ref[...])
        y = (_dot(act.astype(BF16), wb2_ref[...]) * w_row).astype(BF16)
        out_ref[slot] = jnp.concatenate([y, jnp.zeros((MOE_TE, LANES), BF16)], axis=1).reshape(MOE_NB, MOE_BLK, MOE_W)

    start(g, scatter_copy)

    @pl.when(g == n_work - 1)
    def _():
        @pl.when(g >= 1)
        def _():
            wait(g - 1, scatter_copy, scatter_all)
        wait(g, scatter_copy, scatter_all)


def _moe_experts(xs, src, owner, w1, w3, w2, layer):
    nt, sl, width = xs.shape
    d = D_MODEL
    n_work = owner.shape[0]
    xs_blocks = xs.reshape(nt * sl // MOE_BLK, MOE_BLK, width)
    any_spec = pl.BlockSpec(memory_space=pl.ANY)
    by_owner = lambda *shape: pl.BlockSpec((None, None) + shape, lambda g, src, own: (layer, own[g], 0, 0))
    buf = pltpu.VMEM((2, MOE_NB, MOE_BLK, width), BF16)
    yt = pl.pallas_call(
        _expert_kernel,
        grid_spec=pltpu.PrefetchScalarGridSpec(
            num_scalar_prefetch=2,
            grid=(n_work,),
            in_specs=[any_spec, by_owner(d, D_FF_EXPERT), by_owner(d, D_FF_EXPERT), by_owner(D_FF_EXPERT, d)],
            out_specs=any_spec,
            scratch_shapes=[
                buf, buf,
                pltpu.VMEM((d, D_FF_EXPERT), BF16), pltpu.VMEM((d, D_FF_EXPERT), BF16),
                pltpu.VMEM((D_FF_EXPERT, d), BF16),
                pltpu.SemaphoreType.DMA((2,)), pltpu.SemaphoreType.DMA((2,)),
            ],
        ),
        out_shape=jax.ShapeDtypeStruct(xs_blocks.shape, BF16),
        input_output_aliases={2: 0},
        compiler_params=pltpu.CompilerParams(dimension_semantics=("arbitrary",), vmem_limit_bytes=VMEM_LIMIT),
        name="moe_experts",
    )(src, owner, xs_blocks, w1, w3, w2)
    return yt.reshape(nt, sl, width)


def _combine_kernel(final, yt_ref, route_ref, z1_ref, mod_ref, g_ref, o_ref):
    route = route_ref[...]
    slot1, slot2 = route[:, 0:1], route[:, 1:2]
    slot_lane = lax.broadcasted_iota(jnp.int32, (TM, MOE_SLOTS), 1).astype(F32)
    both = jnp.where((slot_lane == slot1) | (slot_lane == slot2), 1.0, 0.0).astype(BF16)
    moe = _dot(both, yt_ref[:, 0:D_MODEL])
    ga2 = mod_ref[...][:, 5 * D_MODEL:6 * D_MODEL]
    z2 = z1_ref[...] + ga2 * moe
    o_ref[...] = _rms(z2, g_ref[...]) if final else z2


def _combine(yt, route, z1, mod, g_final, final, n_ctx_tiles):
    n, d = z1.shape
    tile = lambda w: pl.BlockSpec((TM, w), lambda i: (i, 0))
    return pl.pallas_call(
        functools.partial(_combine_kernel, final),
        grid=(n // TM,),
        in_specs=[pl.BlockSpec((None, MOE_SLOTS, MOE_W), lambda i: (i, 0, 0)), tile(LANES), tile(d),
                  _mod_spec(n_ctx_tiles), _const_spec((1, d))],
        out_specs=tile(d),
        out_shape=jax.ShapeDtypeStruct((n, d), F32),
        compiler_params=pltpu.CompilerParams(vmem_limit_bytes=VMEM_LIMIT),
        name="combine",
    )(yt, route, z1, mod, g_final)


def kernel(x, c, ctx, c_ctx, w_ada, b_ada, g_mix, g_ffn, w_in, lam_re, lam_im, log_dt, b_re, b_im, c_re, c_im, d_skip, w_glu, sg_ln_g, sg_ln_b, w_sg, b_sg, w_conv, w_branch, w_merge, b_merge, w_o, w_grp, b_grp, w_exp, b_exp, w1, w3, w2, g_final):
    depth = w_ada.shape[0]
    bsz, seq, d = x.shape
    n_ctx = ctx.shape[1]
    ctx_rows = -(-n_ctx // CTX_ALIGN) * CTX_ALIGN
    n = ctx_rows + seq
    assert bsz == 1 and d == D_MODEL
    assert ctx_rows % TM == 0 and seq % TM == 0 and (TM % n_ctx == 0 or n_ctx % TM == 0)
    assert n_ctx % S5_T == 0 and seq % (S5_T * SUBLANES * SUBLANES) == 0
    n_ctx_tiles = ctx_rows // TM
    n_seg_chunks = seq // (S5_T * SUBLANES)
    nt = n // TM

    z_lat, z_ctx, lat_off = x[0], jnp.pad(ctx[0], ((0, ctx_rows - n_ctx), (0, 0))), n_ctx_tiles
    cc = jnp.zeros((8, d), F32).at[0].set(c_ctx).at[1].set(c[0])
    mods = _ada(cc, w_ada, b_ada)[:, :2].reshape(depth, 2, 1, 6 * d)
    s5_m, s5_bp, s5_cp, s5_tab, s5_pow = _s5_prep(lam_re, lam_im, log_dt, b_re, b_im, c_re, c_im, d_skip, n_seg_chunks)

    for l in range(depth):
        pad = LANES - N_EXPERTS - N_EXP_GROUPS
        p = dict(
            g_mix=g_mix[l].reshape(1, d), g_ffn=g_ffn[l].reshape(1, d),
            w_in5=w_in[l][:, MIX_W:].astype(BF16), w_glu=w_glu[l].astype(BF16),
            sg_ln_g=sg_ln_g[l].reshape(1, MIX_W), sg_ln_b=sg_ln_b[l].reshape(1, MIX_W),
            w_sg=w_sg[l].astype(BF16),
            b_sg_full=jnp.repeat(b_sg[l].T, SG_GROUP_W, axis=1),
            w_conv=w_conv[l], w_branch=w_branch[l].astype(BF16), w_merge=w_merge[l].astype(BF16),
            b_merge=b_merge[l].reshape(1, N_BRANCH * d), w_o=w_o[l].astype(BF16),
            w_router=_split_bf16(jnp.pad(jnp.concatenate([w_exp[l], w_grp[l]], axis=1).T, ((0, pad), (0, 0)))),
            b_router=jnp.pad(jnp.concatenate([b_exp[l], b_grp[l]]), (0, pad)).reshape(LANES, 1),
        )
        mod = mods[l]

        last = l == depth - 1
        first_tile = n_ctx_tiles if last else 0
        ug = _u_proj(z_lat, z_ctx, lat_off, n, mod, p["g_mix"], w_in[l][:, :MIX_W].astype(BF16), n_ctx_tiles)
        yg = _s5_apply(ug, s5_m[l], s5_bp[l], s5_cp[l], s5_tab[l], s5_pow[l], n_ctx // S5_T, ctx_rows // S5_T)
        z1, xs, route, pc = _mixer(z_lat, z_ctx, lat_off, n, yg, mod, p, n_ctx_tiles, min(n_ctx, TM), first_tile)
        n_work = (nt - first_tile) * MOE_SLOTS // MOE_TE + N_EXPERTS
        src, owner = _moe_plan(pc[:, :N_EXPERTS, 0], n_work)
        yt = _moe_experts(xs, src, owner, w1, w3, w2, l)
        z = _combine(yt, route, z1, mod, g_final.reshape(1, d) if last else p["g_ffn"], last,
                     n_ctx_tiles - first_tile)
        z_lat, z_ctx, lat_off = z, z, 0

    return z.reshape(bsz, seq, d)
```

```python
import functools

import jax
import jax.numpy as jnp
from jax import lax
from jax.experimental import pallas as pl
from jax.experimental.pallas import tpu as pltpu

F32 = jnp.float32
BF16 = jnp.bfloat16
HIGHEST = lax.Precision.HIGHEST

D_MODEL = 1024
GRID_W = 64
MIX_W = D_MODEL // 2
N_BRANCH = 3
S5_GROUP = 16
S5_GROUPS = MIX_W // S5_GROUP
S5_STATE = 64
SG_CHUNK = 128
SG_GROUPS = 4
SG_GROUP_W = MIX_W // SG_GROUPS
CONV_K = 3
N_EXP_GROUPS = 4
EXPERTS_PER_GROUP = 8
N_EXPERTS = N_EXP_GROUPS * EXPERTS_PER_GROUP
D_FF_EXPERT = D_MODEL // 4
NORM_EPS = 1e-6

LANES = 128
SUBLANES = 8
S5_T = 16
S5_CW = S5_T * S5_GROUP
S5_GB = 4
CTX_ALIGN = 512
TM = 512
MOE_BLK = 16
MOE_SLOTS = 2 * TM + N_EXPERTS * MOE_BLK
MOE_TE = 512
MOE_NB = MOE_TE // MOE_BLK
ROUTER_ROWS = 40
MOE_W = D_MODEL + LANES
PLAN_SLACK = 4
PLAN_UNROLL = 3
VMEM_LIMIT = 56 * 1024 * 1024


def _dot(a, b):
    return jnp.dot(a, b, preferred_element_type=F32)


def _dot_hi(a, b):
    return jnp.dot(a, b, preferred_element_type=F32, precision=HIGHEST)


def _sigmoid(x):
    return 0.5 * jnp.tanh(0.5 * x) + 0.5


def _split_bf16(w):
    hi = w.astype(BF16)
    return jnp.stack([hi, (w - hi.astype(F32)).astype(BF16)])


def _rms(x, g):
    return x * lax.rsqrt(jnp.mean(x * x, axis=-1, keepdims=True) + NORM_EPS) * g


def _block_transpose(xs):
    nblk = LANES // S5_GROUP
    assert len(xs) == nblk == 8
    blk = lax.broadcasted_iota(jnp.int32, xs[0].shape, 1) // S5_GROUP
    for k in range(3):
        dist = 1 << k
        hi_half = (blk & dist) != 0
        out = list(xs)
        for a in range(nblk):
            if a & dist == 0:
                p = a + dist
                out[a] = jnp.where(hi_half, pltpu.roll(xs[p], dist * S5_GROUP, 1), xs[a])
                out[p] = jnp.where(hi_half, xs[p], pltpu.roll(xs[a], LANES - dist * S5_GROUP, 1))
        xs = out
    return xs


def _const_spec(shape):
    nd = len(shape)
    return pl.BlockSpec(shape, lambda *_: (0,) * nd, pipeline_mode=pl.Buffered(1))


def _ada_kernel(cc_ref, w_ref, b_ref, o_ref):
    cc = cc_ref[...]
    act = cc * _sigmoid(cc)
    hi = act.astype(BF16)
    lo = (act - hi.astype(F32)).astype(BF16)
    w = w_ref[...].astype(BF16)
    o_ref[...] = _dot(hi, w) + _dot(lo, w) + b_ref[...]


def _ada(cc, w_ada, b_ada):
    depth = w_ada.shape[0]
    nblk = 6
    return pl.pallas_call(
        _ada_kernel,
        grid=(depth, nblk),
        in_specs=[
            pl.BlockSpec((8, D_MODEL), lambda l, j: (0, 0)),
            pl.BlockSpec((None, D_MODEL, D_MODEL), lambda l, j: (l, 0, j)),
            pl.BlockSpec((None, 1, D_MODEL), lambda l, j: (l, 0, j)),
        ],
        out_specs=pl.BlockSpec((None, 8, D_MODEL), lambda l, j: (l, 0, j)),
        out_shape=jax.ShapeDtypeStruct((depth, 8, 6 * D_MODEL), F32),
        name="ada",
    )(cc, w_ada, b_ada.reshape(depth, 1, 6 * D_MODEL))


def _mod_spec(n_ctx_tiles, first_tile=0):
    return pl.BlockSpec((None, 1, 6 * D_MODEL), lambda i: (jnp.where(i + first_tile < n_ctx_tiles, 0, 1), 0, 0))


def _token_specs(n_ctx_tiles, lat_off, first_tile=0):
    lat = pl.BlockSpec((TM, D_MODEL), lambda i: (jnp.maximum(i + first_tile - lat_off, 0), 0))
    ctx = pl.BlockSpec((TM, D_MODEL), lambda i: (jnp.minimum(i + first_tile, n_ctx_tiles - 1), 0))
    return lat, ctx


def _u_kernel(n_ctx_tiles, zl_ref, zc_ref, mod_ref, g_ref, w_ref, ug_ref, scr_ref):
    z = jnp.where(pl.program_id(0) < n_ctx_tiles, zc_ref[...], zl_ref[...])
    _u_group_major(z, mod_ref[...], g_ref[...], w_ref, ug_ref, scr_ref)


def _u_group_major(z, mod, g, w_ref, ug_ref, scr_ref):
    h = _rms(z, g) * (1.0 + mod[:, D_MODEL:2 * D_MODEL]) + mod[:, 0:D_MODEL]
    u = _dot(h.astype(BF16), w_ref[...])
    n_chunks = TM // S5_T
    for q in range(MIX_W // LANES):
        scr_ref[q] = u[:, q * LANES:(q + 1) * LANES]
    for q in range(MIX_W // LANES):
        for half in range(S5_T // SUBLANES):
            rows = [scr_ref[q, pl.ds(half * SUBLANES + a, n_chunks, stride=S5_T), :] for a in range(SUBLANES)]
            for b, y in enumerate(_block_transpose(rows)):
                ug_ref[q * SUBLANES + b, :, half * LANES:(half + 1) * LANES] = y.astype(BF16)


def _u_proj(z_lat, z_ctx, lat_off, n, mod, g_mix, w_u, n_ctx_tiles):
    return pl.pallas_call(
        functools.partial(_u_kernel, n_ctx_tiles),
        grid=(n // TM,),
        in_specs=[
            *_token_specs(n_ctx_tiles, lat_off),
            _mod_spec(n_ctx_tiles),
            _const_spec((1, D_MODEL)),
            _const_spec((D_MODEL, MIX_W)),
        ],
        out_specs=pl.BlockSpec((S5_GROUPS, TM // S5_T, S5_CW), lambda i: (0, i, 0)),
        out_shape=jax.ShapeDtypeStruct((S5_GROUPS, n // S5_T, S5_CW), BF16),
        scratch_shapes=[pltpu.VMEM((MIX_W // LANES, TM, LANES), F32)],
        name="u_proj",
    )(z_lat, z_ctx, mod, g_mix, w_u)


def _s5_prep_kernel(n_seg_chunks, par_ref, bt_ref, ct_ref, dsk_ref, m_ref, bp_ref, cp_ref, tab_ref, pow_ref):
    t = S5_T
    par = par_ref[...]
    lo = lax.broadcasted_iota(jnp.int32, (1, LANES), 1) < S5_STATE
    sgn = jnp.where(lo, -1.0, 1.0)

    def rows(n):
        return lax.broadcasted_iota(jnp.int32, (n, 1), 0).astype(F32)

    def expand(pw_rows, vec_rows):
        pr, pi = pw_rows
        vr, vi = vec_rows
        pr, pi = pr[:, None, :], pi[:, None, :]
        vr, vi = vr[None, :, :], vi[None, :, :]
        re = (pr * vr - pi * vi).reshape(-1, LANES)
        im = (pr * vi + pi * vr).reshape(-1, LANES)
        return re, im

    mt = None
    bp_cols, cp_rows, tab_rows, pow_rows = [], [], [], []
    for d in range(2):
        lre, lim = par[2 * d:2 * d + 1], par[2 * d + 1:2 * d + 2]
        dt = jnp.exp(par[4 + d:5 + d])
        xr, xi = lre * dt, lim * dt

        def cpow(m):
            mag = jnp.exp(m * xr)
            return mag * jnp.cos(m * xi), mag * jnp.sin(m * xi)

        ar, ai = cpow(1.0)
        den = lre * lre + lim * lim
        qr = ((ar - 1.0) * lre + ai * lim) / den
        qi = (ai * lre - (ar - 1.0) * lim) / den
        br, bi = bt_ref[2 * d], bt_ref[2 * d + 1]
        bb = (qr * br - qi * bi, qr * bi + qi * br)
        cc = (ct_ref[2 * d], ct_ref[2 * d + 1])

        inj_pow = cpow(t - 1.0 - rows(t)) if d == 0 else cpow(rows(t))
        re, im = expand(inj_pow, bb)
        bp_cols.append(jnp.where(lo, re, im))

        out_pow = cpow(rows(t) + 1.0) if d == 0 else cpow(t - rows(t))
        re, im = expand(out_pow, cc)
        cp_rows.append(jnp.where(lo, re, -im).T)

        lag_pow = cpow(rows(t)) if d == 0 else cpow(t - 1.0 - rows(t))
        re, im = expand(lag_pow, cc)
        ctab = jnp.where(lo, re, -im)
        bcat = jnp.broadcast_to(jnp.where(lo, bb[0], bb[1])[None], (t, S5_GROUP, LANES)).reshape(-1, LANES)
        k256 = lax.dot_general(ctab, bcat, (((1,), (1,)), ((), ())), precision=HIGHEST,
                               preferred_element_type=F32)
        col_blk = lax.broadcasted_iota(jnp.int32, (S5_CW, S5_CW), 1) // S5_GROUP
        acc = jnp.zeros((S5_CW, S5_CW), F32)
        for r in range(t):
            sh = r * S5_GROUP if d == 0 else (t - 1 - r) * S5_GROUP
            if sh == 0:
                shifted = k256
            elif d == 0:
                shifted = jnp.concatenate([jnp.zeros((sh, S5_CW), F32), k256[:S5_CW - sh]], axis=0)
            else:
                shifted = jnp.concatenate([k256[sh:], jnp.zeros((sh, S5_CW), F32)], axis=0)
            acc = acc + jnp.where(col_blk == r, shifted, 0.0)
        mt = acc if mt is None else mt + acc

        for m in (float(t), float(t * n_seg_chunks)):
            re, im = cpow(m)
            tab_rows += [re, sgn * im]
        blk, j = rows(n_seg_chunks // SUBLANES), rows(SUBLANES)
        if d == 0:
            re, im = expand(cpow(t * SUBLANES * blk), cpow(t * j))
        else:
            re, im = expand(cpow(t * SUBLANES * (n_seg_chunks // SUBLANES - 1.0 - blk)), cpow(t * (SUBLANES - 1.0 - j)))
        pow_rows += [re, sgn * im]

    eye = (lax.broadcasted_iota(jnp.int32, (S5_CW, S5_CW), 0) == lax.broadcasted_iota(jnp.int32, (S5_CW, S5_CW), 1))
    mt = mt + jnp.where(eye, dsk_ref[...], 0.0)
    m_ref[...] = mt.T.astype(BF16)
    bp_ref[...] = jnp.concatenate(bp_cols, axis=1).astype(BF16)
    cp_ref[...] = jnp.concatenate(cp_rows, axis=0).astype(BF16)
    tab_ref[...] = jnp.concatenate(tab_rows, axis=0)
    for q in range(4):
        pow_ref[q] = pow_rows[q]


def _s5_prep(lam_re, lam_im, log_dt, b_re, b_im, c_re, c_im, d_skip, n_seg_chunks):
    depth = lam_re.shape[0]
    g, p_, h = S5_GROUPS, S5_STATE, S5_GROUP
    dup = lambda a: jnp.concatenate([a, a], axis=-1)
    dirs_last = lambda a: jnp.moveaxis(a, 1, 2)
    lam = jnp.stack([lam_re, lam_im], axis=3)
    lam = dirs_last(lam).reshape(depth, g, 4, p_)
    ldt = jnp.broadcast_to(dirs_last(log_dt[..., None]), (depth, g, 2, p_))
    par = dup(jnp.concatenate([lam, ldt, jnp.zeros((depth, g, 2, p_), F32)], axis=2))
    bt = jnp.stack([b_re, b_im], axis=3)
    bt = dup(jnp.swapaxes(dirs_last(bt), -1, -2).reshape(depth, g, 4, h, p_))
    ct = dup(dirs_last(jnp.stack([c_re, c_im], axis=3)).reshape(depth, g, 4, h, p_))
    dsk = jnp.tile(d_skip.reshape(depth, g, 1, h), (1, 1, 1, S5_T))

    blk = lambda *shape: pl.BlockSpec((None, None) + shape, lambda l, j: (l, j) + (0,) * len(shape))
    return pl.pallas_call(
        functools.partial(_s5_prep_kernel, n_seg_chunks),
        grid=(depth, g),
        in_specs=[blk(8, LANES), blk(4, h, LANES), blk(4, h, LANES), blk(1, S5_CW)],
        out_specs=[blk(S5_CW, S5_CW), blk(S5_CW, 2 * LANES), blk(2 * LANES, S5_CW), blk(8, LANES),
                   blk(4, n_seg_chunks, LANES)],
        out_shape=[
            jax.ShapeDtypeStruct((depth, g, S5_CW, S5_CW), BF16),
            jax.ShapeDtypeStruct((depth, g, S5_CW, 2 * LANES), BF16),
            jax.ShapeDtypeStruct((depth, g, 2 * LANES, S5_CW), BF16),
            jax.ShapeDtypeStruct((depth, g, 8, LANES), F32),
            jax.ShapeDtypeStruct((depth, g, 4, n_seg_chunks, LANES), F32),
        ],
        name="s5_prep",
    )(par, bt, ct, dsk)


def _cmul_step(s, s_sw, v, v_sw, re, ims):
    return re * s + ims * s_sw + v, re * s_sw - ims * s + v_sw


def _s5_kernel(n_ctx_chunks, n_lat0, ug_ref, m_ref, bp_ref, cp_ref, tab_ref, pow_ref, y_ref,
               vf_ref, vfs_ref, vb_ref, vbs_ref):
    gb, n_rows, _ = ug_ref.shape
    n_seg = (n_rows - n_lat0) // SUBLANES
    pitch = vf_ref.shape[1] // SUBLANES
    half = S5_STATE

    def put_segments(ref, g, val):
        for r in range(SUBLANES):
            ref[g, r * pitch:r * pitch + n_seg, :] = val[r * n_seg:(r + 1) * n_seg]

    tab = tab_ref[...]
    ctx_end = []
    sp_ctx = []
    for g in range(gb):
        v = _dot(ug_ref[g], bp_ref[g])
        vl = v[n_lat0:]
        put_segments(vf_ref, g, vl[:, :LANES])
        put_segments(vb_ref, g, vl[:, LANES:])
        put_segments(vfs_ref, g, pltpu.roll(vl[:, :LANES], half, 1))
        put_segments(vbs_ref, g, pltpu.roll(vl[:, LANES:], half, 1))

        vc = v[:n_ctx_chunks]
        vcs = jnp.concatenate([pltpu.roll(vc[:, :LANES], half, 1), pltpu.roll(vc[:, LANES:], half, 1)], axis=1)
        zero = jnp.zeros((1, LANES), F32)
        sf, sfs, sb, sbs = zero, zero, zero, zero
        rows_f, rows_b = [], [None] * n_ctx_chunks
        for j in range(n_ctx_chunks):
            jb = n_ctx_chunks - 1 - j
            rows_f.append(sf)
            rows_b[jb] = sb
            sf, sfs = _cmul_step(sf, sfs, vc[j:j + 1, :LANES], vcs[j:j + 1, :LANES], tab[g, 0:1], tab[g, 1:2])
            sb, sbs = _cmul_step(sb, sbs, vc[jb:jb + 1, LANES:], vcs[jb:jb + 1, LANES:], tab[g, 4:5], tab[g, 5:6])
        ctx_end.append((sf, sfs, sb, sbs))
        sp_ctx.append(jnp.concatenate([jnp.concatenate(rows_f, axis=0), jnp.concatenate(rows_b, axis=0)], axis=1))

    coef = [[jnp.broadcast_to(tab[g, q:q + 1], (SUBLANES, LANES)) for q in (0, 1, 4, 5)] for g in range(gb)]

    def step(k, carry):
        kb = n_seg - 1 - k
        out = []
        for g in range(gb):
            sf, sfs, sb, sbs = carry[g]
            are_f, aim_f, are_b, aim_b = coef[g]
            at_f = pl.ds(k, SUBLANES, stride=pitch)
            at_b = pl.ds(kb, SUBLANES, stride=pitch)
            v_f, v_fs = vf_ref[g, at_f, :], vfs_ref[g, at_f, :]
            v_b, v_bs = vb_ref[g, at_b, :], vbs_ref[g, at_b, :]
            vf_ref[g, at_f, :] = sf
            vb_ref[g, at_b, :] = sb
            sf, sfs = _cmul_step(sf, sfs, v_f, v_fs, are_f, aim_f)
            sb, sbs = _cmul_step(sb, sbs, v_b, v_bs, are_b, aim_b)
            out.append((sf, sfs, sb, sbs))
        return tuple(out)

    zero8 = jnp.zeros((SUBLANES, LANES), F32)
    ends = lax.fori_loop(0, n_seg, step, tuple((zero8, zero8, zero8, zero8) for _ in range(gb)))

    for g in range(gb):
        ef, efs, eb, ebs = ends[g]
        sf, sfs, sb, sbs = ctx_end[g]
        ent_f, ent_b = [], [None] * SUBLANES
        for r in range(SUBLANES):
            rb = SUBLANES - 1 - r
            ent_f.append((sf, sfs))
            ent_b[rb] = (sb, sbs)
            sf, sfs = _cmul_step(sf, sfs, ef[r:r + 1], efs[r:r + 1], tab[g, 2:3], tab[g, 3:4])
            sb, sbs = _cmul_step(sb, sbs, eb[rb:rb + 1], ebs[rb:rb + 1], tab[g, 6:7], tab[g, 7:8])

        segs = []
        for r in range(SUBLANES):
            seg_rows = slice(r * pitch, r * pitch + n_seg)
            f = vf_ref[g, seg_rows, :] + pow_ref[g, 0] * ent_f[r][0] + pow_ref[g, 1] * ent_f[r][1]
            b = vb_ref[g, seg_rows, :] + pow_ref[g, 2] * ent_b[r][0] + pow_ref[g, 3] * ent_b[r][1]
            segs.append(jnp.concatenate([f, b], axis=1))
        pad = jnp.zeros((n_lat0 - n_ctx_chunks, 2 * LANES), F32)
        sprev = jnp.concatenate([sp_ctx[g], pad] + segs, axis=0).astype(BF16)
        y_ref[g] = _dot(ug_ref[g], m_ref[g]) + _dot(sprev, cp_ref[g])


def _s5_apply(ug, m, bp, cp, tab, pw, n_ctx_chunks, n_lat0):
    n_groups, n_rows, cw = ug.shape
    n_seg = pw.shape[-2]
    assert n_rows - n_lat0 == SUBLANES * n_seg and n_lat0 % SUBLANES == 0
    blk = lambda *shape: pl.BlockSpec((S5_GB,) + shape, lambda i: (i,) + (0,) * len(shape))
    seg = pltpu.VMEM((S5_GB, SUBLANES * (n_seg + SUBLANES), LANES), F32)
    return pl.pallas_call(
        functools.partial(_s5_kernel, n_ctx_chunks, n_lat0),
        grid=(n_groups // S5_GB,),
        in_specs=[blk(n_rows, cw), blk(cw, cw), blk(cw, 2 * LANES), blk(2 * LANES, cw), blk(8, LANES),
                  blk(4, n_seg, LANES)],
        out_specs=blk(n_rows, cw),
        out_shape=jax.ShapeDtypeStruct((n_groups, n_rows, cw), F32),
        scratch_shapes=[seg, seg, seg, seg],
        compiler_params=pltpu.CompilerParams(vmem_limit_bytes=VMEM_LIMIT),
        name="s5_scan",
    )(ug, m, bp, cp, tab, pw)


def _mixer_kernel(n_ctx_tiles, ctx_row_len, first_tile, zl_ref, zc_ref, y_ref, mod_ref, gmix_ref, gffn_ref, win_ref, wglu_ref,
                  lng_ref, lnb_ref, wsg_ref, bsg_ref, wconv_ref, wbr_ref, wmg_ref, bmg_ref, wo_ref, wr_ref,
                  br_ref, z1_ref, xs_ref, route_ref, pc_ref, yscr_ref):
    d = D_MODEL
    n_chunks = TM // S5_T
    for q in range(MIX_W // LANES):
        for half in range(S5_T // SUBLANES):
            cols = [y_ref[q * SUBLANES + b, :, half * LANES:(half + 1) * LANES] for b in range(SUBLANES)]
            for a, rows in enumerate(_block_transpose(cols)):
                yscr_ref[q, pl.ds(half * SUBLANES + a, n_chunks, stride=S5_T), :] = rows
    y_tok = jnp.concatenate([yscr_ref[q] for q in range(MIX_W // LANES)], axis=1)
    is_ctx = pl.program_id(0) + first_tile < n_ctx_tiles
    z = jnp.where(is_ctx, zc_ref[...], zl_ref[...])
    mod = mod_ref[...]
    sh1, sc1, ga1 = mod[:, 0:d], mod[:, d:2 * d], mod[:, 2 * d:3 * d]
    sh2, sc2 = mod[:, 3 * d:4 * d], mod[:, 4 * d:5 * d]
    h = _rms(z, gmix_ref[...]) * (1.0 + sc1) + sh1
    hb = h.astype(BF16)
    proj = _dot(hb, win_ref[...])
    su, sv, cgb, cgc, chh = [proj[:, k * MIX_W:(k + 1) * MIX_W] for k in range(5)]

    def merge_term(k, y_k):
        gate = _sigmoid(_dot(hb, wmg_ref[:, k * d:(k + 1) * d]) + bmg_ref[:, k * d:(k + 1) * d])
        return gate * _dot(y_k.astype(BF16), wbr_ref[k])

    yg = jax.nn.gelu(y_tok)
    merged = merge_term(0, yg * _sigmoid(_dot(yg.astype(BF16), wglu_ref[...])))

    gv = jax.nn.gelu(sv)
    mu = jnp.mean(gv, axis=-1, keepdims=True)
    var = jnp.mean(jnp.square(gv - mu), axis=-1, keepdims=True)
    vn = ((gv - mu) * lax.rsqrt(var + NORM_EPS) * lng_ref[...] + lnb_ref[...]).astype(BF16)
    bsg = bsg_ref[...]
    chunks = []
    for n in range(TM // SG_CHUNK):
        rows = slice(n * SG_CHUNK, (n + 1) * SG_CHUNK)
        cols = [_dot(wsg_ref[g], vn[rows, g * SG_GROUP_W:(g + 1) * SG_GROUP_W]) for g in range(SG_GROUPS)]
        chunks.append(jnp.concatenate(cols, axis=1) + bsg)
    merged = merged + merge_term(1, jax.nn.gelu(su) * jnp.concatenate(chunks, axis=0))

    zc = cgc * chh
    row = lax.broadcasted_iota(jnp.int32, (TM, 1), 0)
    pos = jnp.where(is_ctx, row % ctx_row_len, row % GRID_W)
    prev = jnp.where(pos == 0, 0.0, pltpu.roll(zc, 1, 0))
    nxt = jnp.where(pos == jnp.where(is_ctx, ctx_row_len - 1, GRID_W - 1), 0.0, pltpu.roll(zc, TM - 1, 0))
    wc = wconv_ref[...]
    merged = merged + merge_term(2, cgb * (wc[0:1] * prev + wc[1:2] * zc + wc[2:3] * nxt))

    z1 = z + ga1 * _dot(merged.astype(BF16), wo_ref[...])
    z1_ref[...] = z1

    h2 = _rms(z1, gffn_ref[...]) * (1.0 + sc2) + sh2
    h2_hi = h2.astype(BF16)
    h2_lo = (h2 - h2_hi.astype(F32)).astype(BF16)
    nt_dims = (((1,), (1,)), ((), ()))
    logits_t = (lax.dot_general(wr_ref[0], h2_hi, nt_dims, preferred_element_type=F32)
                + lax.dot_general(wr_ref[0], h2_lo, nt_dims, preferred_element_type=F32)
                + lax.dot_general(wr_ref[1], h2_hi, nt_dims, preferred_element_type=F32)) + br_ref[...]
    lg = logits_t[0:ROUTER_ROWS]
    row = lax.broadcasted_iota(jnp.int32, (ROUTER_ROWS, TM), 0)
    big = jnp.int32(LANES)
    neg = jnp.float32(-jnp.inf)
    top = lambda v: jnp.max(v, axis=0, keepdims=True)
    total = lambda v: jnp.sum(v, axis=0, keepdims=True)
    first = lambda m: jnp.min(jnp.where(m, row, big), axis=0, keepdims=True)
    is_grp = (row >= N_EXPERTS) & (row < N_EXPERTS + N_EXP_GROUPS)
    gl = jnp.where(is_grp, lg, neg)
    ge = jnp.exp(gl - top(gl))
    gp = ge / total(ge)
    g_p = top(gp)
    g_idx = first(is_grp & (gp == g_p)) - N_EXPERTS
    in_grp = (row >= g_idx * EXPERTS_PER_GROUP) & (row < (g_idx + 1) * EXPERTS_PER_GROUP)
    el = jnp.where(in_grp, lg, neg)
    ee = jnp.exp(el - top(el))
    ep = ee / total(ee)
    p1 = top(ep)
    i1 = first(in_grp & (ep == p1))
    rest = in_grp & (row != i1)
    ep2 = jnp.where(rest, ep, -1.0)
    p2 = top(ep2)
    i2 = first(rest & (ep2 == p2))
    tot = p1 + p2
    w_1, w_2 = g_p * (p1 / tot), g_p * (p2 / tot)

    oh1, oh2 = row == i1, row == i2
    oh = jnp.where(oh1 | oh2, 1.0, 0.0)
    cnt = jnp.sum(oh, axis=1, keepdims=True)
    pcb = jnp.floor((cnt + (MOE_BLK - 1)) * (1.0 / MOE_BLK))
    pc_ref[...] = jnp.broadcast_to(pcb * MOE_BLK, (ROUTER_ROWS, LANES)).astype(jnp.int32)
    er, ec = (lax.broadcasted_iota(jnp.int32, (ROUTER_ROWS, ROUTER_ROWS), k) for k in (0, 1))
    before = jnp.where(ec < er, 1.0, 0.0).astype(BF16)
    start = _dot(before, jnp.broadcast_to(pcb, (ROUTER_ROWS, LANES)).astype(BF16))[:, 0:1] * MOE_BLK
    tr, tc = (lax.broadcasted_iota(jnp.int32, (TM, TM), k) for k in (0, 1))
    earlier = jnp.where(tr < tc, 1.0, 0.0).astype(BF16)
    slot = start + _dot(oh.astype(BF16), earlier)
    slot1 = total(jnp.where(oh1, slot, 0.0))
    slot2 = total(jnp.where(oh2, slot, 0.0))

    def token_rows(vals):
        r8 = lax.broadcasted_iota(jnp.int32, (SUBLANES, TM), 0)
        rows8 = jnp.zeros((SUBLANES, TM), F32)
        for k, v in enumerate(vals):
            rows8 = jnp.where(r8 == k, v, rows8)
        return jnp.concatenate([rows8, jnp.zeros((LANES - SUBLANES, TM), F32)], axis=0).T

    route_ref[...] = token_rows([slot1, slot2])
    sl = xs_ref.shape[0]
    slot_row = lax.broadcasted_iota(jnp.int32, (sl, TM), 0).astype(F32)
    perm1 = jnp.where(slot_row == slot1, 1.0, 0.0).astype(BF16)
    perm2 = jnp.where(slot_row == slot2, 1.0, 0.0).astype(BF16)
    xs_ref[:, 0:d] = _dot(perm1 + perm2, h2_hi).astype(BF16)

    def weight_lanes(w):
        hi = w.astype(BF16).astype(F32)
        return token_rows([hi, w - hi]).astype(BF16)

    xs_ref[:, d:] = (_dot(perm1, weight_lanes(w_1)) + _dot(perm2, weight_lanes(w_2))).astype(BF16)


def _mixer(z_lat, z_ctx, lat_off, n, yg, mod, p, n_ctx_tiles, ctx_row_len, first_tile):
    d = D_MODEL
    nt = n // TM - first_tile
    tile = lambda w: pl.BlockSpec((TM, w), lambda i: (i, 0))
    return pl.pallas_call(
        functools.partial(_mixer_kernel, n_ctx_tiles, ctx_row_len, first_tile),
        grid=(nt,),
        in_specs=[
            *_token_specs(n_ctx_tiles, lat_off, first_tile),
            pl.BlockSpec((S5_GROUPS, TM // S5_T, S5_CW), lambda i: (0, i + first_tile, 0)),
            _mod_spec(n_ctx_tiles, first_tile),
            _const_spec((1, d)), _const_spec((1, d)),
            _const_spec((d, 5 * MIX_W)), _const_spec((MIX_W, MIX_W)),
            _const_spec((1, MIX_W)), _const_spec((1, MIX_W)),
            _const_spec((SG_GROUPS, SG_CHUNK, SG_CHUNK)), _const_spec((SG_CHUNK, MIX_W)),
            _const_spec((CONV_K, MIX_W)), _const_spec((N_BRANCH, MIX_W, d)),
            _const_spec((d, N_BRANCH * d)), _const_spec((1, N_BRANCH * d)),
            _const_spec((d, d)), _const_spec((2, LANES, d)), _const_spec((LANES, 1)),
        ],
        out_specs=[tile(d), pl.BlockSpec((None, MOE_SLOTS, MOE_W), lambda i: (i, 0, 0)), tile(LANES),
                   pl.BlockSpec((None, ROUTER_ROWS, LANES), lambda i: (i, 0, 0))],
        out_shape=[
            jax.ShapeDtypeStruct((nt * TM, d), F32),
            jax.ShapeDtypeStruct((nt, MOE_SLOTS, MOE_W), BF16),
            jax.ShapeDtypeStruct((nt * TM, LANES), F32),
            jax.ShapeDtypeStruct((nt, ROUTER_ROWS, LANES), jnp.int32),
        ],
        scratch_shapes=[pltpu.VMEM((MIX_W // LANES, TM, LANES), F32)],
        compiler_params=pltpu.CompilerParams(vmem_limit_bytes=VMEM_LIMIT),
        name="mixer",
    )(z_lat, z_ctx, yg, mod, p["g_mix"], p["g_ffn"], p["w_in5"], p["w_glu"], p["sg_ln_g"], p["sg_ln_b"],
      p["w_sg"], p["b_sg_full"], p["w_conv"], p["w_branch"], p["w_merge"], p["b_merge"], p["w_o"],
      p["w_router"], p["b_router"])


def _plan_kernel(nt, n_work, pc_ref, src_ref, exp_ref, run_ref):
    bpt = MOE_SLOTS // MOE_BLK

    def clear(i, c):
        run_ref[i] = 0
        return c

    lax.fori_loop(0, nt, clear, 0)

    def no_block(b, c):
        src_ref[b] = -1
        return c

    def expert(e, g0):
        def tile(i, pos):
            nb = pc_ref[i, e] // MOE_BLK
            s0 = i * bpt + run_ref[i] // MOE_BLK
            base = g0 * MOE_NB + pos

            def blk(b, c):
                src_ref[base + b] = s0 + b
                return c

            for b in range(PLAN_SLACK):
                src_ref[base + b] = s0 + b
            lax.fori_loop(PLAN_SLACK, nb, blk, 0)
            run_ref[i] = run_ref[i] + pc_ref[i, e]
            return pos + nb

        def tiles(j, pos):
            for k in range(PLAN_UNROLL):
                pos = tile(j * PLAN_UNROLL + k, pos)
            return pos

        pos = lax.fori_loop(0, nt // PLAN_UNROLL, tiles, 0)
        for i in range(nt - nt % PLAN_UNROLL, nt):
            pos = tile(i, pos)
        n_tiles = (pos + MOE_NB - 1) // MOE_NB
        lax.fori_loop(g0 * MOE_NB + pos, (g0 + n_tiles) * MOE_NB, no_block, 0)

        def owner(t, c):
            exp_ref[g0 + t] = e
            return c

        lax.fori_loop(0, n_tiles, owner, 0)
        return g0 + n_tiles

    g_end = lax.fori_loop(0, N_EXPERTS, expert, 0)
    lax.fori_loop(g_end * MOE_NB, n_work * MOE_NB + PLAN_SLACK, no_block, 0)

    def idle(g, c):
        exp_ref[g] = N_EXPERTS - 1
        return c

    lax.fori_loop(g_end, n_work, idle, 0)


def _moe_plan(pc, n_work):
    nt = pc.shape[0]
    smem = pl.BlockSpec(memory_space=pltpu.SMEM)
    return pl.pallas_call(
        functools.partial(_plan_kernel, nt, n_work),
        in_specs=[smem],
        out_specs=[smem, smem],
        out_shape=[jax.ShapeDtypeStruct((n_work * MOE_NB + PLAN_SLACK,), jnp.int32),
                   jax.ShapeDtypeStruct((n_work,), jnp.int32)],
        scratch_shapes=[pltpu.SMEM((nt,), jnp.int32)],
        name="moe_plan",
    )(pc)


def _expert_kernel(src_ref, exp_ref, xs_hbm, w1_ref, w3_ref, w2_ref, yt_hbm, lhs_ref, out_ref, wb1_ref, wb3_ref,
                   wb2_ref, sem_in, sem_out):
    g = pl.program_id(0)
    n_work = pl.num_programs(0)

    def first(step):
        return src_ref[step * MOE_NB]

    def last(step):
        return src_ref[step * MOE_NB + MOE_NB - 1]

    def gather_copy(step, b):
        return pltpu.make_async_copy(xs_hbm.at[src_ref[step * MOE_NB + b]], lhs_ref.at[step % 2, b],
                                     sem_in.at[step % 2])

    def scatter_copy(step, b):
        return pltpu.make_async_copy(out_ref.at[step % 2, b], yt_hbm.at[src_ref[step * MOE_NB + b]],
                                     sem_out.at[step % 2])

    def gather_all(step):
        return pltpu.make_async_copy(xs_hbm.at[pl.ds(0, MOE_NB)], lhs_ref.at[step % 2], sem_in.at[step % 2])

    def scatter_all(step):
        return pltpu.make_async_copy(out_ref.at[step % 2], yt_hbm.at[pl.ds(0, MOE_NB)], sem_out.at[step % 2])

    def per_block(step, fn):
        def body(b, c):
            @pl.when(src_ref[step * MOE_NB + b] >= 0)
            def _():
                fn(b)
            return c
        lax.fori_loop(0, MOE_NB, body, 0)

    def start(step, copy):
        @pl.when(last(step) >= 0)
        def _():
            for b in range(MOE_NB):
                copy(step, b).start()

        @pl.when((last(step) < 0) & (first(step) >= 0))
        def _():
            per_block(step, lambda b: copy(step, b).start())

    def wait(step, copy, copy_all):
        @pl.when(last(step) >= 0)
        def _():
            copy_all(step).wait()

        @pl.when((last(step) < 0) & (first(step) >= 0))
        def _():
            per_block(step, lambda b: copy(step, b).wait())

    @pl.when(g == 0)
    def _():
        start(g, gather_copy)

    @pl.when(g + 1 < n_work)
    def _():
        start(g + 1, gather_copy)

    wait(g, gather_copy, gather_all)

    @pl.when(g >= 2)
    def _():
        wait(g - 2, scatter_copy, scatter_all)

    slot = g % 2

    @pl.when(first(g) >= 0)
    def _():
        @pl.when((g == 0) | (exp_ref[g] != exp_ref[jnp.maximum(g - 1, 0)]))
        def _():
            wb1_ref[...] = w1_ref[...].astype(BF16)
            wb3_ref[...] = w3_ref[...].astype(BF16)
            wb2_ref[...] = w2_ref[...].astype(BF16)

        @pl.when(last(g) < 0)
        def _():
            def zero_missing(b, c):
                @pl.when(src_ref[g * MOE_NB + b] < 0)
                def _():
                    lhs_ref[slot, b] = jnp.zeros((MOE_BLK, MOE_W), BF16)
                return c
            lax.fori_loop(0, MOE_NB, zero_missing, 0)

        rows = lhs_ref[slot].reshape(MOE_TE, MOE_W)
        x = rows[:, :D_MODEL]
        w_lanes = rows[:, D_MODEL:].astype(F32)
        w_row = w_lanes[:, 0:1] + w_lanes[:, 1:2]
        a = _dot(x, wb1_ref[...])
        act = a * _sigmoid(a) * _dot(x, wb3_ref[...])
        y = (_dot(act.astype(BF16), wb2_ref[...]) * w_row).astype(BF16)
        out_ref[slot] = jnp.concatenate([y, jnp.zeros((MOE_TE, LANES), BF16)], axis=1).reshape(MOE_NB, MOE_BLK, MOE_W)

    start(g, scatter_copy)

    @pl.when(g == n_work - 1)
    def _():
        @pl.when(g >= 1)
        def _():
            wait(g - 1, scatter_copy, scatter_all)
        wait(g, scatter_copy, scatter_all)


def _moe_experts(xs, src, owner, w1, w3, w2, layer):
    nt, sl, width = xs.shape
    d = D_MODEL
    n_work = owner.shape[0]
    xs_blocks = xs.reshape(nt * sl // MOE_BLK, MOE_BLK, width)
    any_spec = pl.BlockSpec(memory_space=pl.ANY)
    by_owner = lambda *shape: pl.BlockSpec((None, None) + shape, lambda g, src, own: (layer, own[g], 0, 0))
    buf = pltpu.VMEM((2, MOE_NB, MOE_BLK, width), BF16)
    yt = pl.pallas_call(
        _expert_kernel,
        grid_spec=pltpu.PrefetchScalarGridSpec(
            num_scalar_prefetch=2,
            grid=(n_work,),
            in_specs=[any_spec, by_owner(d, D_FF_EXPERT), by_owner(d, D_FF_EXPERT), by_owner(D_FF_EXPERT, d)],
            out_specs=any_spec,
            scratch_shapes=[
                buf, buf,
                pltpu.VMEM((d, D_FF_EXPERT), BF16), pltpu.VMEM((d, D_FF_EXPERT), BF16),
                pltpu.VMEM((D_FF_EXPERT, d), BF16),
                pltpu.SemaphoreType.DMA((2,)), pltpu.SemaphoreType.DMA((2,)),
            ],
        ),
        out_shape=jax.ShapeDtypeStruct(xs_blocks.shape, BF16),
        input_output_aliases={2: 0},
        compiler_params=pltpu.CompilerParams(dimension_semantics=("arbitrary",), vmem_limit_bytes=VMEM_LIMIT),
        name="moe_experts",
    )(src, owner, xs_blocks, w1, w3, w2)
    return yt.reshape(nt, sl, width)


def _combine_kernel(final, yt_ref, route_ref, z1_ref, mod_ref, g_ref, *rest):
    o_ref = rest[0] if final else rest[3]
    route = route_ref[...]
    slot1, slot2 = route[:, 0:1], route[:, 1:2]
    slot_lane = lax.broadcasted_iota(jnp.int32, (TM, MOE_SLOTS), 1).astype(F32)
    both = jnp.where((slot_lane == slot1) | (slot_lane == slot2), 1.0, 0.0).astype(BF16)
    moe = _dot(both, yt_ref[:, 0:D_MODEL])
    ga2 = mod_ref[...][:, 5 * D_MODEL:6 * D_MODEL]
    z2 = z1_ref[...] + ga2 * moe
    if final:
        o_ref[...] = _rms(z2, g_ref[...])
    else:
        mod_next, gmix_next, wu_next, _, ug_ref, scr_ref = rest
        o_ref[...] = z2
        _u_group_major(z2, mod_next[...], gmix_next[...], wu_next, ug_ref, scr_ref)


def _combine(yt, route, z1, mod, g_final, n_ctx_tiles, next_layer=None):
    n, d = z1.shape
    final = next_layer is None
    tile = lambda w: pl.BlockSpec((TM, w), lambda i: (i, 0))
    in_specs = [pl.BlockSpec((None, MOE_SLOTS, MOE_W), lambda i: (i, 0, 0)), tile(LANES), tile(d),
                _mod_spec(n_ctx_tiles), _const_spec((1, d))]
    out_specs, out_shape, scratch = tile(d), jax.ShapeDtypeStruct((n, d), F32), []
    if not final:
        in_specs += [_mod_spec(n_ctx_tiles), _const_spec((1, d)), _const_spec((d, MIX_W))]
        out_specs = [out_specs, pl.BlockSpec((S5_GROUPS, TM // S5_T, S5_CW), lambda i: (0, i, 0))]
        out_shape = [out_shape, jax.ShapeDtypeStruct((S5_GROUPS, n // S5_T, S5_CW), BF16)]
        scratch = [pltpu.VMEM((MIX_W // LANES, TM, LANES), F32)]
    return pl.pallas_call(
        functools.partial(_combine_kernel, final),
        grid=(n // TM,),
        in_specs=in_specs,
        out_specs=out_specs,
        out_shape=out_shape,
        scratch_shapes=scratch,
        compiler_params=pltpu.CompilerParams(vmem_limit_bytes=VMEM_LIMIT),
        name="combine",
    )(yt, route, z1, mod, g_final, *(next_layer or ()))


def kernel(x, c, ctx, c_ctx, w_ada, b_ada, g_mix, g_ffn, w_in, lam_re, lam_im, log_dt, b_re, b_im, c_re, c_im, d_skip, w_glu, sg_ln_g, sg_ln_b, w_sg, b_sg, w_conv, w_branch, w_merge, b_merge, w_o, w_grp, b_grp, w_exp, b_exp, w1, w3, w2, g_final):
    depth = w_ada.shape[0]
    bsz, seq, d = x.shape
    n_ctx = ctx.shape[1]
    ctx_rows = -(-n_ctx // CTX_ALIGN) * CTX_ALIGN
    n = ctx_rows + seq
    assert bsz == 1 and d == D_MODEL
    assert ctx_rows % TM == 0 and seq % TM == 0 and (TM % n_ctx == 0 or n_ctx % TM == 0)
    assert n_ctx % S5_T == 0 and seq % (S5_T * SUBLANES * SUBLANES) == 0
    n_ctx_tiles = ctx_rows // TM
    n_seg_chunks = seq // (S5_T * SUBLANES)
    nt = n // TM

    z_lat, z_ctx, lat_off = x[0], jnp.pad(ctx[0], ((0, ctx_rows - n_ctx), (0, 0))), n_ctx_tiles
    cc = jnp.zeros((8, d), F32).at[0].set(c_ctx).at[1].set(c[0])
    mods = _ada(cc, w_ada, b_ada)[:, :2].reshape(depth, 2, 1, 6 * d)
    s5_m, s5_bp, s5_cp, s5_tab, s5_pow = _s5_prep(lam_re, lam_im, log_dt, b_re, b_im, c_re, c_im, d_skip, n_seg_chunks)

    for l in range(depth):
        pad = LANES - N_EXPERTS - N_EXP_GROUPS
        p = dict(
            g_mix=g_mix[l].reshape(1, d), g_ffn=g_ffn[l].reshape(1, d),
            w_in5=w_in[l][:, MIX_W:].astype(BF16), w_glu=w_glu[l].astype(BF16),
            sg_ln_g=sg_ln_g[l].reshape(1, MIX_W), sg_ln_b=sg_ln_b[l].reshape(1, MIX_W),
            w_sg=w_sg[l].astype(BF16),
            b_sg_full=jnp.repeat(b_sg[l].T, SG_GROUP_W, axis=1),
            w_conv=w_conv[l], w_branch=w_branch[l].astype(BF16), w_merge=w_merge[l].astype(BF16),
            b_merge=b_merge[l].reshape(1, N_BRANCH * d), w_o=w_o[l].astype(BF16),
            w_router=_split_bf16(jnp.pad(jnp.concatenate([w_exp[l], w_grp[l]], axis=1).T, ((0, pad), (0, 0)))),
            b_router=jnp.pad(jnp.concatenate([b_exp[l], b_grp[l]]), (0, pad)).reshape(LANES, 1),
        )
        mod = mods[l]

        last = l == depth - 1
        first_tile = n_ctx_tiles if last else 0
        if l == 0:
            ug = _u_proj(z_lat, z_ctx, lat_off, n, mod, p["g_mix"], w_in[l][:, :MIX_W].astype(BF16), n_ctx_tiles)
        yg = _s5_apply(ug, s5_m[l], s5_bp[l], s5_cp[l], s5_tab[l], s5_pow[l], n_ctx // S5_T, ctx_rows // S5_T)
        z1, xs, route, pc = _mixer(z_lat, z_ctx, lat_off, n, yg, mod, p, n_ctx_tiles, min(n_ctx, TM), first_tile)
        n_work = (nt - first_tile) * MOE_SLOTS // MOE_TE + N_EXPERTS
        src, owner = _moe_plan(pc[:, :N_EXPERTS, 0], n_work)
        yt = _moe_experts(xs, src, owner, w1, w3, w2, l)
        if last:
            z = _combine(yt, route, z1, mod, g_final.reshape(1, d), n_ctx_tiles - first_tile)
        else:
            nxt = (mods[l + 1], g_mix[l + 1].reshape(1, d), w_in[l + 1][:, :MIX_W].astype(BF16))
            z, ug = _combine(yt, route, z1, mod, p["g_ffn"], n_ctx_tiles, nxt)
        z_lat, z_ctx, lat_off = z, z, 0

    return z.reshape(bsz, seq, d)
```

```python
import functools

import jax
import jax.numpy as jnp
from jax import lax
from jax.experimental import pallas as pl
from jax.experimental.pallas import tpu as pltpu

F32 = jnp.float32
BF16 = jnp.bfloat16
HIGHEST = lax.Precision.HIGHEST

D_MODEL = 1024
GRID_W = 64
MIX_W = D_MODEL // 2
N_BRANCH = 3
S5_GROUP = 16
S5_GROUPS = MIX_W // S5_GROUP
S5_STATE = 64
SG_CHUNK = 128
SG_GROUPS = 4
SG_GROUP_W = MIX_W // SG_GROUPS
CONV_K = 3
N_EXP_GROUPS = 4
EXPERTS_PER_GROUP = 8
N_EXPERTS = N_EXP_GROUPS * EXPERTS_PER_GROUP
D_FF_EXPERT = D_MODEL // 4
NORM_EPS = 1e-6

LANES = 128
SUBLANES = 8
S5_T = 16
S5_CW = S5_T * S5_GROUP
S5_GB = 4
CTX_ALIGN = 512
TM = 512
MOE_BLK = 16
MOE_SLOTS = 2 * TM + N_EXPERTS * MOE_BLK
MOE_TE = 512
MOE_NB = MOE_TE // MOE_BLK
ROUTER_ROWS = 40
MOE_W = D_MODEL + LANES
PLAN_SLACK = 4
PLAN_UNROLL = 3
VMEM_LIMIT = 56 * 1024 * 1024


def _dot(a, b):
    return jnp.dot(a, b, preferred_element_type=F32)


def _sigmoid(x):
    return 0.5 * jnp.tanh(0.5 * x) + 0.5


def _split_bf16(w):
    hi = w.astype(BF16)
    return jnp.stack([hi, (w - hi.astype(F32)).astype(BF16)])


def _rms(x, g):
    return x * lax.rsqrt(jnp.mean(x * x, axis=-1, keepdims=True) + NORM_EPS) * g


def _block_transpose(xs):
    nblk = LANES // S5_GROUP
    assert len(xs) == nblk == 8
    blk = lax.broadcasted_iota(jnp.int32, xs[0].shape, 1) // S5_GROUP
    for k in range(3):
        dist = 1 << k
        hi_half = (blk & dist) != 0
        out = list(xs)
        for a in range(nblk):
            if a & dist == 0:
                p = a + dist
                out[a] = jnp.where(hi_half, pltpu.roll(xs[p], dist * S5_GROUP, 1), xs[a])
                out[p] = jnp.where(hi_half, xs[p], pltpu.roll(xs[a], LANES - dist * S5_GROUP, 1))
        xs = out
    return xs


def _const_spec(shape):
    nd = len(shape)
    return pl.BlockSpec(shape, lambda *_: (0,) * nd, pipeline_mode=pl.Buffered(1))


def _ada_kernel(cc_ref, w_ref, b_ref, o_ref):
    cc = cc_ref[...]
    act = cc * _sigmoid(cc)
    hi = act.astype(BF16)
    lo = (act - hi.astype(F32)).astype(BF16)
    w = w_ref[...].astype(BF16)
    o_ref[...] = _dot(hi, w) + _dot(lo, w) + b_ref[...]


def _ada(cc, w_ada, b_ada):
    depth = w_ada.shape[0]
    nblk = 6
    return pl.pallas_call(
        _ada_kernel,
        grid=(depth, nblk),
        in_specs=[
            pl.BlockSpec((8, D_MODEL), lambda l, j: (0, 0)),
            pl.BlockSpec((None, D_MODEL, D_MODEL), lambda l, j: (l, 0, j)),
            pl.BlockSpec((None, 1, D_MODEL), lambda l, j: (l, 0, j)),
        ],
        out_specs=pl.BlockSpec((None, 8, D_MODEL), lambda l, j: (l, 0, j)),
        out_shape=jax.ShapeDtypeStruct((depth, 8, 6 * D_MODEL), F32),
        name="ada",
    )(cc, w_ada, b_ada.reshape(depth, 1, 6 * D_MODEL))


def _mod_spec(n_ctx_tiles, first_tile=0):
    return pl.BlockSpec((None, 1, 6 * D_MODEL), lambda i: (jnp.where(i + first_tile < n_ctx_tiles, 0, 1), 0, 0))


def _token_specs(n_ctx_tiles, lat_off, first_tile=0):
    lat = pl.BlockSpec((TM, D_MODEL), lambda i: (jnp.maximum(i + first_tile - lat_off, 0), 0))
    ctx = pl.BlockSpec((TM, D_MODEL), lambda i: (jnp.minimum(i + first_tile, n_ctx_tiles - 1), 0))
    return lat, ctx


def _u_kernel(n_ctx_tiles, zl_ref, zc_ref, mod_ref, g_ref, w_ref, ug_ref, scr_ref):
    z = jnp.where(pl.program_id(0) < n_ctx_tiles, zc_ref[...], zl_ref[...])
    _u_group_major(z, mod_ref[...], g_ref[...], w_ref, ug_ref, scr_ref)


def _u_group_major(z, mod, g, w_ref, ug_ref, scr_ref):
    h = _rms(z, g) * (1.0 + mod[:, D_MODEL:2 * D_MODEL]) + mod[:, 0:D_MODEL]
    u = _dot(h.astype(BF16), w_ref[...])
    n_chunks = TM // S5_T
    for q in range(MIX_W // LANES):
        scr_ref[q] = u[:, q * LANES:(q + 1) * LANES]
    for q in range(MIX_W // LANES):
        for half in range(S5_T // SUBLANES):
            rows = [scr_ref[q, pl.ds(half * SUBLANES + a, n_chunks, stride=S5_T), :] for a in range(SUBLANES)]
            for b, y in enumerate(_block_transpose(rows)):
                ug_ref[q * SUBLANES + b, :, half * LANES:(half + 1) * LANES] = y.astype(BF16)


def _u_proj(z_lat, z_ctx, lat_off, n, mod, g_mix, w_u, n_ctx_tiles):
    return pl.pallas_call(
        functools.partial(_u_kernel, n_ctx_tiles),
        grid=(n // TM,),
        in_specs=[
            *_token_specs(n_ctx_tiles, lat_off),
            _mod_spec(n_ctx_tiles),
            _const_spec((1, D_MODEL)),
            _const_spec((D_MODEL, MIX_W)),
        ],
        out_specs=pl.BlockSpec((S5_GROUPS, TM // S5_T, S5_CW), lambda i: (0, i, 0)),
        out_shape=jax.ShapeDtypeStruct((S5_GROUPS, n // S5_T, S5_CW), BF16),
        scratch_shapes=[pltpu.VMEM((MIX_W // LANES, TM, LANES), F32)],
        name="u_proj",
    )(z_lat, z_ctx, mod, g_mix, w_u)


def _s5_prep_kernel(n_seg_chunks, par_ref, bt_ref, ct_ref, dsk_ref, m_ref, bp_ref, cp_ref, tab_ref, pow_ref):
    t = S5_T
    par = par_ref[...]
    lo = lax.broadcasted_iota(jnp.int32, (1, LANES), 1) < S5_STATE
    sgn = jnp.where(lo, -1.0, 1.0)

    def rows(n):
        return lax.broadcasted_iota(jnp.int32, (n, 1), 0).astype(F32)

    def expand(pw_rows, vec_rows):
        pr, pi = pw_rows
        vr, vi = vec_rows
        pr, pi = pr[:, None, :], pi[:, None, :]
        vr, vi = vr[None, :, :], vi[None, :, :]
        re = (pr * vr - pi * vi).reshape(-1, LANES)
        im = (pr * vi + pi * vr).reshape(-1, LANES)
        return re, im

    mt = None
    bp_cols, cp_rows, tab_rows, pow_rows = [], [], [], []
    for d in range(2):
        lre, lim = par[2 * d:2 * d + 1], par[2 * d + 1:2 * d + 2]
        dt = jnp.exp(par[4 + d:5 + d])
        xr, xi = lre * dt, lim * dt

        def cpow(m):
            mag = jnp.exp(m * xr)
            return mag * jnp.cos(m * xi), mag * jnp.sin(m * xi)

        ar, ai = cpow(1.0)
        den = lre * lre + lim * lim
        qr = ((ar - 1.0) * lre + ai * lim) / den
        qi = (ai * lre - (ar - 1.0) * lim) / den
        br, bi = bt_ref[2 * d], bt_ref[2 * d + 1]
        bb = (qr * br - qi * bi, qr * bi + qi * br)
        cc = (ct_ref[2 * d], ct_ref[2 * d + 1])

        inj_pow = cpow(t - 1.0 - rows(t)) if d == 0 else cpow(rows(t))
        re, im = expand(inj_pow, bb)
        bp_cols.append(jnp.where(lo, re, im))

        out_pow = cpow(rows(t) + 1.0) if d == 0 else cpow(t - rows(t))
        re, im = expand(out_pow, cc)
        cp_rows.append(jnp.where(lo, re, -im).T)

        lag_pow = cpow(rows(t)) if d == 0 else cpow(t - 1.0 - rows(t))
        re, im = expand(lag_pow, cc)
        ctab = jnp.where(lo, re, -im)
        bcat = jnp.broadcast_to(jnp.where(lo, bb[0], bb[1])[None], (t, S5_GROUP, LANES)).reshape(-1, LANES)
        k256 = lax.dot_general(ctab, bcat, (((1,), (1,)), ((), ())), precision=HIGHEST,
                               preferred_element_type=F32)
        col_blk = lax.broadcasted_iota(jnp.int32, (S5_CW, S5_CW), 1) // S5_GROUP
        acc = jnp.zeros((S5_CW, S5_CW), F32)
        for r in range(t):
            sh = r * S5_GROUP if d == 0 else (t - 1 - r) * S5_GROUP
            if sh == 0:
                shifted = k256
            elif d == 0:
                shifted = jnp.concatenate([jnp.zeros((sh, S5_CW), F32), k256[:S5_CW - sh]], axis=0)
            else:
                shifted = jnp.concatenate([k256[sh:], jnp.zeros((sh, S5_CW), F32)], axis=0)
            acc = acc + jnp.where(col_blk == r, shifted, 0.0)
        mt = acc if mt is None else mt + acc

        for m in (float(t), float(t * n_seg_chunks)):
            re, im = cpow(m)
            tab_rows += [re, sgn * im]
        blk, j = rows(n_seg_chunks // SUBLANES), rows(SUBLANES)
        if d == 0:
            re, im = expand(cpow(t * SUBLANES * blk), cpow(t * j))
        else:
            re, im = expand(cpow(t * SUBLANES * (n_seg_chunks // SUBLANES - 1.0 - blk)), cpow(t * (SUBLANES - 1.0 - j)))
        pow_rows += [re, sgn * im]

    eye = (lax.broadcasted_iota(jnp.int32, (S5_CW, S5_CW), 0) == lax.broadcasted_iota(jnp.int32, (S5_CW, S5_CW), 1))
    mt = mt + jnp.where(eye, dsk_ref[...], 0.0)
    m_ref[...] = mt.T.astype(BF16)
    bp_ref[...] = jnp.concatenate(bp_cols, axis=1).astype(BF16)
    cp_ref[...] = jnp.concatenate(cp_rows, axis=0).astype(BF16)
    tab_ref[...] = jnp.concatenate(tab_rows, axis=0)
    for q in range(4):
        pow_ref[q] = pow_rows[q]


def _s5_prep(lam_re, lam_im, log_dt, b_re, b_im, c_re, c_im, d_skip, n_seg_chunks):
    depth = lam_re.shape[0]
    g, p_, h = S5_GROUPS, S5_STATE, S5_GROUP
    dup = lambda a: jnp.concatenate([a, a], axis=-1)
    dirs_last = lambda a: jnp.moveaxis(a, 1, 2)
    lam = jnp.stack([lam_re, lam_im], axis=3)
    lam = dirs_last(lam).reshape(depth, g, 4, p_)
    ldt = jnp.broadcast_to(dirs_last(log_dt[..., None]), (depth, g, 2, p_))
    par = dup(jnp.concatenate([lam, ldt, jnp.zeros((depth, g, 2, p_), F32)], axis=2))
    bt = jnp.stack([b_re, b_im], axis=3)
    bt = dup(jnp.swapaxes(dirs_last(bt), -1, -2).reshape(depth, g, 4, h, p_))
    ct = dup(dirs_last(jnp.stack([c_re, c_im], axis=3)).reshape(depth, g, 4, h, p_))
    dsk = jnp.tile(d_skip.reshape(depth, g, 1, h), (1, 1, 1, S5_T))

    blk = lambda *shape: pl.BlockSpec((None, None) + shape, lambda l, j: (l, j) + (0,) * len(shape))
    return pl.pallas_call(
        functools.partial(_s5_prep_kernel, n_seg_chunks),
        grid=(depth, g),
        in_specs=[blk(8, LANES), blk(4, h, LANES), blk(4, h, LANES), blk(1, S5_CW)],
        out_specs=[blk(S5_CW, S5_CW), blk(S5_CW, 2 * LANES), blk(2 * LANES, S5_CW), blk(8, LANES),
                   blk(4, n_seg_chunks, LANES)],
        out_shape=[
            jax.ShapeDtypeStruct((depth, g, S5_CW, S5_CW), BF16),
            jax.ShapeDtypeStruct((depth, g, S5_CW, 2 * LANES), BF16),
            jax.ShapeDtypeStruct((depth, g, 2 * LANES, S5_CW), BF16),
            jax.ShapeDtypeStruct((depth, g, 8, LANES), F32),
            jax.ShapeDtypeStruct((depth, g, 4, n_seg_chunks, LANES), F32),
        ],
        name="s5_prep",
    )(par, bt, ct, dsk)


def _cmul_step(s, s_sw, v, v_sw, re, ims):
    return re * s + ims * s_sw + v, re * s_sw - ims * s + v_sw


def _s5_kernel(n_ctx_chunks, n_lat0, ug_ref, m_ref, bp_ref, cp_ref, tab_ref, pow_ref, y_ref,
               vf_ref, vfs_ref, vb_ref, vbs_ref):
    gb, n_rows, _ = ug_ref.shape
    n_seg = (n_rows - n_lat0) // SUBLANES
    pitch = vf_ref.shape[1] // SUBLANES
    half = S5_STATE

    def put_segments(ref, g, val):
        for r in range(SUBLANES):
            ref[g, r * pitch:r * pitch + n_seg, :] = val[r * n_seg:(r + 1) * n_seg]

    tab = tab_ref[...]
    ctx_end = []
    sp_ctx = []
    for g in range(gb):
        v = _dot(ug_ref[g], bp_ref[g])
        vl = v[n_lat0:]
        put_segments(vf_ref, g, vl[:, :LANES])
        put_segments(vb_ref, g, vl[:, LANES:])
        put_segments(vfs_ref, g, pltpu.roll(vl[:, :LANES], half, 1))
        put_segments(vbs_ref, g, pltpu.roll(vl[:, LANES:], half, 1))

        vc = v[:n_ctx_chunks]
        vcs = jnp.concatenate([pltpu.roll(vc[:, :LANES], half, 1), pltpu.roll(vc[:, LANES:], half, 1)], axis=1)
        zero = jnp.zeros((1, LANES), F32)
        sf, sfs, sb, sbs = zero, zero, zero, zero
        rows_f, rows_b = [], [None] * n_ctx_chunks
        for j in range(n_ctx_chunks):
            jb = n_ctx_chunks - 1 - j
            rows_f.append(sf)
            rows_b[jb] = sb
            sf, sfs = _cmul_step(sf, sfs, vc[j:j + 1, :LANES], vcs[j:j + 1, :LANES], tab[g, 0:1], tab[g, 1:2])
            sb, sbs = _cmul_step(sb, sbs, vc[jb:jb + 1, LANES:], vcs[jb:jb + 1, LANES:], tab[g, 4:5], tab[g, 5:6])
        ctx_end.append((sf, sfs, sb, sbs))
        sp_ctx.append(jnp.concatenate([jnp.concatenate(rows_f, axis=0), jnp.concatenate(rows_b, axis=0)], axis=1))

    coef = [[jnp.broadcast_to(tab[g, q:q + 1], (SUBLANES, LANES)) for q in (0, 1, 4, 5)] for g in range(gb)]

    def step(k, carry):
        kb = n_seg - 1 - k
        out = []
        for g in range(gb):
            sf, sfs, sb, sbs = carry[g]
            are_f, aim_f, are_b, aim_b = coef[g]
            at_f = pl.ds(k, SUBLANES, stride=pitch)
            at_b = pl.ds(kb, SUBLANES, stride=pitch)
            v_f, v_fs = vf_ref[g, at_f, :], vfs_ref[g, at_f, :]
            v_b, v_bs = vb_ref[g, at_b, :], vbs_ref[g, at_b, :]
            vf_ref[g, at_f, :] = sf
            vb_ref[g, at_b, :] = sb
            sf, sfs = _cmul_step(sf, sfs, v_f, v_fs, are_f, aim_f)
            sb, sbs = _cmul_step(sb, sbs, v_b, v_bs, are_b, aim_b)
            out.append((sf, sfs, sb, sbs))
        return tuple(out)

    zero8 = jnp.zeros((SUBLANES, LANES), F32)
    ends = lax.fori_loop(0, n_seg, step, tuple((zero8, zero8, zero8, zero8) for _ in range(gb)))

    for g in range(gb):
        ef, efs, eb, ebs = ends[g]
        sf, sfs, sb, sbs = ctx_end[g]
        ent_f, ent_b = [], [None] * SUBLANES
        for r in range(SUBLANES):
            rb = SUBLANES - 1 - r
            ent_f.append((sf, sfs))
            ent_b[rb] = (sb, sbs)
            sf, sfs = _cmul_step(sf, sfs, ef[r:r + 1], efs[r:r + 1], tab[g, 2:3], tab[g, 3:4])
            sb, sbs = _cmul_step(sb, sbs, eb[rb:rb + 1], ebs[rb:rb + 1], tab[g, 6:7], tab[g, 7:8])

        segs = []
        for r in range(SUBLANES):
            seg_rows = slice(r * pitch, r * pitch + n_seg)
            f = vf_ref[g, seg_rows, :] + pow_ref[g, 0] * ent_f[r][0] + pow_ref[g, 1] * ent_f[r][1]
            b = vb_ref[g, seg_rows, :] + pow_ref[g, 2] * ent_b[r][0] + pow_ref[g, 3] * ent_b[r][1]
            segs.append(jnp.concatenate([f, b], axis=1))
        pad = jnp.zeros((n_lat0 - n_ctx_chunks, 2 * LANES), F32)
        sprev = jnp.concatenate([sp_ctx[g], pad] + segs, axis=0).astype(BF16)
        y_ref[g] = _dot(ug_ref[g], m_ref[g]) + _dot(sprev, cp_ref[g])


def _s5_apply(ug, m, bp, cp, tab, pw, n_ctx_chunks, n_lat0):
    n_groups, n_rows, cw = ug.shape
    n_seg = pw.shape[-2]
    assert n_rows - n_lat0 == SUBLANES * n_seg and n_lat0 % SUBLANES == 0
    blk = lambda *shape: pl.BlockSpec((S5_GB,) + shape, lambda i: (i,) + (0,) * len(shape))
    seg = pltpu.VMEM((S5_GB, SUBLANES * (n_seg + SUBLANES), LANES), F32)
    return pl.pallas_call(
        functools.partial(_s5_kernel, n_ctx_chunks, n_lat0),
        grid=(n_groups // S5_GB,),
        in_specs=[blk(n_rows, cw), blk(cw, cw), blk(cw, 2 * LANES), blk(2 * LANES, cw), blk(8, LANES),
                  blk(4, n_seg, LANES)],
        out_specs=blk(n_rows, cw),
        out_shape=jax.ShapeDtypeStruct((n_groups, n_rows, cw), F32),
        scratch_shapes=[seg, seg, seg, seg],
        compiler_params=pltpu.CompilerParams(vmem_limit_bytes=VMEM_LIMIT),
        name="s5_scan",
    )(ug, m, bp, cp, tab, pw)


def _mixer_kernel(n_ctx_tiles, ctx_row_len, first_tile, zl_ref, zc_ref, y_ref, mod_ref, gmix_ref, gffn_ref, win_ref, wglu_ref,
                  lng_ref, lnb_ref, wsg_ref, bsg_ref, wconv_ref, wbr_ref, wmg_ref, bmg_ref, wo_ref, wr_ref,
                  br_ref, z1_ref, xs_ref, route_ref, pc_ref, yscr_ref):
    d = D_MODEL
    n_chunks = TM // S5_T
    for q in range(MIX_W // LANES):
        for half in range(S5_T // SUBLANES):
            cols = [y_ref[q * SUBLANES + b, :, half * LANES:(half + 1) * LANES] for b in range(SUBLANES)]
            for a, rows in enumerate(_block_transpose(cols)):
                yscr_ref[q, pl.ds(half * SUBLANES + a, n_chunks, stride=S5_T), :] = rows
    y_tok = jnp.concatenate([yscr_ref[q] for q in range(MIX_W // LANES)], axis=1)
    is_ctx = pl.program_id(0) + first_tile < n_ctx_tiles
    z = jnp.where(is_ctx, zc_ref[...], zl_ref[...])
    mod = mod_ref[...]
    sh1, sc1, ga1 = mod[:, 0:d], mod[:, d:2 * d], mod[:, 2 * d:3 * d]
    sh2, sc2 = mod[:, 3 * d:4 * d], mod[:, 4 * d:5 * d]
    h = _rms(z, gmix_ref[...]) * (1.0 + sc1) + sh1
    hb = h.astype(BF16)
    proj = _dot(hb, win_ref[...])
    su, sv, cgb, cgc, chh = [proj[:, k * MIX_W:(k + 1) * MIX_W] for k in range(5)]

    def merge_term(k, y_k):
        gate = _sigmoid(_dot(hb, wmg_ref[:, k * d:(k + 1) * d]) + bmg_ref[:, k * d:(k + 1) * d])
        return gate * _dot(y_k.astype(BF16), wbr_ref[k])

    yg = jax.nn.gelu(y_tok)
    merged = merge_term(0, yg * _sigmoid(_dot(yg.astype(BF16), wglu_ref[...])))

    gv = jax.nn.gelu(sv)
    mu = jnp.mean(gv, axis=-1, keepdims=True)
    var = jnp.mean(jnp.square(gv - mu), axis=-1, keepdims=True)
    vn = ((gv - mu) * lax.rsqrt(var + NORM_EPS) * lng_ref[...] + lnb_ref[...]).astype(BF16)
    bsg = bsg_ref[...]
    chunks = []
    for n in range(TM // SG_CHUNK):
        rows = slice(n * SG_CHUNK, (n + 1) * SG_CHUNK)
        cols = [_dot(wsg_ref[g], vn[rows, g * SG_GROUP_W:(g + 1) * SG_GROUP_W]) for g in range(SG_GROUPS)]
        chunks.append(jnp.concatenate(cols, axis=1) + bsg)
    merged = merged + merge_term(1, jax.nn.gelu(su) * jnp.concatenate(chunks, axis=0))

    zc = cgc * chh
    row = lax.broadcasted_iota(jnp.int32, (TM, 1), 0)
    pos = jnp.where(is_ctx, row % ctx_row_len, row % GRID_W)
    prev = jnp.where(pos == 0, 0.0, pltpu.roll(zc, 1, 0))
    nxt = jnp.where(pos == jnp.where(is_ctx, ctx_row_len - 1, GRID_W - 1), 0.0, pltpu.roll(zc, TM - 1, 0))
    wc = wconv_ref[...]
    merged = merged + merge_term(2, cgb * (wc[0:1] * prev + wc[1:2] * zc + wc[2:3] * nxt))

    z1 = z + ga1 * _dot(merged.astype(BF16), wo_ref[...])
    z1_ref[...] = z1

    h2 = _rms(z1, gffn_ref[...]) * (1.0 + sc2) + sh2
    h2_hi = h2.astype(BF16)
    h2_lo = (h2 - h2_hi.astype(F32)).astype(BF16)
    nt_dims = (((1,), (1,)), ((), ()))
    logits_t = (lax.dot_general(wr_ref[0], h2_hi, nt_dims, preferred_element_type=F32)
                + lax.dot_general(wr_ref[0], h2_lo, nt_dims, preferred_element_type=F32)
                + lax.dot_general(wr_ref[1], h2_hi, nt_dims, preferred_element_type=F32)) + br_ref[...]
    lg = logits_t[0:ROUTER_ROWS]
    row = lax.broadcasted_iota(jnp.int32, (ROUTER_ROWS, TM), 0)
    big = jnp.int32(LANES)
    neg = jnp.float32(-jnp.inf)
    top = lambda v: jnp.max(v, axis=0, keepdims=True)
    total = lambda v: jnp.sum(v, axis=0, keepdims=True)
    first = lambda m: jnp.min(jnp.where(m, row, big), axis=0, keepdims=True)
    is_grp = (row >= N_EXPERTS) & (row < N_EXPERTS + N_EXP_GROUPS)
    gl = jnp.where(is_grp, lg, neg)
    ge = jnp.exp(gl - top(gl))
    gp = ge / total(ge)
    g_p = top(gp)
    g_idx = first(is_grp & (gp == g_p)) - N_EXPERTS
    in_grp = (row >= g_idx * EXPERTS_PER_GROUP) & (row < (g_idx + 1) * EXPERTS_PER_GROUP)
    el = jnp.where(in_grp, lg, neg)
    ee = jnp.exp(el - top(el))
    ep = ee / total(ee)
    p1 = top(ep)
    i1 = first(in_grp & (ep == p1))
    rest = in_grp & (row != i1)
    ep2 = jnp.where(rest, ep, -1.0)
    p2 = top(ep2)
    i2 = first(rest & (ep2 == p2))
    tot = p1 + p2
    w_1, w_2 = g_p * (p1 / tot), g_p * (p2 / tot)

    oh1, oh2 = row == i1, row == i2
    oh = jnp.where(oh1 | oh2, 1.0, 0.0)
    cnt = jnp.sum(oh, axis=1, keepdims=True)
    pcb = jnp.floor((cnt + (MOE_BLK - 1)) * (1.0 / MOE_BLK))
    er, ec = (lax.broadcasted_iota(jnp.int32, (ROUTER_ROWS, ROUTER_ROWS), k) for k in (0, 1))
    before = jnp.where(ec < er, 1.0, 0.0).astype(BF16)
    startb = _dot(before, jnp.broadcast_to(pcb, (ROUTER_ROWS, LANES)).astype(BF16))[:, 0:1]
    start = startb * MOE_BLK
    plan_lane = lax.broadcasted_iota(jnp.int32, (ROUTER_ROWS, LANES), 1)
    pc_ref[...] = jnp.where(plan_lane == 0, pcb, jnp.where(plan_lane == 1, startb, 0.0)).astype(jnp.int32)
    tr, tc = (lax.broadcasted_iota(jnp.int32, (TM, TM), k) for k in (0, 1))
    earlier = jnp.where(tr < tc, 1.0, 0.0).astype(BF16)
    slot = start + _dot(oh.astype(BF16), earlier)
    slot1 = total(jnp.where(oh1, slot, 0.0))
    slot2 = total(jnp.where(oh2, slot, 0.0))

    def token_rows(vals):
        r8 = lax.broadcasted_iota(jnp.int32, (SUBLANES, TM), 0)
        rows8 = jnp.zeros((SUBLANES, TM), F32)
        for k, v in enumerate(vals):
            rows8 = jnp.where(r8 == k, v, rows8)
        return jnp.concatenate([rows8, jnp.zeros((LANES - SUBLANES, TM), F32)], axis=0).T

    route_ref[...] = token_rows([slot1, slot2])
    sl = xs_ref.shape[0]
    slot_row = lax.broadcasted_iota(jnp.int32, (sl, TM), 0)
    perm1 = jnp.where(slot_row == slot1.astype(jnp.int32), 1.0, 0.0).astype(BF16)
    perm2 = jnp.where(slot_row == slot2.astype(jnp.int32), 1.0, 0.0).astype(BF16)
    xs_ref[:, 0:d] = _dot(perm1 + perm2, h2_hi).astype(BF16)

    def weight_lanes(w):
        hi = w.astype(BF16).astype(F32)
        return token_rows([hi, w - hi]).astype(BF16)

    xs_ref[:, d:] = (_dot(perm1, weight_lanes(w_1)) + _dot(perm2, weight_lanes(w_2))).astype(BF16)


def _mixer(z_lat, z_ctx, lat_off, n, yg, mod, p, n_ctx_tiles, ctx_row_len, first_tile):
    d = D_MODEL
    nt = n // TM - first_tile
    tile = lambda w: pl.BlockSpec((TM, w), lambda i: (i, 0))
    return pl.pallas_call(
        functools.partial(_mixer_kernel, n_ctx_tiles, ctx_row_len, first_tile),
        grid=(nt,),
        in_specs=[
            *_token_specs(n_ctx_tiles, lat_off, first_tile),
            pl.BlockSpec((S5_GROUPS, TM // S5_T, S5_CW), lambda i: (0, i + first_tile, 0)),
            _mod_spec(n_ctx_tiles, first_tile),
            _const_spec((1, d)), _const_spec((1, d)),
            _const_spec((d, 5 * MIX_W)), _const_spec((MIX_W, MIX_W)),
            _const_spec((1, MIX_W)), _const_spec((1, MIX_W)),
            _const_spec((SG_GROUPS, SG_CHUNK, SG_CHUNK)), _const_spec((SG_CHUNK, MIX_W)),
            _const_spec((CONV_K, MIX_W)), _const_spec((N_BRANCH, MIX_W, d)),
            _const_spec((d, N_BRANCH * d)), _const_spec((1, N_BRANCH * d)),
            _const_spec((d, d)), _const_spec((2, LANES, d)), _const_spec((LANES, 1)),
        ],
        out_specs=[tile(d), pl.BlockSpec((None, MOE_SLOTS, MOE_W), lambda i: (i, 0, 0)), tile(LANES),
                   pl.BlockSpec((None, ROUTER_ROWS, LANES), lambda i: (i, 0, 0))],
        out_shape=[
            jax.ShapeDtypeStruct((nt * TM, d), F32),
            jax.ShapeDtypeStruct((nt, MOE_SLOTS, MOE_W), BF16),
            jax.ShapeDtypeStruct((nt * TM, LANES), F32),
            jax.ShapeDtypeStruct((nt, ROUTER_ROWS, LANES), jnp.int32),
        ],
        scratch_shapes=[pltpu.VMEM((MIX_W // LANES, TM, LANES), F32)],
        compiler_params=pltpu.CompilerParams(vmem_limit_bytes=VMEM_LIMIT),
        name="mixer",
    )(z_lat, z_ctx, yg, mod, p["g_mix"], p["g_ffn"], p["w_in5"], p["w_glu"], p["sg_ln_g"], p["sg_ln_b"],
      p["w_sg"], p["b_sg_full"], p["w_conv"], p["w_branch"], p["w_merge"], p["b_merge"], p["w_o"],
      p["w_router"], p["b_router"])


def _plan_kernel(nt, n_work, nb_ref, sb_ref, src_ref, exp_ref):
    bpt = MOE_SLOTS // MOE_BLK

    def no_block(b, c):
        src_ref[b] = -1
        return c

    def expert(e, g0):
        def tile(i, pos):
            nb = nb_ref[i, e]
            s0 = i * bpt + sb_ref[i, e]
            base = g0 * MOE_NB + pos

            def blk(b, c):
                src_ref[base + b] = s0 + b
                return c

            for b in range(PLAN_SLACK):
                src_ref[base + b] = s0 + b
            lax.fori_loop(PLAN_SLACK, nb, blk, 0)
            return pos + nb

        def tiles(j, pos):
            for k in range(PLAN_UNROLL):
                pos = tile(j * PLAN_UNROLL + k, pos)
            return pos

        pos = lax.fori_loop(0, nt // PLAN_UNROLL, tiles, 0)
        for i in range(nt - nt % PLAN_UNROLL, nt):
            pos = tile(i, pos)
        n_tiles = (pos + MOE_NB - 1) // MOE_NB
        lax.fori_loop(g0 * MOE_NB + pos, (g0 + n_tiles) * MOE_NB, no_block, 0)

        def owner(t, c):
            exp_ref[g0 + t] = e
            return c

        lax.fori_loop(0, n_tiles, owner, 0)
        return g0 + n_tiles

    g_end = lax.fori_loop(0, N_EXPERTS, expert, 0)
    lax.fori_loop(g_end * MOE_NB, n_work * MOE_NB + PLAN_SLACK, no_block, 0)

    def idle(g, c):
        exp_ref[g] = N_EXPERTS - 1
        return c

    lax.fori_loop(g_end, n_work, idle, 0)


def _moe_plan(nb, sb, n_work):
    nt = nb.shape[0]
    smem = pl.BlockSpec(memory_space=pltpu.SMEM)
    return pl.pallas_call(
        functools.partial(_plan_kernel, nt, n_work),
        in_specs=[smem, smem],
        out_specs=[smem, smem],
        out_shape=[jax.ShapeDtypeStruct((n_work * MOE_NB + PLAN_SLACK,), jnp.int32),
                   jax.ShapeDtypeStruct((n_work,), jnp.int32)],
        name="moe_plan",
    )(nb, sb)


def _expert_kernel(src_ref, exp_ref, xs_hbm, w1_ref, w3_ref, w2_ref, yt_hbm, lhs_ref, out_ref, wb1_ref, wb3_ref,
                   wb2_ref, sem_in, sem_out):
    g = pl.program_id(0)
    n_work = pl.num_programs(0)

    def first(step):
        return src_ref[step * MOE_NB]

    def last(step):
        return src_ref[step * MOE_NB + MOE_NB - 1]

    def gather_copy(step, b):
        return pltpu.make_async_copy(xs_hbm.at[src_ref[step * MOE_NB + b]], lhs_ref.at[step % 2, b],
                                     sem_in.at[step % 2])

    def scatter_copy(step, b):
        return pltpu.make_async_copy(out_ref.at[step % 2, b], yt_hbm.at[src_ref[step * MOE_NB + b]],
                                     sem_out.at[step % 2])

    def gather_all(step):
        return pltpu.make_async_copy(xs_hbm.at[pl.ds(0, MOE_NB)], lhs_ref.at[step % 2], sem_in.at[step % 2])

    def scatter_all(step):
        return pltpu.make_async_copy(out_ref.at[step % 2], yt_hbm.at[pl.ds(0, MOE_NB)], sem_out.at[step % 2])

    def per_block(step, fn):
        def body(b, c):
            @pl.when(src_ref[step * MOE_NB + b] >= 0)
            def _():
                fn(b)
            return c
        lax.fori_loop(0, MOE_NB, body, 0)

    def start(step, copy):
        @pl.when(last(step) >= 0)
        def _():
            for b in range(MOE_NB):
                copy(step, b).start()

        @pl.when((last(step) < 0) & (first(step) >= 0))
        def _():
            per_block(step, lambda b: copy(step, b).start())

    def wait(step, copy, copy_all):
        @pl.when(last(step) >= 0)
        def _():
            copy_all(step).wait()

        @pl.when((last(step) < 0) & (first(step) >= 0))
        def _():
            per_block(step, lambda b: copy(step, b).wait())

    @pl.when(g == 0)
    def _():
        start(g, gather_copy)

    @pl.when(g + 1 < n_work)
    def _():
        start(g + 1, gather_copy)

    wait(g, gather_copy, gather_all)

    @pl.when(g >= 2)
    def _():
        wait(g - 2, scatter_copy, scatter_all)

    slot = g % 2

    @pl.when(first(g) >= 0)
    def _():
        @pl.when((g == 0) | (exp_ref[g] != exp_ref[jnp.maximum(g - 1, 0)]))
        def _():
            wb1_ref[...] = w1_ref[...].astype(BF16)
            wb3_ref[...] = w3_ref[...].astype(BF16)
            wb2_ref[...] = w2_ref[...].astype(BF16)

        @pl.when(last(g) < 0)
        def _():
            def zero_missing(b, c):
                @pl.when(src_ref[g * MOE_NB + b] < 0)
                def _():
                    lhs_ref[slot, b] = jnp.zeros((MOE_BLK, MOE_W), BF16)
                return c
            lax.fori_loop(0, MOE_NB, zero_missing, 0)

        rows = lhs_ref[slot].reshape(MOE_TE, MOE_W)
        x = rows[:, :D_MODEL]
        w_lanes = rows[:, D_MODEL:].astype(F32)
        w_row = w_lanes[:, 0:1] + w_lanes[:, 1:2]
        a = _dot(x, wb1_ref[...])
        act = a * _sigmoid(a) * _dot(x, wb3_ref[...])
        y = (_dot(act.astype(BF16), wb2_ref[...]) * w_row).astype(BF16)
        out_ref[slot] = jnp.concatenate([y, jnp.zeros((MOE_TE, LANES), BF16)], axis=1).reshape(MOE_NB, MOE_BLK, MOE_W)

    start(g, scatter_copy)

    @pl.when(g == n_work - 1)
    def _():
        @pl.when(g >= 1)
        def _():
            wait(g - 1, scatter_copy, scatter_all)
        wait(g, scatter_copy, scatter_all)


def _moe_experts(xs, src, owner, w1, w3, w2, layer):
    nt, sl, width = xs.shape
    d = D_MODEL
    n_work = owner.shape[0]
    xs_blocks = xs.reshape(nt * sl // MOE_BLK, MOE_BLK, width)
    any_spec = pl.BlockSpec(memory_space=pl.ANY)
    by_owner = lambda *shape: pl.BlockSpec((None, None) + shape, lambda g, src, own: (layer, own[g], 0, 0))
    buf = pltpu.VMEM((2, MOE_NB, MOE_BLK, width), BF16)
    yt = pl.pallas_call(
        _expert_kernel,
        grid_spec=pltpu.PrefetchScalarGridSpec(
            num_scalar_prefetch=2,
            grid=(n_work,),
            in_specs=[any_spec, by_owner(d, D_FF_EXPERT), by_owner(d, D_FF_EXPERT), by_owner(D_FF_EXPERT, d)],
            out_specs=any_spec,
            scratch_shapes=[
                buf, buf,
                pltpu.VMEM((d, D_FF_EXPERT), BF16), pltpu.VMEM((d, D_FF_EXPERT), BF16),
                pltpu.VMEM((D_FF_EXPERT, d), BF16),
                pltpu.SemaphoreType.DMA((2,)), pltpu.SemaphoreType.DMA((2,)),
            ],
        ),
        out_shape=jax.ShapeDtypeStruct(xs_blocks.shape, BF16),
        input_output_aliases={2: 0},
        compiler_params=pltpu.CompilerParams(dimension_semantics=("arbitrary",), vmem_limit_bytes=VMEM_LIMIT),
        name="moe_experts",
    )(src, owner, xs_blocks, w1, w3, w2)
    return yt.reshape(nt, sl, width)


def _combine_kernel(final, yt_ref, route_ref, z1_ref, mod_ref, g_ref, *rest):
    o_ref = rest[0] if final else rest[3]
    route = route_ref[...]
    slot1, slot2 = route[:, 0:1], route[:, 1:2]
    slot_lane = lax.broadcasted_iota(jnp.int32, (TM, MOE_SLOTS), 1).astype(F32)
    both = jnp.where((slot_lane == slot1) | (slot_lane == slot2), 1.0, 0.0).astype(BF16)
    moe = _dot(both, yt_ref[:, 0:D_MODEL])
    ga2 = mod_ref[...][:, 5 * D_MODEL:6 * D_MODEL]
    z2 = z1_ref[...] + ga2 * moe
    if final:
        o_ref[...] = _rms(z2, g_ref[...])
    else:
        mod_next, gmix_next, wu_next, _, ug_ref, scr_ref = rest
        o_ref[...] = z2
        _u_group_major(z2, mod_next[...], gmix_next[...], wu_next, ug_ref, scr_ref)


def _combine(yt, route, z1, mod, g_final, n_ctx_tiles, next_layer=None):
    n, d = z1.shape
    final = next_layer is None
    tile = lambda w: pl.BlockSpec((TM, w), lambda i: (i, 0))
    in_specs = [pl.BlockSpec((None, MOE_SLOTS, MOE_W), lambda i: (i, 0, 0)), tile(LANES), tile(d),
                _mod_spec(n_ctx_tiles), _const_spec((1, d))]
    out_specs, out_shape, scratch = tile(d), jax.ShapeDtypeStruct((n, d), F32), []
    if not final:
        in_specs += [_mod_spec(n_ctx_tiles), _const_spec((1, d)), _const_spec((d, MIX_W))]
        out_specs = [out_specs, pl.BlockSpec((S5_GROUPS, TM // S5_T, S5_CW), lambda i: (0, i, 0))]
        out_shape = [out_shape, jax.ShapeDtypeStruct((S5_GROUPS, n // S5_T, S5_CW), BF16)]
        scratch = [pltpu.VMEM((MIX_W // LANES, TM, LANES), F32)]
    return pl.pallas_call(
        functools.partial(_combine_kernel, final),
        grid=(n // TM,),
        in_specs=in_specs,
        out_specs=out_specs,
        out_shape=out_shape,
        scratch_shapes=scratch,
        compiler_params=pltpu.CompilerParams(vmem_limit_bytes=VMEM_LIMIT),
        name="combine",
    )(yt, route, z1, mod, g_final, *(next_layer or ()))


def kernel(x, c, ctx, c_ctx, w_ada, b_ada, g_mix, g_ffn, w_in, lam_re, lam_im, log_dt, b_re, b_im, c_re, c_im, d_skip, w_glu, sg_ln_g, sg_ln_b, w_sg, b_sg, w_conv, w_branch, w_merge, b_merge, w_o, w_grp, b_grp, w_exp, b_exp, w1, w3, w2, g_final):
    depth = w_ada.shape[0]
    bsz, seq, d = x.shape
    n_ctx = ctx.shape[1]
    ctx_rows = -(-n_ctx // CTX_ALIGN) * CTX_ALIGN
    n = ctx_rows + seq
    assert bsz == 1 and d == D_MODEL
    assert ctx_rows % TM == 0 and seq % TM == 0 and (TM % n_ctx == 0 or n_ctx % TM == 0)
    assert n_ctx % S5_T == 0 and seq % (S5_T * SUBLANES * SUBLANES) == 0
    n_ctx_tiles = ctx_rows // TM
    n_seg_chunks = seq // (S5_T * SUBLANES)
    nt = n // TM

    z_lat, z_ctx, lat_off = x[0], jnp.pad(ctx[0], ((0, ctx_rows - n_ctx), (0, 0))), n_ctx_tiles
    cc = jnp.zeros((8, d), F32).at[0].set(c_ctx).at[1].set(c[0])
    mods = _ada(cc, w_ada, b_ada)[:, :2].reshape(depth, 2, 1, 6 * d)
    s5_m, s5_bp, s5_cp, s5_tab, s5_pow = _s5_prep(lam_re, lam_im, log_dt, b_re, b_im, c_re, c_im, d_skip, n_seg_chunks)

    for l in range(depth):
        pad = LANES - N_EXPERTS - N_EXP_GROUPS
        p = dict(
            g_mix=g_mix[l].reshape(1, d), g_ffn=g_ffn[l].reshape(1, d),
            w_in5=w_in[l][:, MIX_W:].astype(BF16), w_glu=w_glu[l].astype(BF16),
            sg_ln_g=sg_ln_g[l].reshape(1, MIX_W), sg_ln_b=sg_ln_b[l].reshape(1, MIX_W),
            w_sg=w_sg[l].astype(BF16),
            b_sg_full=jnp.repeat(b_sg[l].T, SG_GROUP_W, axis=1),
            w_conv=w_conv[l], w_branch=w_branch[l].astype(BF16), w_merge=w_merge[l].astype(BF16),
            b_merge=b_merge[l].reshape(1, N_BRANCH * d), w_o=w_o[l].astype(BF16),
            w_router=_split_bf16(jnp.pad(jnp.concatenate([w_exp[l], w_grp[l]], axis=1).T, ((0, pad), (0, 0)))),
            b_router=jnp.pad(jnp.concatenate([b_exp[l], b_grp[l]]), (0, pad)).reshape(LANES, 1),
        )
        mod = mods[l]

        last = l == depth - 1
        first_tile = n_ctx_tiles if last else 0
        if l == 0:
            ug = _u_proj(z_lat, z_ctx, lat_off, n, mod, p["g_mix"], w_in[l][:, :MIX_W].astype(BF16), n_ctx_tiles)
        yg = _s5_apply(ug, s5_m[l], s5_bp[l], s5_cp[l], s5_tab[l], s5_pow[l], n_ctx // S5_T, ctx_rows // S5_T)
        z1, xs, route, pc = _mixer(z_lat, z_ctx, lat_off, n, yg, mod, p, n_ctx_tiles, min(n_ctx, TM), first_tile)
        n_work = (nt - first_tile) * MOE_SLOTS // MOE_TE + N_EXPERTS
        src, owner = _moe_plan(pc[:, :N_EXPERTS, 0], pc[:, :N_EXPERTS, 1], n_work)
        yt = _moe_experts(xs, src, owner, w1, w3, w2, l)
        if last:
            z = _combine(yt, route, z1, mod, g_final.reshape(1, d), n_ctx_tiles - first_tile)
        else:
            nxt = (mods[l + 1], g_mix[l + 1].reshape(1, d), w_in[l + 1][:, :MIX_W].astype(BF16))
            z, ug = _combine(yt, route, z1, mod, p["g_ffn"], n_ctx_tiles, nxt)
        z_lat, z_ctx, lat_off = z, z, 0

    return z.reshape(bsz, seq, d)
```

```python
import functools

import jax
import jax.numpy as jnp
from jax import lax
from jax.experimental import pallas as pl
from jax.experimental.pallas import tpu as pltpu

F32 = jnp.float32
BF16 = jnp.bfloat16
HIGHEST = lax.Precision.HIGHEST

D_MODEL = 1024
GRID_W = 64
MIX_W = D_MODEL // 2
N_BRANCH = 3
S5_GROUP = 16
S5_GROUPS = MIX_W // S5_GROUP
S5_STATE = 64
SG_CHUNK = 128
SG_GROUPS = 4
SG_GROUP_W = MIX_W // SG_GROUPS
CONV_K = 3
N_EXP_GROUPS = 4
EXPERTS_PER_GROUP = 8
N_EXPERTS = N_EXP_GROUPS * EXPERTS_PER_GROUP
D_FF_EXPERT = D_MODEL // 4
NORM_EPS = 1e-6

LANES = 128
SUBLANES = 8
S5_T = 16
S5_CW = S5_T * S5_GROUP
S5_PITCH = S5_T + SUBLANES
S5_GB = 4
CTX_ALIGN = 512
TM = 512
MOE_BLK = 16
MOE_SLOTS = 2 * TM + N_EXPERTS * MOE_BLK
MOE_TE = 512
MOE_NB = MOE_TE // MOE_BLK
ROUTER_ROWS = 40
MOE_W = D_MODEL + LANES
PLAN_SLACK = 4
PLAN_UNROLL = 3
VMEM_LIMIT = 56 * 1024 * 1024


def _dot(a, b):
    return jnp.dot(a, b, preferred_element_type=F32)


def _sigmoid(x):
    return 0.5 * jnp.tanh(0.5 * x) + 0.5


def _split_bf16(w):
    hi = w.astype(BF16)
    return jnp.stack([hi, (w - hi.astype(F32)).astype(BF16)])


def _rms(x, g):
    return x * lax.rsqrt(jnp.mean(x * x, axis=-1, keepdims=True) + NORM_EPS) * g


def _block_transpose(xs):
    nblk = LANES // S5_GROUP
    assert len(xs) == nblk == 8
    blk = lax.broadcasted_iota(jnp.int32, xs[0].shape, 1) // S5_GROUP
    for k in range(3):
        dist = 1 << k
        hi_half = (blk & dist) != 0
        out = list(xs)
        for a in range(nblk):
            if a & dist == 0:
                p = a + dist
                out[a] = jnp.where(hi_half, pltpu.roll(xs[p], dist * S5_GROUP, 1), xs[a])
                out[p] = jnp.where(hi_half, xs[p], pltpu.roll(xs[a], LANES - dist * S5_GROUP, 1))
        xs = out
    return xs


def _const_spec(shape):
    nd = len(shape)
    return pl.BlockSpec(shape, lambda *_: (0,) * nd, pipeline_mode=pl.Buffered(1))


def _ada_kernel(cc_ref, w_ref, b_ref, o_ref):
    cc = cc_ref[...]
    act = cc * _sigmoid(cc)
    hi = act.astype(BF16)
    lo = (act - hi.astype(F32)).astype(BF16)
    w = w_ref[...].astype(BF16)
    o_ref[...] = _dot(hi, w) + _dot(lo, w) + b_ref[...]


def _ada(cc, w_ada, b_ada):
    depth = w_ada.shape[0]
    nblk = 6
    return pl.pallas_call(
        _ada_kernel,
        grid=(depth, nblk),
        in_specs=[
            pl.BlockSpec((8, D_MODEL), lambda l, j: (0, 0)),
            pl.BlockSpec((None, D_MODEL, D_MODEL), lambda l, j: (l, 0, j)),
            pl.BlockSpec((None, 1, D_MODEL), lambda l, j: (l, 0, j)),
        ],
        out_specs=pl.BlockSpec((None, 8, D_MODEL), lambda l, j: (l, 0, j)),
        out_shape=jax.ShapeDtypeStruct((depth, 8, 6 * D_MODEL), F32),
        name="ada",
    )(cc, w_ada, b_ada.reshape(depth, 1, 6 * D_MODEL))


def _mod_spec(n_ctx_tiles, first_tile=0):
    return pl.BlockSpec((None, 1, 6 * D_MODEL), lambda i: (jnp.where(i + first_tile < n_ctx_tiles, 0, 1), 0, 0))


def _token_specs(n_ctx_tiles, lat_off, first_tile=0):
    lat = pl.BlockSpec((TM, D_MODEL), lambda i: (jnp.maximum(i + first_tile - lat_off, 0), 0))
    ctx = pl.BlockSpec((TM, D_MODEL), lambda i: (jnp.minimum(i + first_tile, n_ctx_tiles - 1), 0))
    return lat, ctx


def _u_kernel(n_ctx_tiles, zl_ref, zc_ref, mod_ref, g_ref, w_ref, ug_ref, scr_ref):
    z = jnp.where(pl.program_id(0) < n_ctx_tiles, zc_ref[...], zl_ref[...])
    _u_group_major(z, mod_ref[...], g_ref[...], w_ref, ug_ref, scr_ref)


def _u_group_major(z, mod, g, w_ref, ug_ref, scr_ref):
    h = _rms(z, g) * (1.0 + mod[:, D_MODEL:2 * D_MODEL]) + mod[:, 0:D_MODEL]
    u = _dot(h.astype(BF16), w_ref[...])
    n_chunks = TM // S5_T
    for q in range(MIX_W // LANES):
        for c in range(n_chunks):
            scr_ref[q, c * S5_PITCH:c * S5_PITCH + S5_T, :] = u[c * S5_T:(c + 1) * S5_T, q * LANES:(q + 1) * LANES]
    for q in range(MIX_W // LANES):
        for half in range(S5_T // SUBLANES):
            rows = [scr_ref[q, pl.ds(half * SUBLANES + a, n_chunks, stride=S5_PITCH), :] for a in range(SUBLANES)]
            for b, y in enumerate(_block_transpose(rows)):
                ug_ref[q * SUBLANES + b, :, half * LANES:(half + 1) * LANES] = y.astype(BF16)


def _u_proj(z_lat, z_ctx, lat_off, n, mod, g_mix, w_u, n_ctx_tiles):
    return pl.pallas_call(
        functools.partial(_u_kernel, n_ctx_tiles),
        grid=(n // TM,),
        in_specs=[
            *_token_specs(n_ctx_tiles, lat_off),
            _mod_spec(n_ctx_tiles),
            _const_spec((1, D_MODEL)),
            _const_spec((D_MODEL, MIX_W)),
        ],
        out_specs=pl.BlockSpec((S5_GROUPS, TM // S5_T, S5_CW), lambda i: (0, i, 0)),
        out_shape=jax.ShapeDtypeStruct((S5_GROUPS, n // S5_T, S5_CW), BF16),
        scratch_shapes=[pltpu.VMEM((MIX_W // LANES, TM // S5_T * S5_PITCH, LANES), F32)],
        name="u_proj",
    )(z_lat, z_ctx, mod, g_mix, w_u)


def _s5_prep_kernel(n_seg_chunks, par_ref, bt_ref, ct_ref, dsk_ref, m_ref, bp_ref, cp_ref, tab_ref, pow_ref):
    t = S5_T
    par = par_ref[...]
    lo = lax.broadcasted_iota(jnp.int32, (1, LANES), 1) < S5_STATE
    sgn = jnp.where(lo, -1.0, 1.0)

    def rows(n):
        return lax.broadcasted_iota(jnp.int32, (n, 1), 0).astype(F32)

    def expand(pw_rows, vec_rows):
        pr, pi = pw_rows
        vr, vi = vec_rows
        pr, pi = pr[:, None, :], pi[:, None, :]
        vr, vi = vr[None, :, :], vi[None, :, :]
        re = (pr * vr - pi * vi).reshape(-1, LANES)
        im = (pr * vi + pi * vr).reshape(-1, LANES)
        return re, im

    mt = None
    bp_cols, cp_rows, tab_rows, pow_rows = [], [], [], []
    for d in range(2):
        lre, lim = par[2 * d:2 * d + 1], par[2 * d + 1:2 * d + 2]
        dt = jnp.exp(par[4 + d:5 + d])
        xr, xi = lre * dt, lim * dt

        def cpow(m):
            mag = jnp.exp(m * xr)
            return mag * jnp.cos(m * xi), mag * jnp.sin(m * xi)

        ar, ai = cpow(1.0)
        den = lre * lre + lim * lim
        qr = ((ar - 1.0) * lre + ai * lim) / den
        qi = (ai * lre - (ar - 1.0) * lim) / den
        br, bi = bt_ref[2 * d], bt_ref[2 * d + 1]
        bb = (qr * br - qi * bi, qr * bi + qi * br)
        cc = (ct_ref[2 * d], ct_ref[2 * d + 1])

        inj_pow = cpow(t - 1.0 - rows(t)) if d == 0 else cpow(rows(t))
        re, im = expand(inj_pow, bb)
        bp_cols.append(jnp.where(lo, re, im))

        out_pow = cpow(rows(t) + 1.0) if d == 0 else cpow(t - rows(t))
        re, im = expand(out_pow, cc)
        cp_rows.append(jnp.where(lo, re, -im).T)

        lag_pow = cpow(rows(t)) if d == 0 else cpow(t - 1.0 - rows(t))
        re, im = expand(lag_pow, cc)
        ctab = jnp.where(lo, re, -im)
        bcat = jnp.broadcast_to(jnp.where(lo, bb[0], bb[1])[None], (t, S5_GROUP, LANES)).reshape(-1, LANES)
        k256 = lax.dot_general(ctab, bcat, (((1,), (1,)), ((), ())), precision=HIGHEST,
                               preferred_element_type=F32)
        col_blk = lax.broadcasted_iota(jnp.int32, (S5_CW, S5_CW), 1) // S5_GROUP
        acc = jnp.zeros((S5_CW, S5_CW), F32)
        for r in range(t):
            sh = r * S5_GROUP if d == 0 else (t - 1 - r) * S5_GROUP
            if sh == 0:
                shifted = k256
            elif d == 0:
                shifted = jnp.concatenate([jnp.zeros((sh, S5_CW), F32), k256[:S5_CW - sh]], axis=0)
            else:
                shifted = jnp.concatenate([k256[sh:], jnp.zeros((sh, S5_CW), F32)], axis=0)
            acc = acc + jnp.where(col_blk == r, shifted, 0.0)
        mt = acc if mt is None else mt + acc

        for m in (float(t), float(t * n_seg_chunks)):
            re, im = cpow(m)
            tab_rows += [re, sgn * im]
        blk, j = rows(n_seg_chunks // SUBLANES), rows(SUBLANES)
        if d == 0:
            re, im = expand(cpow(t * SUBLANES * blk), cpow(t * j))
        else:
            re, im = expand(cpow(t * SUBLANES * (n_seg_chunks // SUBLANES - 1.0 - blk)), cpow(t * (SUBLANES - 1.0 - j)))
        pow_rows += [re, sgn * im]

    eye = (lax.broadcasted_iota(jnp.int32, (S5_CW, S5_CW), 0) == lax.broadcasted_iota(jnp.int32, (S5_CW, S5_CW), 1))
    mt = mt + jnp.where(eye, dsk_ref[...], 0.0)
    m_ref[...] = mt.T.astype(BF16)
    bp_ref[...] = jnp.concatenate(bp_cols, axis=1).astype(BF16)
    cp_ref[...] = jnp.concatenate(cp_rows, axis=0).astype(BF16)
    tab_ref[...] = jnp.concatenate(tab_rows, axis=0)
    for q in range(4):
        pow_ref[q] = pow_rows[q]


def _s5_prep(lam_re, lam_im, log_dt, b_re, b_im, c_re, c_im, d_skip, n_seg_chunks):
    depth = lam_re.shape[0]
    g, p_, h = S5_GROUPS, S5_STATE, S5_GROUP
    dup = lambda a: jnp.concatenate([a, a], axis=-1)
    dirs_last = lambda a: jnp.moveaxis(a, 1, 2)
    lam = jnp.stack([lam_re, lam_im], axis=3)
    lam = dirs_last(lam).reshape(depth, g, 4, p_)
    ldt = jnp.broadcast_to(dirs_last(log_dt[..., None]), (depth, g, 2, p_))
    par = dup(jnp.concatenate([lam, ldt, jnp.zeros((depth, g, 2, p_), F32)], axis=2))
    bt = jnp.stack([b_re, b_im], axis=3)
    bt = dup(jnp.swapaxes(dirs_last(bt), -1, -2).reshape(depth, g, 4, h, p_))
    ct = dup(dirs_last(jnp.stack([c_re, c_im], axis=3)).reshape(depth, g, 4, h, p_))
    dsk = jnp.tile(d_skip.reshape(depth, g, 1, h), (1, 1, 1, S5_T))

    blk = lambda *shape: pl.BlockSpec((None, None) + shape, lambda l, j: (l, j) + (0,) * len(shape))
    return pl.pallas_call(
        functools.partial(_s5_prep_kernel, n_seg_chunks),
        grid=(depth, g),
        in_specs=[blk(8, LANES), blk(4, h, LANES), blk(4, h, LANES), blk(1, S5_CW)],
        out_specs=[blk(S5_CW, S5_CW), blk(S5_CW, 2 * LANES), blk(2 * LANES, S5_CW), blk(8, LANES),
                   blk(4, n_seg_chunks, LANES)],
        out_shape=[
            jax.ShapeDtypeStruct((depth, g, S5_CW, S5_CW), BF16),
            jax.ShapeDtypeStruct((depth, g, S5_CW, 2 * LANES), BF16),
            jax.ShapeDtypeStruct((depth, g, 2 * LANES, S5_CW), BF16),
            jax.ShapeDtypeStruct((depth, g, 8, LANES), F32),
            jax.ShapeDtypeStruct((depth, g, 4, n_seg_chunks, LANES), F32),
        ],
        name="s5_prep",
    )(par, bt, ct, dsk)


def _cmul_step(s, s_sw, v, v_sw, re, ims):
    return re * s + ims * s_sw + v, re * s_sw - ims * s + v_sw


def _s5_kernel(n_ctx_chunks, n_lat0, ug_ref, m_ref, bp_ref, cp_ref, tab_ref, pow_ref, y_ref,
               vf_ref, vfs_ref, vb_ref, vbs_ref):
    gb, n_rows, _ = ug_ref.shape
    n_seg = (n_rows - n_lat0) // SUBLANES
    pitch = vf_ref.shape[1] // SUBLANES
    half = S5_STATE

    def put_segments(ref, g, val):
        for r in range(SUBLANES):
            ref[g, r * pitch:r * pitch + n_seg, :] = val[r * n_seg:(r + 1) * n_seg]

    tab = tab_ref[...]
    ctx_end = []
    sp_ctx = []
    for g in range(gb):
        v = _dot(ug_ref[g], bp_ref[g])
        vl = v[n_lat0:]
        put_segments(vf_ref, g, vl[:, :LANES])
        put_segments(vb_ref, g, vl[:, LANES:])
        put_segments(vfs_ref, g, pltpu.roll(vl[:, :LANES], half, 1))
        put_segments(vbs_ref, g, pltpu.roll(vl[:, LANES:], half, 1))

        vc = v[:n_ctx_chunks]
        vcs = jnp.concatenate([pltpu.roll(vc[:, :LANES], half, 1), pltpu.roll(vc[:, LANES:], half, 1)], axis=1)
        zero = jnp.zeros((1, LANES), F32)
        sf, sfs, sb, sbs = zero, zero, zero, zero
        rows_f, rows_b = [], [None] * n_ctx_chunks
        for j in range(n_ctx_chunks):
            jb = n_ctx_chunks - 1 - j
            rows_f.append(sf)
            rows_b[jb] = sb
            sf, sfs = _cmul_step(sf, sfs, vc[j:j + 1, :LANES], vcs[j:j + 1, :LANES], tab[g, 0:1], tab[g, 1:2])
            sb, sbs = _cmul_step(sb, sbs, vc[jb:jb + 1, LANES:], vcs[jb:jb + 1, LANES:], tab[g, 4:5], tab[g, 5:6])
        ctx_end.append((sf, sfs, sb, sbs))
        sp_ctx.append(jnp.concatenate([jnp.concatenate(rows_f, axis=0), jnp.concatenate(rows_b, axis=0)], axis=1))

    coef = [[jnp.broadcast_to(tab[g, q:q + 1], (SUBLANES, LANES)) for q in (0, 1, 4, 5)] for g in range(gb)]

    def step(k, carry):
        kb = n_seg - 1 - k
        out = []
        for g in range(gb):
            sf, sfs, sb, sbs = carry[g]
            are_f, aim_f, are_b, aim_b = coef[g]
            at_f = pl.ds(k, SUBLANES, stride=pitch)
            at_b = pl.ds(kb, SUBLANES, stride=pitch)
            v_f, v_fs = vf_ref[g, at_f, :], vfs_ref[g, at_f, :]
            v_b, v_bs = vb_ref[g, at_b, :], vbs_ref[g, at_b, :]
            vf_ref[g, at_f, :] = sf
            vb_ref[g, at_b, :] = sb
            sf, sfs = _cmul_step(sf, sfs, v_f, v_fs, are_f, aim_f)
            sb, sbs = _cmul_step(sb, sbs, v_b, v_bs, are_b, aim_b)
            out.append((sf, sfs, sb, sbs))
        return tuple(out)

    zero8 = jnp.zeros((SUBLANES, LANES), F32)
    ends = lax.fori_loop(0, n_seg, step, tuple((zero8, zero8, zero8, zero8) for _ in range(gb)))

    for g in range(gb):
        ef, efs, eb, ebs = ends[g]
        sf, sfs, sb, sbs = ctx_end[g]
        ent_f, ent_b = [], [None] * SUBLANES
        for r in range(SUBLANES):
            rb = SUBLANES - 1 - r
            ent_f.append((sf, sfs))
            ent_b[rb] = (sb, sbs)
            sf, sfs = _cmul_step(sf, sfs, ef[r:r + 1], efs[r:r + 1], tab[g, 2:3], tab[g, 3:4])
            sb, sbs = _cmul_step(sb, sbs, eb[rb:rb + 1], ebs[rb:rb + 1], tab[g, 6:7], tab[g, 7:8])

        segs = []
        for r in range(SUBLANES):
            seg_rows = slice(r * pitch, r * pitch + n_seg)
            f = vf_ref[g, seg_rows, :] + pow_ref[g, 0] * ent_f[r][0] + pow_ref[g, 1] * ent_f[r][1]
            b = vb_ref[g, seg_rows, :] + pow_ref[g, 2] * ent_b[r][0] + pow_ref[g, 3] * ent_b[r][1]
            segs.append(jnp.concatenate([f, b], axis=1))
        pad = jnp.zeros((n_lat0 - n_ctx_chunks, 2 * LANES), F32)
        sprev = jnp.concatenate([sp_ctx[g], pad] + segs, axis=0).astype(BF16)
        y_ref[g] = _dot(ug_ref[g], m_ref[g]) + _dot(sprev, cp_ref[g])


def _s5_apply(ug, m, bp, cp, tab, pw, n_ctx_chunks, n_lat0):
    n_groups, n_rows, cw = ug.shape
    n_seg = pw.shape[-2]
    assert n_rows - n_lat0 == SUBLANES * n_seg and n_lat0 % SUBLANES == 0
    blk = lambda *shape: pl.BlockSpec((S5_GB,) + shape, lambda i: (i,) + (0,) * len(shape))
    seg = pltpu.VMEM((S5_GB, SUBLANES * (n_seg + SUBLANES), LANES), F32)
    return pl.pallas_call(
        functools.partial(_s5_kernel, n_ctx_chunks, n_lat0),
        grid=(n_groups // S5_GB,),
        in_specs=[blk(n_rows, cw), blk(cw, cw), blk(cw, 2 * LANES), blk(2 * LANES, cw), blk(8, LANES),
                  blk(4, n_seg, LANES)],
        out_specs=blk(n_rows, cw),
        out_shape=jax.ShapeDtypeStruct((n_groups, n_rows, cw), F32),
        scratch_shapes=[seg, seg, seg, seg],
        compiler_params=pltpu.CompilerParams(vmem_limit_bytes=VMEM_LIMIT),
        name="s5_scan",
    )(ug, m, bp, cp, tab, pw)


def _mixer_kernel(n_ctx_tiles, ctx_row_len, first_tile, zl_ref, zc_ref, y_ref, mod_ref, gmix_ref, gffn_ref, win_ref, wglu_ref,
                  lng_ref, lnb_ref, wsg_ref, bsg_ref, wconv_ref, wbr_ref, wmg_ref, bmg_ref, wo_ref, wr_ref,
                  br_ref, z1_ref, xs_ref, route_ref, pc_ref, yscr_ref):
    d = D_MODEL
    n_chunks = TM // S5_T
    for q in range(MIX_W // LANES):
        for half in range(S5_T // SUBLANES):
            cols = [y_ref[q * SUBLANES + b, :, half * LANES:(half + 1) * LANES] for b in range(SUBLANES)]
            for a, rows in enumerate(_block_transpose(cols)):
                yscr_ref[q, pl.ds(half * SUBLANES + a, n_chunks, stride=S5_PITCH), :] = rows
    y_tok = jnp.concatenate(
        [jnp.concatenate([yscr_ref[q, c * S5_PITCH:c * S5_PITCH + S5_T, :] for c in range(n_chunks)], axis=0)
         for q in range(MIX_W // LANES)], axis=1)
    is_ctx = pl.program_id(0) + first_tile < n_ctx_tiles
    z = jnp.where(is_ctx, zc_ref[...], zl_ref[...])
    mod = mod_ref[...]
    sh1, sc1, ga1 = mod[:, 0:d], mod[:, d:2 * d], mod[:, 2 * d:3 * d]
    sh2, sc2 = mod[:, 3 * d:4 * d], mod[:, 4 * d:5 * d]
    h = _rms(z, gmix_ref[...]) * (1.0 + sc1) + sh1
    hb = h.astype(BF16)
    proj = _dot(hb, win_ref[...])
    su, sv, cgb, cgc, chh = [proj[:, k * MIX_W:(k + 1) * MIX_W] for k in range(5)]

    def merge_term(k, y_k):
        gate = _sigmoid(_dot(hb, wmg_ref[:, k * d:(k + 1) * d]) + bmg_ref[:, k * d:(k + 1) * d])
        return gate * _dot(y_k.astype(BF16), wbr_ref[k])

    yg = jax.nn.gelu(y_tok)
    merged = merge_term(0, yg * _sigmoid(_dot(yg.astype(BF16), wglu_ref[...])))

    gv = jax.nn.gelu(sv)
    mu = jnp.mean(gv, axis=-1, keepdims=True)
    var = jnp.mean(jnp.square(gv - mu), axis=-1, keepdims=True)
    vn = ((gv - mu) * lax.rsqrt(var + NORM_EPS) * lng_ref[...] + lnb_ref[...]).astype(BF16)
    bsg = bsg_ref[...]
    chunks = []
    for n in range(TM // SG_CHUNK):
        rows = slice(n * SG_CHUNK, (n + 1) * SG_CHUNK)
        cols = [_dot(wsg_ref[g], vn[rows, g * SG_GROUP_W:(g + 1) * SG_GROUP_W]) for g in range(SG_GROUPS)]
        chunks.append(jnp.concatenate(cols, axis=1) + bsg)
    merged = merged + merge_term(1, jax.nn.gelu(su) * jnp.concatenate(chunks, axis=0))

    zc = cgc * chh
    row = lax.broadcasted_iota(jnp.int32, (TM, 1), 0)
    pos = jnp.where(is_ctx, row % ctx_row_len, row % GRID_W)
    prev = jnp.where(pos == 0, 0.0, pltpu.roll(zc, 1, 0))
    nxt = jnp.where(pos == jnp.where(is_ctx, ctx_row_len - 1, GRID_W - 1), 0.0, pltpu.roll(zc, TM - 1, 0))
    wc = wconv_ref[...]
    merged = merged + merge_term(2, cgb * (wc[0:1] * prev + wc[1:2] * zc + wc[2:3] * nxt))

    z1 = z + ga1 * _dot(merged.astype(BF16), wo_ref[...])
    z1_ref[...] = z1

    h2 = _rms(z1, gffn_ref[...]) * (1.0 + sc2) + sh2
    h2_hi = h2.astype(BF16)
    h2_lo = (h2 - h2_hi.astype(F32)).astype(BF16)
    nt_dims = (((1,), (1,)), ((), ()))
    logits_t = (lax.dot_general(wr_ref[0], h2_hi, nt_dims, preferred_element_type=F32)
                + lax.dot_general(wr_ref[0], h2_lo, nt_dims, preferred_element_type=F32)
                + lax.dot_general(wr_ref[1], h2_hi, nt_dims, preferred_element_type=F32)) + br_ref[...]
    lg = logits_t[0:ROUTER_ROWS]
    row = lax.broadcasted_iota(jnp.int32, (ROUTER_ROWS, TM), 0)
    big = jnp.int32(LANES)
    neg = jnp.float32(-jnp.inf)
    top = lambda v: jnp.max(v, axis=0, keepdims=True)
    total = lambda v: jnp.sum(v, axis=0, keepdims=True)
    first = lambda m: jnp.min(jnp.where(m, row, big), axis=0, keepdims=True)
    is_grp = (row >= N_EXPERTS) & (row < N_EXPERTS + N_EXP_GROUPS)
    gl = jnp.where(is_grp, lg, neg)
    ge = jnp.exp(gl - top(gl))
    gp = ge / total(ge)
    g_p = top(gp)
    g_idx = first(is_grp & (gp == g_p)) - N_EXPERTS
    in_grp = (row >= g_idx * EXPERTS_PER_GROUP) & (row < (g_idx + 1) * EXPERTS_PER_GROUP)
    el = jnp.where(in_grp, lg, neg)
    ee = jnp.exp(el - top(el))
    ep = ee / total(ee)
    p1 = top(ep)
    i1 = first(in_grp & (ep == p1))
    rest = in_grp & (row != i1)
    ep2 = jnp.where(rest, ep, -1.0)
    p2 = top(ep2)
    i2 = first(rest & (ep2 == p2))
    tot = p1 + p2
    w_1, w_2 = g_p * (p1 / tot), g_p * (p2 / tot)

    oh1, oh2 = row == i1, row == i2
    oh = jnp.where(oh1 | oh2, 1.0, 0.0)
    cnt = jnp.sum(oh, axis=1, keepdims=True)
    pcb = jnp.floor((cnt + (MOE_BLK - 1)) * (1.0 / MOE_BLK))
    er, ec = (lax.broadcasted_iota(jnp.int32, (ROUTER_ROWS, ROUTER_ROWS), k) for k in (0, 1))
    before = jnp.where(ec < er, 1.0, 0.0).astype(BF16)
    startb = _dot(before, jnp.broadcast_to(pcb, (ROUTER_ROWS, LANES)).astype(BF16))[:, 0:1]
    start = startb * MOE_BLK
    plan_lane = lax.broadcasted_iota(jnp.int32, (ROUTER_ROWS, LANES), 1)
    pc_ref[...] = jnp.where(plan_lane == 0, pcb, jnp.where(plan_lane == 1, startb, 0.0)).astype(jnp.int32)
    tr, tc = (lax.broadcasted_iota(jnp.int32, (TM, TM), k) for k in (0, 1))
    earlier = jnp.where(tr < tc, 1.0, 0.0).astype(BF16)
    slot = start + _dot(oh.astype(BF16), earlier)
    slot1 = total(jnp.where(oh1, slot, 0.0))
    slot2 = total(jnp.where(oh2, slot, 0.0))

    def token_rows(vals):
        r8 = lax.broadcasted_iota(jnp.int32, (SUBLANES, TM), 0)
        rows8 = jnp.zeros((SUBLANES, TM), F32)
        for k, v in enumerate(vals):
            rows8 = jnp.where(r8 == k, v, rows8)
        return jnp.concatenate([rows8, jnp.zeros((LANES - SUBLANES, TM), F32)], axis=0).T

    route_ref[...] = token_rows([slot1, slot2])
    sl = xs_ref.shape[0]
    slot_row = lax.broadcasted_iota(jnp.int32, (sl, TM), 0)
    perm1 = jnp.where(slot_row == slot1.astype(jnp.int32), 1.0, 0.0).astype(BF16)
    perm2 = jnp.where(slot_row == slot2.astype(jnp.int32), 1.0, 0.0).astype(BF16)
    xs_ref[:, 0:d] = _dot(perm1 + perm2, h2_hi).astype(BF16)

    def weight_lanes(w):
        hi = w.astype(BF16).astype(F32)
        return token_rows([hi, w - hi]).astype(BF16)

    xs_ref[:, d:] = (_dot(perm1, weight_lanes(w_1)) + _dot(perm2, weight_lanes(w_2))).astype(BF16)


def _mixer(z_lat, z_ctx, lat_off, n, yg, mod, p, n_ctx_tiles, ctx_row_len, first_tile):
    d = D_MODEL
    nt = n // TM - first_tile
    tile = lambda w: pl.BlockSpec((TM, w), lambda i: (i, 0))
    return pl.pallas_call(
        functools.partial(_mixer_kernel, n_ctx_tiles, ctx_row_len, first_tile),
        grid=(nt,),
        in_specs=[
            *_token_specs(n_ctx_tiles, lat_off, first_tile),
            pl.BlockSpec((S5_GROUPS, TM // S5_T, S5_CW), lambda i: (0, i + first_tile, 0)),
            _mod_spec(n_ctx_tiles, first_tile),
            _const_spec((1, d)), _const_spec((1, d)),
            _const_spec((d, 5 * MIX_W)), _const_spec((MIX_W, MIX_W)),
            _const_spec((1, MIX_W)), _const_spec((1, MIX_W)),
            _const_spec((SG_GROUPS, SG_CHUNK, SG_CHUNK)), _const_spec((SG_CHUNK, MIX_W)),
            _const_spec((CONV_K, MIX_W)), _const_spec((N_BRANCH, MIX_W, d)),
            _const_spec((d, N_BRANCH * d)), _const_spec((1, N_BRANCH * d)),
            _const_spec((d, d)), _const_spec((2, LANES, d)), _const_spec((LANES, 1)),
        ],
        out_specs=[tile(d), pl.BlockSpec((None, MOE_SLOTS, MOE_W), lambda i: (i, 0, 0)), tile(LANES),
                   pl.BlockSpec((None, ROUTER_ROWS, LANES), lambda i: (i, 0, 0))],
        out_shape=[
            jax.ShapeDtypeStruct((nt * TM, d), F32),
            jax.ShapeDtypeStruct((nt, MOE_SLOTS, MOE_W), BF16),
            jax.ShapeDtypeStruct((nt * TM, LANES), F32),
            jax.ShapeDtypeStruct((nt, ROUTER_ROWS, LANES), jnp.int32),
        ],
        scratch_shapes=[pltpu.VMEM((MIX_W // LANES, TM // S5_T * S5_PITCH, LANES), F32)],
        compiler_params=pltpu.CompilerParams(vmem_limit_bytes=VMEM_LIMIT),
        name="mixer",
    )(z_lat, z_ctx, yg, mod, p["g_mix"], p["g_ffn"], p["w_in5"], p["w_glu"], p["sg_ln_g"], p["sg_ln_b"],
      p["w_sg"], p["b_sg_full"], p["w_conv"], p["w_branch"], p["w_merge"], p["b_merge"], p["w_o"],
      p["w_router"], p["b_router"])


def _plan_kernel(nt, n_work, nb_ref, sb_ref, src_ref, exp_ref):
    bpt = MOE_SLOTS // MOE_BLK

    def no_block(b, c):
        src_ref[b] = -1
        return c

    def expert(e, g0):
        def tile(i, pos):
            nb = nb_ref[i, e]
            s0 = i * bpt + sb_ref[i, e]
            base = g0 * MOE_NB + pos

            def blk(b, c):
                src_ref[base + b] = s0 + b
                return c

            for b in range(PLAN_SLACK):
                src_ref[base + b] = s0 + b
            lax.fori_loop(PLAN_SLACK, nb, blk, 0)
            return pos + nb

        def tiles(j, pos):
            for k in range(PLAN_UNROLL):
                pos = tile(j * PLAN_UNROLL + k, pos)
            return pos

        pos = lax.fori_loop(0, nt // PLAN_UNROLL, tiles, 0)
        for i in range(nt - nt % PLAN_UNROLL, nt):
            pos = tile(i, pos)
        n_tiles = (pos + MOE_NB - 1) // MOE_NB
        lax.fori_loop(g0 * MOE_NB + pos, (g0 + n_tiles) * MOE_NB, no_block, 0)

        def owner(t, c):
            exp_ref[g0 + t] = e
            return c

        lax.fori_loop(0, n_tiles, owner, 0)
        return g0 + n_tiles

    g_end = lax.fori_loop(0, N_EXPERTS, expert, 0)
    lax.fori_loop(g_end * MOE_NB, n_work * MOE_NB + PLAN_SLACK, no_block, 0)

    def idle(g, c):
        exp_ref[g] = N_EXPERTS - 1
        return c

    lax.fori_loop(g_end, n_work, idle, 0)


def _moe_plan(nb, sb, n_work):
    nt = nb.shape[0]
    smem = pl.BlockSpec(memory_space=pltpu.SMEM)
    return pl.pallas_call(
        functools.partial(_plan_kernel, nt, n_work),
        in_specs=[smem, smem],
        out_specs=[smem, smem],
        out_shape=[jax.ShapeDtypeStruct((n_work * MOE_NB + PLAN_SLACK,), jnp.int32),
                   jax.ShapeDtypeStruct((n_work,), jnp.int32)],
        name="moe_plan",
    )(nb, sb)


def _expert_kernel(src_ref, exp_ref, xs_hbm, w1_ref, w3_ref, w2_ref, yt_hbm, lhs_ref, out_ref, wb1_ref, wb3_ref,
                   wb2_ref, sem_in, sem_out):
    g = pl.program_id(0)
    n_work = pl.num_programs(0)

    def first(step):
        return src_ref[step * MOE_NB]

    def last(step):
        return src_ref[step * MOE_NB + MOE_NB - 1]

    def gather_copy(step, b):
        return pltpu.make_async_copy(xs_hbm.at[src_ref[step * MOE_NB + b]], lhs_ref.at[step % 2, b],
                                     sem_in.at[step % 2])

    def scatter_copy(step, b):
        return pltpu.make_async_copy(out_ref.at[step % 2, b], yt_hbm.at[src_ref[step * MOE_NB + b]],
                                     sem_out.at[step % 2])

    def gather_all(step):
        return pltpu.make_async_copy(xs_hbm.at[pl.ds(0, MOE_NB)], lhs_ref.at[step % 2], sem_in.at[step % 2])

    def scatter_all(step):
        return pltpu.make_async_copy(out_ref.at[step % 2], yt_hbm.at[pl.ds(0, MOE_NB)], sem_out.at[step % 2])

    def per_block(step, fn):
        def body(b, c):
            @pl.when(src_ref[step * MOE_NB + b] >= 0)
            def _():
                fn(b)
            return c
        lax.fori_loop(0, MOE_NB, body, 0)

    def start(step, copy):
        @pl.when(last(step) >= 0)
        def _():
            for b in range(MOE_NB):
                copy(step, b).start()

        @pl.when((last(step) < 0) & (first(step) >= 0))
        def _():
            per_block(step, lambda b: copy(step, b).start())

    def wait(step, copy, copy_all):
        @pl.when(last(step) >= 0)
        def _():
            copy_all(step).wait()

        @pl.when((last(step) < 0) & (first(step) >= 0))
        def _():
            per_block(step, lambda b: copy(step, b).wait())

    @pl.when(g == 0)
    def _():
        start(g, gather_copy)

    @pl.when(g + 1 < n_work)
    def _():
        start(g + 1, gather_copy)

    wait(g, gather_copy, gather_all)

    @pl.when(g >= 2)
    def _():
        wait(g - 2, scatter_copy, scatter_all)

    slot = g % 2

    @pl.when(first(g) >= 0)
    def _():
        @pl.when((g == 0) | (exp_ref[g] != exp_ref[jnp.maximum(g - 1, 0)]))
        def _():
            wb1_ref[...] = w1_ref[...].astype(BF16)
            wb3_ref[...] = w3_ref[...].astype(BF16)
            wb2_ref[...] = w2_ref[...].astype(BF16)

        @pl.when(last(g) < 0)
        def _():
            def zero_missing(b, c):
                @pl.when(src_ref[g * MOE_NB + b] < 0)
                def _():
                    lhs_ref[slot, b] = jnp.zeros((MOE_BLK, MOE_W), BF16)
                return c
            lax.fori_loop(0, MOE_NB, zero_missing, 0)

        rows = lhs_ref[slot].reshape(MOE_TE, MOE_W)
        x = rows[:, :D_MODEL]
        w_lanes = rows[:, D_MODEL:].astype(F32)
        w_row = w_lanes[:, 0:1] + w_lanes[:, 1:2]
        a = _dot(x, wb1_ref[...])
        act = a * _sigmoid(a) * _dot(x, wb3_ref[...])
        y = (_dot(act.astype(BF16), wb2_ref[...]) * w_row).astype(BF16)
        out_ref[slot] = jnp.concatenate([y, jnp.zeros((MOE_TE, LANES), BF16)], axis=1).reshape(MOE_NB, MOE_BLK, MOE_W)

    start(g, scatter_copy)

    @pl.when(g == n_work - 1)
    def _():
        @pl.when(g >= 1)
        def _():
            wait(g - 1, scatter_copy, scatter_all)
        wait(g, scatter_copy, scatter_all)


def _moe_experts(xs, src, owner, w1, w3, w2, layer):
    nt, sl, width = xs.shape
    d = D_MODEL
    n_work = owner.shape[0]
    xs_blocks = xs.reshape(nt * sl // MOE_BLK, MOE_BLK, width)
    any_spec = pl.BlockSpec(memory_space=pl.ANY)
    by_owner = lambda *shape: pl.BlockSpec((None, None) + shape, lambda g, src, own: (layer, own[g], 0, 0))
    buf = pltpu.VMEM((2, MOE_NB, MOE_BLK, width), BF16)
    yt = pl.pallas_call(
        _expert_kernel,
        grid_spec=pltpu.PrefetchScalarGridSpec(
            num_scalar_prefetch=2,
            grid=(n_work,),
            in_specs=[any_spec, by_owner(d, D_FF_EXPERT), by_owner(d, D_FF_EXPERT), by_owner(D_FF_EXPERT, d)],
            out_specs=any_spec,
            scratch_shapes=[
                buf, buf,
                pltpu.VMEM((d, D_FF_EXPERT), BF16), pltpu.VMEM((d, D_FF_EXPERT), BF16),
                pltpu.VMEM((D_FF_EXPERT, d), BF16),
                pltpu.SemaphoreType.DMA((2,)), pltpu.SemaphoreType.DMA((2,)),
            ],
        ),
        out_shape=jax.ShapeDtypeStruct(xs_blocks.shape, BF16),
        input_output_aliases={2: 0},
        compiler_params=pltpu.CompilerParams(dimension_semantics=("arbitrary",), vmem_limit_bytes=VMEM_LIMIT),
        name="moe_experts",
    )(src, owner, xs_blocks, w1, w3, w2)
    return yt.reshape(nt, sl, width)


def _combine_kernel(final, yt_ref, route_ref, z1_ref, mod_ref, g_ref, *rest):
    o_ref = rest[0] if final else rest[3]
    route = route_ref[...]
    slot1, slot2 = route[:, 0:1], route[:, 1:2]
    slot_lane = lax.broadcasted_iota(jnp.int32, (TM, MOE_SLOTS), 1).astype(F32)
    both = jnp.where((slot_lane == slot1) | (slot_lane == slot2), 1.0, 0.0).astype(BF16)
    moe = _dot(both, yt_ref[:, 0:D_MODEL])
    ga2 = mod_ref[...][:, 5 * D_MODEL:6 * D_MODEL]
    z2 = z1_ref[...] + ga2 * moe
    if final:
        o_ref[...] = _rms(z2, g_ref[...])
    else:
        mod_next, gmix_next, wu_next, _, ug_ref, scr_ref = rest
        o_ref[...] = z2
        _u_group_major(z2, mod_next[...], gmix_next[...], wu_next, ug_ref, scr_ref)


def _combine(yt, route, z1, mod, g_final, n_ctx_tiles, next_layer=None):
    n, d = z1.shape
    final = next_layer is None
    tile = lambda w: pl.BlockSpec((TM, w), lambda i: (i, 0))
    in_specs = [pl.BlockSpec((None, MOE_SLOTS, MOE_W), lambda i: (i, 0, 0)), tile(LANES), tile(d),
                _mod_spec(n_ctx_tiles), _const_spec((1, d))]
    out_specs, out_shape, scratch = tile(d), jax.ShapeDtypeStruct((n, d), F32), []
    if not final:
        in_specs += [_mod_spec(n_ctx_tiles), _const_spec((1, d)), _const_spec((d, MIX_W))]
        out_specs = [out_specs, pl.BlockSpec((S5_GROUPS, TM // S5_T, S5_CW), lambda i: (0, i, 0))]
        out_shape = [out_shape, jax.ShapeDtypeStruct((S5_GROUPS, n // S5_T, S5_CW), BF16)]
        scratch = [pltpu.VMEM((MIX_W // LANES, TM // S5_T * S5_PITCH, LANES), F32)]
    return pl.pallas_call(
        functools.partial(_combine_kernel, final),
        grid=(n // TM,),
        in_specs=in_specs,
        out_specs=out_specs,
        out_shape=out_shape,
        scratch_shapes=scratch,
        compiler_params=pltpu.CompilerParams(vmem_limit_bytes=VMEM_LIMIT),
        name="combine",
    )(yt, route, z1, mod, g_final, *(next_layer or ()))


def kernel(x, c, ctx, c_ctx, w_ada, b_ada, g_mix, g_ffn, w_in, lam_re, lam_im, log_dt, b_re, b_im, c_re, c_im, d_skip, w_glu, sg_ln_g, sg_ln_b, w_sg, b_sg, w_conv, w_branch, w_merge, b_merge, w_o, w_grp, b_grp, w_exp, b_exp, w1, w3, w2, g_final):
    depth = w_ada.shape[0]
    bsz, seq, d = x.shape
    n_ctx = ctx.shape[1]
    ctx_rows = -(-n_ctx // CTX_ALIGN) * CTX_ALIGN
    n = ctx_rows + seq
    assert bsz == 1 and d == D_MODEL
    assert ctx_rows % TM == 0 and seq % TM == 0 and (TM % n_ctx == 0 or n_ctx % TM == 0)
    assert n_ctx % S5_T == 0 and seq % (S5_T * SUBLANES * SUBLANES) == 0
    n_ctx_tiles = ctx_rows // TM
    n_seg_chunks = seq // (S5_T * SUBLANES)
    nt = n // TM

    z_lat, z_ctx, lat_off = x[0], jnp.pad(ctx[0], ((0, ctx_rows - n_ctx), (0, 0))), n_ctx_tiles
    cc = jnp.zeros((8, d), F32).at[0].set(c_ctx).at[1].set(c[0])
    mods = _ada(cc, w_ada, b_ada)[:, :2].reshape(depth, 2, 1, 6 * d)
    s5_m, s5_bp, s5_cp, s5_tab, s5_pow = _s5_prep(lam_re, lam_im, log_dt, b_re, b_im, c_re, c_im, d_skip, n_seg_chunks)

    for l in range(depth):
        pad = LANES - N_EXPERTS - N_EXP_GROUPS
        p = dict(
            g_mix=g_mix[l].reshape(1, d), g_ffn=g_ffn[l].reshape(1, d),
            w_in5=w_in[l][:, MIX_W:].astype(BF16), w_glu=w_glu[l].astype(BF16),
            sg_ln_g=sg_ln_g[l].reshape(1, MIX_W), sg_ln_b=sg_ln_b[l].reshape(1, MIX_W),
            w_sg=w_sg[l].astype(BF16),
            b_sg_full=jnp.repeat(b_sg[l].T, SG_GROUP_W, axis=1),
            w_conv=w_conv[l], w_branch=w_branch[l].astype(BF16), w_merge=w_merge[l].astype(BF16),
            b_merge=b_merge[l].reshape(1, N_BRANCH * d), w_o=w_o[l].astype(BF16),
            w_router=_split_bf16(jnp.pad(jnp.concatenate([w_exp[l], w_grp[l]], axis=1).T, ((0, pad), (0, 0)))),
            b_router=jnp.pad(jnp.concatenate([b_exp[l], b_grp[l]]), (0, pad)).reshape(LANES, 1),
        )
        mod = mods[l]

        last = l == depth - 1
        first_tile = n_ctx_tiles if last else 0
        if l == 0:
            ug = _u_proj(z_lat, z_ctx, lat_off, n, mod, p["g_mix"], w_in[l][:, :MIX_W].astype(BF16), n_ctx_tiles)
        yg = _s5_apply(ug, s5_m[l], s5_bp[l], s5_cp[l], s5_tab[l], s5_pow[l], n_ctx // S5_T, ctx_rows // S5_T)
        z1, xs, route, pc = _mixer(z_lat, z_ctx, lat_off, n, yg, mod, p, n_ctx_tiles, min(n_ctx, TM), first_tile)
        n_work = (nt - first_tile) * MOE_SLOTS // MOE_TE + N_EXPERTS
        src, owner = _moe_plan(pc[:, :N_EXPERTS, 0], pc[:, :N_EXPERTS, 1], n_work)
        yt = _moe_experts(xs, src, owner, w1, w3, w2, l)
        if last:
            z = _combine(yt, route, z1, mod, g_final.reshape(1, d), n_ctx_tiles - first_tile)
        else:
            nxt = (mods[l + 1], g_mix[l + 1].reshape(1, d), w_in[l + 1][:, :MIX_W].astype(BF16))
            z, ug = _combine(yt, route, z1, mod, p["g_ffn"], n_ctx_tiles, nxt)
        z_lat, z_ctx, lat_off = z, z, 0

    return z.reshape(bsz, seq, d)
```

```python
import functools

import jax
import jax.numpy as jnp
from jax import lax
from jax.experimental import pallas as pl
from jax.experimental.pallas import tpu as pltpu

F32 = jnp.float32
BF16 = jnp.bfloat16
HIGHEST = lax.Precision.HIGHEST

D_MODEL = 1024
GRID_W = 64
MIX_W = D_MODEL // 2
N_BRANCH = 3
S5_GROUP = 16
S5_GROUPS = MIX_W // S5_GROUP
S5_STATE = 64
SG_CHUNK = 128
SG_GROUPS = 4
SG_GROUP_W = MIX_W // SG_GROUPS
CONV_K = 3
N_EXP_GROUPS = 4
EXPERTS_PER_GROUP = 8
N_EXPERTS = N_EXP_GROUPS * EXPERTS_PER_GROUP
D_FF_EXPERT = D_MODEL // 4
NORM_EPS = 1e-6

LANES = 128
SUBLANES = 8
S5_T = 16
S5_CW = S5_T * S5_GROUP
S5_PITCH = S5_T + SUBLANES
S5_GB = 4
CTX_ALIGN = 512
TM = 512
MOE_BLK = 16
MOE_SLOTS = 2 * TM + N_EXPERTS * MOE_BLK
MOE_TE = 512
MOE_NB = MOE_TE // MOE_BLK
ROUTER_ROWS = 40
MOE_W = D_MODEL + LANES
PLAN_SLACK = 4
PLAN_UNROLL = 3
VMEM_LIMIT = 56 * 1024 * 1024


def _dot(a, b):
    return jnp.dot(a, b, preferred_element_type=F32)


def _sigmoid(x):
    return 0.5 * jnp.tanh(0.5 * x) + 0.5


def _split_bf16(w):
    hi = w.astype(BF16)
    return jnp.stack([hi, (w - hi.astype(F32)).astype(BF16)])


def _rms(x, g):
    return x * lax.rsqrt(jnp.mean(x * x, axis=-1, keepdims=True) + NORM_EPS) * g


def _block_transpose(xs):
    nblk = LANES // S5_GROUP
    assert len(xs) == nblk == 8
    blk = lax.broadcasted_iota(jnp.int32, xs[0].shape, 1) // S5_GROUP
    for k in range(3):
        dist = 1 << k
        hi_half = (blk & dist) != 0
        out = list(xs)
        for a in range(nblk):
            if a & dist == 0:
                p = a + dist
                out[a] = jnp.where(hi_half, pltpu.roll(xs[p], dist * S5_GROUP, 1), xs[a])
                out[p] = jnp.where(hi_half, xs[p], pltpu.roll(xs[a], LANES - dist * S5_GROUP, 1))
        xs = out
    return xs


def _const_spec(shape):
    nd = len(shape)
    return pl.BlockSpec(shape, lambda *_: (0,) * nd, pipeline_mode=pl.Buffered(1))


def _ada_kernel(cc_ref, w_ref, b_ref, o_ref):
    cc = cc_ref[...]
    act = cc * _sigmoid(cc)
    hi = act.astype(BF16)
    lo = (act - hi.astype(F32)).astype(BF16)
    w = w_ref[...].astype(BF16)
    o_ref[...] = _dot(hi, w) + _dot(lo, w) + b_ref[...]


def _ada(cc, w_ada, b_ada):
    depth = w_ada.shape[0]
    nblk = 6
    return pl.pallas_call(
        _ada_kernel,
        grid=(depth, nblk),
        in_specs=[
            pl.BlockSpec((8, D_MODEL), lambda l, j: (0, 0)),
            pl.BlockSpec((None, D_MODEL, D_MODEL), lambda l, j: (l, 0, j)),
            pl.BlockSpec((None, 1, D_MODEL), lambda l, j: (l, 0, j)),
        ],
        out_specs=pl.BlockSpec((None, 8, D_MODEL), lambda l, j: (l, 0, j)),
        out_shape=jax.ShapeDtypeStruct((depth, 8, 6 * D_MODEL), F32),
        name="ada",
    )(cc, w_ada, b_ada.reshape(depth, 1, 6 * D_MODEL))


def _mod_spec(n_ctx_tiles, first_tile=0):
    return pl.BlockSpec((None, 1, 6 * D_MODEL), lambda i: (jnp.where(i + first_tile < n_ctx_tiles, 0, 1), 0, 0))


def _token_specs(n_ctx_tiles, lat_off, first_tile=0):
    lat = pl.BlockSpec((TM, D_MODEL), lambda i: (jnp.maximum(i + first_tile - lat_off, 0), 0))
    ctx = pl.BlockSpec((TM, D_MODEL), lambda i: (jnp.minimum(i + first_tile, n_ctx_tiles - 1), 0))
    return lat, ctx


def _u_kernel(n_ctx_tiles, zl_ref, zc_ref, mod_ref, g_ref, w_ref, ug_ref, scr_ref):
    z = jnp.where(pl.program_id(0) < n_ctx_tiles, zc_ref[...], zl_ref[...])
    _u_group_major(z, mod_ref[...], g_ref[...], w_ref, ug_ref, scr_ref)


def _u_group_major(z, mod, g, w_ref, ug_ref, scr_ref):
    h = _rms(z, g) * (1.0 + mod[:, D_MODEL:2 * D_MODEL]) + mod[:, 0:D_MODEL]
    u = _dot(h.astype(BF16), w_ref[...])
    n_chunks = TM // S5_T
    for q in range(MIX_W // LANES):
        for c in range(n_chunks):
            scr_ref[q, c * S5_PITCH:c * S5_PITCH + S5_T, :] = u[c * S5_T:(c + 1) * S5_T, q * LANES:(q + 1) * LANES]
    for q in range(MIX_W // LANES):
        for half in range(S5_T // SUBLANES):
            rows = [scr_ref[q, pl.ds(half * SUBLANES + a, n_chunks, stride=S5_PITCH), :] for a in range(SUBLANES)]
            for b, y in enumerate(_block_transpose(rows)):
                ug_ref[q * SUBLANES + b, :, half * LANES:(half + 1) * LANES] = y.astype(BF16)


def _u_proj(z_lat, z_ctx, lat_off, n, mod, g_mix, w_u, n_ctx_tiles):
    return pl.pallas_call(
        functools.partial(_u_kernel, n_ctx_tiles),
        grid=(n // TM,),
        in_specs=[
            *_token_specs(n_ctx_tiles, lat_off),
            _mod_spec(n_ctx_tiles),
            _const_spec((1, D_MODEL)),
            _const_spec((D_MODEL, MIX_W)),
        ],
        out_specs=pl.BlockSpec((S5_GROUPS, TM // S5_T, S5_CW), lambda i: (0, i, 0)),
        out_shape=jax.ShapeDtypeStruct((S5_GROUPS, n // S5_T, S5_CW), BF16),
        scratch_shapes=[pltpu.VMEM((MIX_W // LANES, TM // S5_T * S5_PITCH, LANES), F32)],
        name="u_proj",
    )(z_lat, z_ctx, mod, g_mix, w_u)


def _s5_prep_kernel(n_seg_chunks, par_ref, bt_ref, ct_ref, dsk_ref, m_ref, bp_ref, cp_ref, tab_ref, pow_ref):
    t = S5_T
    par = par_ref[...]
    lo = lax.broadcasted_iota(jnp.int32, (1, LANES), 1) < S5_STATE
    sgn = jnp.where(lo, -1.0, 1.0)

    def rows(n):
        return lax.broadcasted_iota(jnp.int32, (n, 1), 0).astype(F32)

    def expand(pw_rows, vec_rows):
        pr, pi = pw_rows
        vr, vi = vec_rows
        pr, pi = pr[:, None, :], pi[:, None, :]
        vr, vi = vr[None, :, :], vi[None, :, :]
        re = (pr * vr - pi * vi).reshape(-1, LANES)
        im = (pr * vi + pi * vr).reshape(-1, LANES)
        return re, im

    mt = None
    bp_cols, cp_rows, tab_rows, pow_rows = [], [], [], []
    for d in range(2):
        lre, lim = par[2 * d:2 * d + 1], par[2 * d + 1:2 * d + 2]
        dt = jnp.exp(par[4 + d:5 + d])
        xr, xi = lre * dt, lim * dt

        def cpow(m):
            mag = jnp.exp(m * xr)
            return mag * jnp.cos(m * xi), mag * jnp.sin(m * xi)

        ar, ai = cpow(1.0)
        den = lre * lre + lim * lim
        qr = ((ar - 1.0) * lre + ai * lim) / den
        qi = (ai * lre - (ar - 1.0) * lim) / den
        br, bi = bt_ref[2 * d], bt_ref[2 * d + 1]
        bb = (qr * br - qi * bi, qr * bi + qi * br)
        cc = (ct_ref[2 * d], ct_ref[2 * d + 1])

        inj_pow = cpow(t - 1.0 - rows(t)) if d == 0 else cpow(rows(t))
        re, im = expand(inj_pow, bb)
        bp_cols.append(jnp.where(lo, re, im))

        out_pow = cpow(rows(t) + 1.0) if d == 0 else cpow(t - rows(t))
        re, im = expand(out_pow, cc)
        cp_rows.append(jnp.where(lo, re, -im).T)

        lag_pow = cpow(rows(t)) if d == 0 else cpow(t - 1.0 - rows(t))
        re, im = expand(lag_pow, cc)
        ctab = jnp.where(lo, re, -im)
        bcat = jnp.broadcast_to(jnp.where(lo, bb[0], bb[1])[None], (t, S5_GROUP, LANES)).reshape(-1, LANES)
        k256 = lax.dot_general(ctab, bcat, (((1,), (1,)), ((), ())), precision=HIGHEST,
                               preferred_element_type=F32)
        col_blk = lax.broadcasted_iota(jnp.int32, (S5_CW, S5_CW), 1) // S5_GROUP
        acc = jnp.zeros((S5_CW, S5_CW), F32)
        for r in range(t):
            sh = r * S5_GROUP if d == 0 else (t - 1 - r) * S5_GROUP
            if sh == 0:
                shifted = k256
            elif d == 0:
                shifted = jnp.concatenate([jnp.zeros((sh, S5_CW), F32), k256[:S5_CW - sh]], axis=0)
            else:
                shifted = jnp.concatenate([k256[sh:], jnp.zeros((sh, S5_CW), F32)], axis=0)
            acc = acc + jnp.where(col_blk == r, shifted, 0.0)
        mt = acc if mt is None else mt + acc

        for m in (float(t), float(t * n_seg_chunks)):
            re, im = cpow(m)
            tab_rows += [re, sgn * im]
        blk, j = rows(n_seg_chunks // SUBLANES), rows(SUBLANES)
        if d == 0:
            re, im = expand(cpow(t * SUBLANES * blk), cpow(t * j))
        else:
            re, im = expand(cpow(t * SUBLANES * (n_seg_chunks // SUBLANES - 1.0 - blk)), cpow(t * (SUBLANES - 1.0 - j)))
        pow_rows += [re, sgn * im]

    eye = (lax.broadcasted_iota(jnp.int32, (S5_CW, S5_CW), 0) == lax.broadcasted_iota(jnp.int32, (S5_CW, S5_CW), 1))
    mt = mt + jnp.where(eye, dsk_ref[...], 0.0)
    m_ref[...] = mt.T.astype(BF16)
    bp_ref[...] = jnp.concatenate(bp_cols, axis=1).astype(BF16)
    cp_ref[...] = jnp.concatenate(cp_rows, axis=0).astype(BF16)
    tab_ref[...] = jnp.concatenate(tab_rows, axis=0)
    for q in range(4):
        pow_ref[q] = pow_rows[q]


def _s5_prep(lam_re, lam_im, log_dt, b_re, b_im, c_re, c_im, d_skip, n_seg_chunks):
    depth = lam_re.shape[0]
    g, p_, h = S5_GROUPS, S5_STATE, S5_GROUP
    dup = lambda a: jnp.concatenate([a, a], axis=-1)
    dirs_last = lambda a: jnp.moveaxis(a, 1, 2)
    lam = jnp.stack([lam_re, lam_im], axis=3)
    lam = dirs_last(lam).reshape(depth, g, 4, p_)
    ldt = jnp.broadcast_to(dirs_last(log_dt[..., None]), (depth, g, 2, p_))
    par = dup(jnp.concatenate([lam, ldt, jnp.zeros((depth, g, 2, p_), F32)], axis=2))
    bt = jnp.stack([b_re, b_im], axis=3)
    bt = dup(jnp.swapaxes(dirs_last(bt), -1, -2).reshape(depth, g, 4, h, p_))
    ct = dup(dirs_last(jnp.stack([c_re, c_im], axis=3)).reshape(depth, g, 4, h, p_))
    dsk = jnp.tile(d_skip.reshape(depth, g, 1, h), (1, 1, 1, S5_T))

    blk = lambda *shape: pl.BlockSpec((None, None) + shape, lambda l, j: (l, j) + (0,) * len(shape))
    return pl.pallas_call(
        functools.partial(_s5_prep_kernel, n_seg_chunks),
        grid=(depth, g),
        in_specs=[blk(8, LANES), blk(4, h, LANES), blk(4, h, LANES), blk(1, S5_CW)],
        out_specs=[blk(S5_CW, S5_CW), blk(S5_CW, 2 * LANES), blk(2 * LANES, S5_CW), blk(8, LANES),
                   blk(4, n_seg_chunks, LANES)],
        out_shape=[
            jax.ShapeDtypeStruct((depth, g, S5_CW, S5_CW), BF16),
            jax.ShapeDtypeStruct((depth, g, S5_CW, 2 * LANES), BF16),
            jax.ShapeDtypeStruct((depth, g, 2 * LANES, S5_CW), BF16),
            jax.ShapeDtypeStruct((depth, g, 8, LANES), F32),
            jax.ShapeDtypeStruct((depth, g, 4, n_seg_chunks, LANES), F32),
        ],
        name="s5_prep",
    )(par, bt, ct, dsk)


def _cmul_step(s, s_sw, v, v_sw, re, ims):
    return re * s + ims * s_sw + v, re * s_sw - ims * s + v_sw


def _s5_kernel(n_ctx_chunks, n_lat0, ug_ref, m_ref, bp_ref, cp_ref, tab_ref, pow_ref, y_ref,
               vf_ref, vfs_ref, vb_ref, vbs_ref):
    gb, n_rows, _ = ug_ref.shape
    n_seg = (n_rows - n_lat0) // SUBLANES
    pitch = vf_ref.shape[1] // SUBLANES
    half = S5_STATE

    def put_segments(ref, g, val):
        for r in range(SUBLANES):
            ref[g, r * pitch:r * pitch + n_seg, :] = val[r * n_seg:(r + 1) * n_seg]

    tab = tab_ref[...]
    ctx_end = []
    sp_ctx = []
    for g in range(gb):
        v = _dot(ug_ref[g], bp_ref[g])
        vl = v[n_lat0:]
        put_segments(vf_ref, g, vl[:, :LANES])
        put_segments(vb_ref, g, vl[:, LANES:])
        put_segments(vfs_ref, g, pltpu.roll(vl[:, :LANES], half, 1))
        put_segments(vbs_ref, g, pltpu.roll(vl[:, LANES:], half, 1))

        vc = v[:n_ctx_chunks]
        vcs = jnp.concatenate([pltpu.roll(vc[:, :LANES], half, 1), pltpu.roll(vc[:, LANES:], half, 1)], axis=1)
        zero = jnp.zeros((1, LANES), F32)
        sf, sfs, sb, sbs = zero, zero, zero, zero
        rows_f, rows_b = [], [None] * n_ctx_chunks
        for j in range(n_ctx_chunks):
            jb = n_ctx_chunks - 1 - j
            rows_f.append(sf)
            rows_b[jb] = sb
            sf, sfs = _cmul_step(sf, sfs, vc[j:j + 1, :LANES], vcs[j:j + 1, :LANES], tab[g, 0:1], tab[g, 1:2])
            sb, sbs = _cmul_step(sb, sbs, vc[jb:jb + 1, LANES:], vcs[jb:jb + 1, LANES:], tab[g, 4:5], tab[g, 5:6])
        ctx_end.append((sf, sfs, sb, sbs))
        sp_ctx.append(jnp.concatenate([jnp.concatenate(rows_f, axis=0), jnp.concatenate(rows_b, axis=0)], axis=1))

    coef = [[jnp.broadcast_to(tab[g, q:q + 1], (SUBLANES, LANES)) for q in (0, 1, 4, 5)] for g in range(gb)]

    def step(k, carry):
        kb = n_seg - 1 - k
        out = []
        for g in range(gb):
            sf, sfs, sb, sbs = carry[g]
            are_f, aim_f, are_b, aim_b = coef[g]
            at_f = pl.ds(k, SUBLANES, stride=pitch)
            at_b = pl.ds(kb, SUBLANES, stride=pitch)
            v_f, v_fs = vf_ref[g, at_f, :], vfs_ref[g, at_f, :]
            v_b, v_bs = vb_ref[g, at_b, :], vbs_ref[g, at_b, :]
            vf_ref[g, at_f, :] = sf
            vb_ref[g, at_b, :] = sb
            sf, sfs = _cmul_step(sf, sfs, v_f, v_fs, are_f, aim_f)
            sb, sbs = _cmul_step(sb, sbs, v_b, v_bs, are_b, aim_b)
            out.append((sf, sfs, sb, sbs))
        return tuple(out)

    zero8 = jnp.zeros((SUBLANES, LANES), F32)
    ends = lax.fori_loop(0, n_seg, step, tuple((zero8, zero8, zero8, zero8) for _ in range(gb)))

    for g in range(gb):
        ef, efs, eb, ebs = ends[g]
        sf, sfs, sb, sbs = ctx_end[g]
        ent_f, ent_b = [], [None] * SUBLANES
        for r in range(SUBLANES):
            rb = SUBLANES - 1 - r
            ent_f.append((sf, sfs))
            ent_b[rb] = (sb, sbs)
            sf, sfs = _cmul_step(sf, sfs, ef[r:r + 1], efs[r:r + 1], tab[g, 2:3], tab[g, 3:4])
            sb, sbs = _cmul_step(sb, sbs, eb[rb:rb + 1], ebs[rb:rb + 1], tab[g, 6:7], tab[g, 7:8])

        segs = []
        for r in range(SUBLANES):
            seg_rows = slice(r * pitch, r * pitch + n_seg)
            f = vf_ref[g, seg_rows, :] + pow_ref[g, 0] * ent_f[r][0] + pow_ref[g, 1] * ent_f[r][1]
            b = vb_ref[g, seg_rows, :] + pow_ref[g, 2] * ent_b[r][0] + pow_ref[g, 3] * ent_b[r][1]
            segs.append(jnp.concatenate([f, b], axis=1))
        pad = jnp.zeros((n_lat0 - n_ctx_chunks, 2 * LANES), F32)
        sprev = jnp.concatenate([sp_ctx[g], pad] + segs, axis=0).astype(BF16)
        y_ref[g] = _dot(ug_ref[g], m_ref[g]) + _dot(sprev, cp_ref[g])


def _s5_apply(ug, m, bp, cp, tab, pw, n_ctx_chunks, n_lat0):
    n_groups, n_rows, cw = ug.shape
    n_seg = pw.shape[-2]
    assert n_rows - n_lat0 == SUBLANES * n_seg and n_lat0 % SUBLANES == 0
    blk = lambda *shape: pl.BlockSpec((S5_GB,) + shape, lambda i: (i,) + (0,) * len(shape))
    seg = pltpu.VMEM((S5_GB, SUBLANES * (n_seg + SUBLANES), LANES), F32)
    return pl.pallas_call(
        functools.partial(_s5_kernel, n_ctx_chunks, n_lat0),
        grid=(n_groups // S5_GB,),
        in_specs=[blk(n_rows, cw), blk(cw, cw), blk(cw, 2 * LANES), blk(2 * LANES, cw), blk(8, LANES),
                  blk(4, n_seg, LANES)],
        out_specs=blk(n_rows, cw),
        out_shape=jax.ShapeDtypeStruct((n_groups, n_rows, cw), F32),
        scratch_shapes=[seg, seg, seg, seg],
        compiler_params=pltpu.CompilerParams(vmem_limit_bytes=VMEM_LIMIT),
        name="s5_scan",
    )(ug, m, bp, cp, tab, pw)


def _mixer_kernel(n_ctx_tiles, ctx_row_len, first_tile, zl_ref, zc_ref, y_ref, mod_ref, gmix_ref, gffn_ref, win_ref, wglu_ref,
                  lng_ref, lnb_ref, wsg_ref, bsg_ref, wconv_ref, wbr_ref, wmg_ref, bmg_ref, wo_ref, wr_ref,
                  br_ref, z1_ref, xs_ref, route_ref, pc_ref, yscr_ref):
    d = D_MODEL
    n_chunks = TM // S5_T
    for q in range(MIX_W // LANES):
        for half in range(S5_T // SUBLANES):
            cols = [y_ref[q * SUBLANES + b, :, half * LANES:(half + 1) * LANES] for b in range(SUBLANES)]
            for a, rows in enumerate(_block_transpose(cols)):
                yscr_ref[q, pl.ds(half * SUBLANES + a, n_chunks, stride=S5_PITCH), :] = rows
    y_tok = jnp.concatenate(
        [jnp.concatenate([yscr_ref[q, c * S5_PITCH:c * S5_PITCH + S5_T, :] for c in range(n_chunks)], axis=0)
         for q in range(MIX_W // LANES)], axis=1)
    is_ctx = pl.program_id(0) + first_tile < n_ctx_tiles
    z = jnp.where(is_ctx, zc_ref[...], zl_ref[...])
    mod = mod_ref[...]
    sh1, sc1, ga1 = mod[:, 0:d], mod[:, d:2 * d], mod[:, 2 * d:3 * d]
    sh2, sc2 = mod[:, 3 * d:4 * d], mod[:, 4 * d:5 * d]
    h = _rms(z, gmix_ref[...]) * (1.0 + sc1) + sh1
    hb = h.astype(BF16)
    proj = _dot(hb, win_ref[...])
    su, sv, cgb, cgc, chh = [proj[:, k * MIX_W:(k + 1) * MIX_W] for k in range(5)]

    def merge_term(k, y_k):
        gate = _sigmoid(_dot(hb, wmg_ref[:, k * d:(k + 1) * d]) + bmg_ref[:, k * d:(k + 1) * d])
        return gate * _dot(y_k.astype(BF16), wbr_ref[k])

    yg = jax.nn.gelu(y_tok)
    merged = merge_term(0, yg * _sigmoid(_dot(yg.astype(BF16), wglu_ref[...])))

    gv = jax.nn.gelu(sv)
    mu = jnp.mean(gv, axis=-1, keepdims=True)
    var = jnp.mean(jnp.square(gv - mu), axis=-1, keepdims=True)
    vn = ((gv - mu) * lax.rsqrt(var + NORM_EPS) * lng_ref[...] + lnb_ref[...]).astype(BF16)
    bsg = bsg_ref[...]
    chunks = []
    for n in range(TM // SG_CHUNK):
        rows = slice(n * SG_CHUNK, (n + 1) * SG_CHUNK)
        cols = [_dot(wsg_ref[g], vn[rows, g * SG_GROUP_W:(g + 1) * SG_GROUP_W]) for g in range(SG_GROUPS)]
        chunks.append(jnp.concatenate(cols, axis=1) + bsg)
    merged = merged + merge_term(1, jax.nn.gelu(su) * jnp.concatenate(chunks, axis=0))

    zc = cgc * chh
    row = lax.broadcasted_iota(jnp.int32, (TM, 1), 0)
    pos = jnp.where(is_ctx, row % ctx_row_len, row % GRID_W)
    prev = jnp.where(pos == 0, 0.0, pltpu.roll(zc, 1, 0))
    nxt = jnp.where(pos == jnp.where(is_ctx, ctx_row_len - 1, GRID_W - 1), 0.0, pltpu.roll(zc, TM - 1, 0))
    wc = wconv_ref[...]
    merged = merged + merge_term(2, cgb * (wc[0:1] * prev + wc[1:2] * zc + wc[2:3] * nxt))

    z1 = z + ga1 * _dot(merged.astype(BF16), wo_ref[...])
    z1_ref[...] = z1

    h2 = _rms(z1, gffn_ref[...]) * (1.0 + sc2) + sh2
    h2_hi = h2.astype(BF16)
    h2_lo = (h2 - h2_hi.astype(F32)).astype(BF16)
    nt_dims = (((1,), (1,)), ((), ()))
    logits_t = (lax.dot_general(wr_ref[0], h2_hi, nt_dims, preferred_element_type=F32)
                + lax.dot_general(wr_ref[0], h2_lo, nt_dims, preferred_element_type=F32)
                + lax.dot_general(wr_ref[1], h2_hi, nt_dims, preferred_element_type=F32)) + br_ref[...]
    lg = logits_t[0:ROUTER_ROWS]
    row = lax.broadcasted_iota(jnp.int32, (ROUTER_ROWS, TM), 0)
    big = jnp.int32(LANES)
    neg = jnp.float32(-jnp.inf)
    top = lambda v: jnp.max(v, axis=0, keepdims=True)
    total = lambda v: jnp.sum(v, axis=0, keepdims=True)
    first = lambda m: jnp.min(jnp.where(m, row, big), axis=0, keepdims=True)
    is_grp = (row >= N_EXPERTS) & (row < N_EXPERTS + N_EXP_GROUPS)
    gl = jnp.where(is_grp, lg, neg)
    ge = jnp.exp(gl - top(gl))
    gp = ge / total(ge)
    g_p = top(gp)
    g_idx = first(is_grp & (gp == g_p)) - N_EXPERTS
    in_grp = (row >= g_idx * EXPERTS_PER_GROUP) & (row < (g_idx + 1) * EXPERTS_PER_GROUP)
    el = jnp.where(in_grp, lg, neg)
    ee = jnp.exp(el - top(el))
    ep = ee / total(ee)
    p1 = top(ep)
    i1 = first(in_grp & (ep == p1))
    rest = in_grp & (row != i1)
    ep2 = jnp.where(rest, ep, -1.0)
    p2 = top(ep2)
    i2 = first(rest & (ep2 == p2))
    tot = p1 + p2
    w_1, w_2 = g_p * (p1 / tot), g_p * (p2 / tot)

    oh1, oh2 = row == i1, row == i2
    oh = jnp.where(oh1 | oh2, 1.0, 0.0)
    cnt = jnp.sum(oh, axis=1, keepdims=True)
    pcb = jnp.floor((cnt + (MOE_BLK - 1)) * (1.0 / MOE_BLK))
    er, ec = (lax.broadcasted_iota(jnp.int32, (ROUTER_ROWS, ROUTER_ROWS), k) for k in (0, 1))
    before = jnp.where(ec < er, 1.0, 0.0).astype(BF16)
    startb = _dot(before, jnp.broadcast_to(pcb, (ROUTER_ROWS, LANES)).astype(BF16))[:, 0:1]
    start = startb * MOE_BLK
    plan_lane = lax.broadcasted_iota(jnp.int32, (ROUTER_ROWS, LANES), 1)
    pc_ref[...] = jnp.where(plan_lane == 0, pcb, jnp.where(plan_lane == 1, startb, 0.0)).astype(jnp.int32)
    tr, tc = (lax.broadcasted_iota(jnp.int32, (TM, TM), k) for k in (0, 1))
    earlier = jnp.where(tr < tc, 1.0, 0.0).astype(BF16)
    slot = start + _dot(oh.astype(BF16), earlier)
    slot1 = total(jnp.where(oh1, slot, 0.0))
    slot2 = total(jnp.where(oh2, slot, 0.0))

    def token_rows(vals):
        r8 = lax.broadcasted_iota(jnp.int32, (SUBLANES, TM), 0)
        rows8 = jnp.zeros((SUBLANES, TM), F32)
        for k, v in enumerate(vals):
            rows8 = jnp.where(r8 == k, v, rows8)
        return jnp.concatenate([rows8, jnp.zeros((LANES - SUBLANES, TM), F32)], axis=0).T

    route_ref[...] = token_rows([slot1, slot2])
    sl = xs_ref.shape[0]
    slot_row = lax.broadcasted_iota(jnp.int32, (sl, TM), 0)
    perm1 = jnp.where(slot_row == slot1.astype(jnp.int32), 1.0, 0.0).astype(BF16)
    perm2 = jnp.where(slot_row == slot2.astype(jnp.int32), 1.0, 0.0).astype(BF16)
    xs_ref[:, 0:d] = _dot(perm1 + perm2, h2_hi).astype(BF16)

    def weight_lanes(w):
        hi = w.astype(BF16).astype(F32)
        return token_rows([hi, w - hi]).astype(BF16)

    xs_ref[:, d:] = (_dot(perm1, weight_lanes(w_1)) + _dot(perm2, weight_lanes(w_2))).astype(BF16)


def _mixer(z_lat, z_ctx, lat_off, n, yg, mod, p, n_ctx_tiles, ctx_row_len, first_tile):
    d = D_MODEL
    nt = n // TM - first_tile
    tile = lambda w: pl.BlockSpec((TM, w), lambda i: (i, 0))
    return pl.pallas_call(
        functools.partial(_mixer_kernel, n_ctx_tiles, ctx_row_len, first_tile),
        grid=(nt,),
        in_specs=[
            *_token_specs(n_ctx_tiles, lat_off, first_tile),
            pl.BlockSpec((S5_GROUPS, TM // S5_T, S5_CW), lambda i: (0, i + first_tile, 0)),
            _mod_spec(n_ctx_tiles, first_tile),
            _const_spec((1, d)), _const_spec((1, d)),
            _const_spec((d, 5 * MIX_W)), _const_spec((MIX_W, MIX_W)),
            _const_spec((1, MIX_W)), _const_spec((1, MIX_W)),
            _const_spec((SG_GROUPS, SG_CHUNK, SG_CHUNK)), _const_spec((SG_CHUNK, MIX_W)),
            _const_spec((CONV_K, MIX_W)), _const_spec((N_BRANCH, MIX_W, d)),
            _const_spec((d, N_BRANCH * d)), _const_spec((1, N_BRANCH * d)),
            _const_spec((d, d)), _const_spec((2, LANES, d)), _const_spec((LANES, 1)),
        ],
        out_specs=[tile(d), pl.BlockSpec((None, MOE_SLOTS, MOE_W), lambda i: (i, 0, 0)), tile(LANES),
                   pl.BlockSpec((None, ROUTER_ROWS, LANES), lambda i: (i, 0, 0))],
        out_shape=[
            jax.ShapeDtypeStruct((nt * TM, d), F32),
            jax.ShapeDtypeStruct((nt, MOE_SLOTS, MOE_W), BF16),
            jax.ShapeDtypeStruct((nt * TM, LANES), F32),
            jax.ShapeDtypeStruct((nt, ROUTER_ROWS, LANES), jnp.int32),
        ],
        scratch_shapes=[pltpu.VMEM((MIX_W // LANES, TM // S5_T * S5_PITCH, LANES), F32)],
        compiler_params=pltpu.CompilerParams(vmem_limit_bytes=VMEM_LIMIT),
        name="mixer",
    )(z_lat, z_ctx, yg, mod, p["g_mix"], p["g_ffn"], p["w_in5"], p["w_glu"], p["sg_ln_g"], p["sg_ln_b"],
      p["w_sg"], p["b_sg_full"], p["w_conv"], p["w_branch"], p["w_merge"], p["b_merge"], p["w_o"],
      p["w_router"], p["b_router"])


def _plan_kernel(nt, n_work, nb_ref, sb_ref, src_ref, exp_ref):
    bpt = MOE_SLOTS // MOE_BLK

    def no_block(b, c):
        src_ref[b] = -1
        return c

    def expert(e, g0):
        def tile(i, pos):
            nb = nb_ref[i, e]
            s0 = i * bpt + sb_ref[i, e]
            base = g0 * MOE_NB + pos

            def blk(b, c):
                src_ref[base + b] = s0 + b
                return c

            for b in range(PLAN_SLACK):
                src_ref[base + b] = s0 + b
            lax.fori_loop(PLAN_SLACK, nb, blk, 0)
            return pos + nb

        def tiles(j, pos):
            for k in range(PLAN_UNROLL):
                pos = tile(j * PLAN_UNROLL + k, pos)
            return pos

        pos = lax.fori_loop(0, nt // PLAN_UNROLL, tiles, 0)
        for i in range(nt - nt % PLAN_UNROLL, nt):
            pos = tile(i, pos)
        n_tiles = (pos + MOE_NB - 1) // MOE_NB
        lax.fori_loop(g0 * MOE_NB + pos, (g0 + n_tiles) * MOE_NB, no_block, 0)

        def owner(t, c):
            exp_ref[g0 + t] = e
            return c

        lax.fori_loop(0, n_tiles, owner, 0)
        return g0 + n_tiles

    g_end = lax.fori_loop(0, N_EXPERTS, expert, 0)
    lax.fori_loop(g_end * MOE_NB, n_work * MOE_NB + PLAN_SLACK, no_block, 0)

    def idle(g, c):
        exp_ref[g] = N_EXPERTS - 1
        return c

    lax.fori_loop(g_end, n_work, idle, 0)


def _moe_plan(nb, sb, n_work):
    nt = nb.shape[0]
    smem = pl.BlockSpec(memory_space=pltpu.SMEM)
    return pl.pallas_call(
        functools.partial(_plan_kernel, nt, n_work),
        in_specs=[smem, smem],
        out_specs=[smem, smem],
        out_shape=[jax.ShapeDtypeStruct((n_work * MOE_NB + PLAN_SLACK,), jnp.int32),
                   jax.ShapeDtypeStruct((n_work,), jnp.int32)],
        name="moe_plan",
    )(nb, sb)


def _expert_kernel(src_ref, exp_ref, xs_hbm, w1_ref, w3_ref, w2_ref, yt_hbm, lhs_ref, out_ref, wb1_ref, wb3_ref,
                   wb2_ref, sem_in, sem_out):
    g = pl.program_id(0)
    n_work = pl.num_programs(0)

    def first(step):
        return src_ref[step * MOE_NB]

    def last(step):
        return src_ref[step * MOE_NB + MOE_NB - 1]

    def gather_copy(step, b):
        return pltpu.make_async_copy(xs_hbm.at[src_ref[step * MOE_NB + b]], lhs_ref.at[step % 2, b],
                                     sem_in.at[step % 2])

    def scatter_copy(step, b):
        return pltpu.make_async_copy(out_ref.at[step % 2, b], yt_hbm.at[src_ref[step * MOE_NB + b]],
                                     sem_out.at[step % 2])

    def gather_all(step):
        return pltpu.make_async_copy(xs_hbm.at[pl.ds(0, MOE_NB)], lhs_ref.at[step % 2], sem_in.at[step % 2])

    def scatter_all(step):
        return pltpu.make_async_copy(out_ref.at[step % 2], yt_hbm.at[pl.ds(0, MOE_NB)], sem_out.at[step % 2])

    def per_block(step, fn):
        def body(b, c):
            @pl.when(src_ref[step * MOE_NB + b] >= 0)
            def _():
                fn(b)
            return c
        lax.fori_loop(0, MOE_NB, body, 0)

    def start(step, copy):
        @pl.when(last(step) >= 0)
        def _():
            for b in range(MOE_NB):
                copy(step, b).start(priority=b % 2)

        @pl.when((last(step) < 0) & (first(step) >= 0))
        def _():
            per_block(step, lambda b: copy(step, b).start())

    def wait(step, copy, copy_all):
        @pl.when(last(step) >= 0)
        def _():
            copy_all(step).wait()

        @pl.when((last(step) < 0) & (first(step) >= 0))
        def _():
            per_block(step, lambda b: copy(step, b).wait())

    @pl.when(g == 0)
    def _():
        start(g, gather_copy)

    @pl.when(g + 1 < n_work)
    def _():
        start(g + 1, gather_copy)

    wait(g, gather_copy, gather_all)

    @pl.when(g >= 2)
    def _():
        wait(g - 2, scatter_copy, scatter_all)

    slot = g % 2

    @pl.when(first(g) >= 0)
    def _():
        @pl.when((g == 0) | (exp_ref[g] != exp_ref[jnp.maximum(g - 1, 0)]))
        def _():
            wb1_ref[...] = w1_ref[...].astype(BF16)
            wb3_ref[...] = w3_ref[...].astype(BF16)
            wb2_ref[...] = w2_ref[...].astype(BF16)

        @pl.when(last(g) < 0)
        def _():
            def zero_missing(b, c):
                @pl.when(src_ref[g * MOE_NB + b] < 0)
                def _():
                    lhs_ref[slot, b] = jnp.zeros((MOE_BLK, MOE_W), BF16)
                return c
            lax.fori_loop(0, MOE_NB, zero_missing, 0)

        rows = lhs_ref[slot].reshape(MOE_TE, MOE_W)
        x = rows[:, :D_MODEL]
        w_lanes = rows[:, D_MODEL:].astype(F32)
        w_row = w_lanes[:, 0:1] + w_lanes[:, 1:2]
        a = _dot(x, wb1_ref[...])
        act = a * _sigmoid(a) * _dot(x, wb3_ref[...])
        y = (_dot(act.astype(BF16), wb2_ref[...]) * w_row).astype(BF16)
        out_ref[slot] = jnp.concatenate([y, jnp.zeros((MOE_TE, LANES), BF16)], axis=1).reshape(MOE_NB, MOE_BLK, MOE_W)

    start(g, scatter_copy)

    @pl.when(g == n_work - 1)
    def _():
        @pl.when(g >= 1)
        def _():
            wait(g - 1, scatter_copy, scatter_all)
        wait(g, scatter_copy, scatter_all)


def _moe_experts(xs, src, owner, w1, w3, w2, layer):
    nt, sl, width = xs.shape
    d = D_MODEL
    n_work = owner.shape[0]
    xs_blocks = xs.reshape(nt * sl // MOE_BLK, MOE_BLK, width)
    any_spec = pl.BlockSpec(memory_space=pl.ANY)
    by_owner = lambda *shape: pl.BlockSpec((None, None) + shape, lambda g, src, own: (layer, own[g], 0, 0))
    buf = pltpu.VMEM((2, MOE_NB, MOE_BLK, width), BF16)
    yt = pl.pallas_call(
        _expert_kernel,
        grid_spec=pltpu.PrefetchScalarGridSpec(
            num_scalar_prefetch=2,
            grid=(n_work,),
            in_specs=[any_spec, by_owner(d, D_FF_EXPERT), by_owner(d, D_FF_EXPERT), by_owner(D_FF_EXPERT, d)],
            out_specs=any_spec,
            scratch_shapes=[
                buf, buf,
                pltpu.VMEM((d, D_FF_EXPERT), BF16), pltpu.VMEM((d, D_FF_EXPERT), BF16),
                pltpu.VMEM((D_FF_EXPERT, d), BF16),
                pltpu.SemaphoreType.DMA((2,)), pltpu.SemaphoreType.DMA((2,)),
            ],
        ),
        out_shape=jax.ShapeDtypeStruct(xs_blocks.shape, BF16),
        input_output_aliases={2: 0},
        compiler_params=pltpu.CompilerParams(dimension_semantics=("arbitrary",), vmem_limit_bytes=VMEM_LIMIT),
        name="moe_experts",
    )(src, owner, xs_blocks, w1, w3, w2)
    return yt.reshape(nt, sl, width)


def _combine_kernel(final, yt_ref, route_ref, z1_ref, mod_ref, g_ref, *rest):
    o_ref = rest[0] if final else rest[3]
    route = route_ref[...]
    slot1, slot2 = route[:, 0:1], route[:, 1:2]
    slot_lane = lax.broadcasted_iota(jnp.int32, (TM, MOE_SLOTS), 1).astype(F32)
    both = jnp.where((slot_lane == slot1) | (slot_lane == slot2), 1.0, 0.0).astype(BF16)
    moe = _dot(both, yt_ref[:, 0:D_MODEL])
    ga2 = mod_ref[...][:, 5 * D_MODEL:6 * D_MODEL]
    z2 = z1_ref[...] + ga2 * moe
    if final:
        o_ref[...] = _rms(z2, g_ref[...])
    else:
        mod_next, gmix_next, wu_next, _, ug_ref, scr_ref = rest
        o_ref[...] = z2
        _u_group_major(z2, mod_next[...], gmix_next[...], wu_next, ug_ref, scr_ref)


def _combine(yt, route, z1, mod, g_final, n_ctx_tiles, next_layer=None):
    n, d = z1.shape
    final = next_layer is None
    tile = lambda w: pl.BlockSpec((TM, w), lambda i: (i, 0))
    in_specs = [pl.BlockSpec((None, MOE_SLOTS, MOE_W), lambda i: (i, 0, 0)), tile(LANES), tile(d),
                _mod_spec(n_ctx_tiles), _const_spec((1, d))]
    out_specs, out_shape, scratch = tile(d), jax.ShapeDtypeStruct((n, d), F32), []
    if not final:
        in_specs += [_mod_spec(n_ctx_tiles), _const_spec((1, d)), _const_spec((d, MIX_W))]
        out_specs = [out_specs, pl.BlockSpec((S5_GROUPS, TM // S5_T, S5_CW), lambda i: (0, i, 0))]
        out_shape = [out_shape, jax.ShapeDtypeStruct((S5_GROUPS, n // S5_T, S5_CW), BF16)]
        scratch = [pltpu.VMEM((MIX_W // LANES, TM // S5_T * S5_PITCH, LANES), F32)]
    return pl.pallas_call(
        functools.partial(_combine_kernel, final),
        grid=(n // TM,),
        in_specs=in_specs,
        out_specs=out_specs,
        out_shape=out_shape,
        scratch_shapes=scratch,
        compiler_params=pltpu.CompilerParams(vmem_limit_bytes=VMEM_LIMIT),
        name="combine",
    )(yt, route, z1, mod, g_final, *(next_layer or ()))


def kernel(x, c, ctx, c_ctx, w_ada, b_ada, g_mix, g_ffn, w_in, lam_re, lam_im, log_dt, b_re, b_im, c_re, c_im, d_skip, w_glu, sg_ln_g, sg_ln_b, w_sg, b_sg, w_conv, w_branch, w_merge, b_merge, w_o, w_grp, b_grp, w_exp, b_exp, w1, w3, w2, g_final):
    depth = w_ada.shape[0]
    bsz, seq, d = x.shape
    n_ctx = ctx.shape[1]
    ctx_rows = -(-n_ctx // CTX_ALIGN) * CTX_ALIGN
    n = ctx_rows + seq
    assert bsz == 1 and d == D_MODEL
    assert ctx_rows % TM == 0 and seq % TM == 0 and (TM % n_ctx == 0 or n_ctx % TM == 0)
    assert n_ctx % S5_T == 0 and seq % (S5_T * SUBLANES * SUBLANES) == 0
    n_ctx_tiles = ctx_rows // TM
    n_seg_chunks = seq // (S5_T * SUBLANES)
    nt = n // TM

    z_lat, z_ctx, lat_off = x[0], jnp.pad(ctx[0], ((0, ctx_rows - n_ctx), (0, 0))), n_ctx_tiles
    cc = jnp.zeros((8, d), F32).at[0].set(c_ctx).at[1].set(c[0])
    mods = _ada(cc, w_ada, b_ada)[:, :2].reshape(depth, 2, 1, 6 * d)
    s5_m, s5_bp, s5_cp, s5_tab, s5_pow = _s5_prep(lam_re, lam_im, log_dt, b_re, b_im, c_re, c_im, d_skip, n_seg_chunks)

    for l in range(depth):
        pad = LANES - N_EXPERTS - N_EXP_GROUPS
        p = dict(
            g_mix=g_mix[l].reshape(1, d), g_ffn=g_ffn[l].reshape(1, d),
            w_in5=w_in[l][:, MIX_W:].astype(BF16), w_glu=w_glu[l].astype(BF16),
            sg_ln_g=sg_ln_g[l].reshape(1, MIX_W), sg_ln_b=sg_ln_b[l].reshape(1, MIX_W),
            w_sg=w_sg[l].astype(BF16),
            b_sg_full=jnp.repeat(b_sg[l].T, SG_GROUP_W, axis=1),
            w_conv=w_conv[l], w_branch=w_branch[l].astype(BF16), w_merge=w_merge[l].astype(BF16),
            b_merge=b_merge[l].reshape(1, N_BRANCH * d), w_o=w_o[l].astype(BF16),
            w_router=_split_bf16(jnp.pad(jnp.concatenate([w_exp[l], w_grp[l]], axis=1).T, ((0, pad), (0, 0)))),
            b_router=jnp.pad(jnp.concatenate([b_exp[l], b_grp[l]]), (0, pad)).reshape(LANES, 1),
        )
        mod = mods[l]

        last = l == depth - 1
        first_tile = n_ctx_tiles if last else 0
        if l == 0:
            ug = _u_proj(z_lat, z_ctx, lat_off, n, mod, p["g_mix"], w_in[l][:, :MIX_W].astype(BF16), n_ctx_tiles)
        yg = _s5_apply(ug, s5_m[l], s5_bp[l], s5_cp[l], s5_tab[l], s5_pow[l], n_ctx // S5_T, ctx_rows // S5_T)
        z1, xs, route, pc = _mixer(z_lat, z_ctx, lat_off, n, yg, mod, p, n_ctx_tiles, min(n_ctx, TM), first_tile)
        n_work = (nt - first_tile) * MOE_SLOTS // MOE_TE + N_EXPERTS
        src, owner = _moe_plan(pc[:, :N_EXPERTS, 0], pc[:, :N_EXPERTS, 1], n_work)
        yt = _moe_experts(xs, src, owner, w1, w3, w2, l)
        if last:
            z = _combine(yt, route, z1, mod, g_final.reshape(1, d), n_ctx_tiles - first_tile)
        else:
            nxt = (mods[l + 1], g_mix[l + 1].reshape(1, d), w_in[l + 1][:, :MIX_W].astype(BF16))
            z, ug = _combine(yt, route, z1, mod, p["g_ffn"], n_ctx_tiles, nxt)
        z_lat, z_ctx, lat_off = z, z, 0

    return z.reshape(bsz, seq, d)
```
